```python
import math
import jax, jax.numpy as jnp
from jax import lax
import numpy as np

D_MODEL = 2048
BATCH = 4
SEQ = 2048
DEPTH = 4

CTX_LEN = 256
GRID_W = 64

DA_HEADS = 4
DA_QK = 64
DA_V = 2 * DA_QK
ML_HEADS = 4
ML_DIM = 128
ML_CHUNK = 64
ML_CONV = 3
MLA_HEADS = 8
MLA_NOPE = 64
MLA_ROPE = 32
MLA_V = 64
MLA_Q_RANK = 384
MLA_KV_RANK = 128
NA_HEADS = 8
NA_DIM = 64
NA_WIN_R = 8
NA_WIN_C = 16
NA_QBLK_R = 2
NA_QBLK_C = 16

MIX_WIDTH = DA_HEADS * DA_V + ML_HEADS * ML_DIM + MLA_HEADS * MLA_V + NA_HEADS * NA_DIM
IN_SPLITS = (
    DA_HEADS * 2 * DA_QK, DA_HEADS * 2 * DA_QK, DA_HEADS * DA_V,
    ML_HEADS * ML_DIM, ML_HEADS * ML_DIM, ML_HEADS * ML_DIM, ML_HEADS * ML_DIM,
    4 * ML_HEADS,
    MLA_Q_RANK, MLA_KV_RANK, MLA_ROPE,
    NA_HEADS * NA_DIM, NA_HEADS * NA_DIM, NA_HEADS * NA_DIM,
)
D_IN = sum(IN_SPLITS)
IN_OFFSETS = tuple(sum(IN_SPLITS[:i + 1]) for i in range(len(IN_SPLITS) - 1))

D_FF = ((8 * D_MODEL // 3 + 255) // 256) * 256
N_EXPERTS = 8
TOP_K = 2
N_DENSE = (DEPTH + 1) // 2
N_MOE = DEPTH // 2
ATTN_QBLK = 128
ROPE_BASE = 10000.0
RMS_EPS = 1e-6
F32 = jnp.float32

kernel_name = "hybrid_headgroup_diffusion_block"


def rms_norm(x, gain=None):
    xf = x.astype(F32)
    y = xf * lax.rsqrt(jnp.mean(xf * xf, axis=-1, keepdims=True) + RMS_EPS)
    if gain is not None:
        y = y * gain.astype(F32)
    return y.astype(x.dtype)


def adaln(x, shift, scale):
    return rms_norm(x) * (1.0 + scale) + shift


def axial_rope_tables(n_tokens, rot_dim):
    t = jnp.arange(n_tokens)
    row = (t // GRID_W).astype(F32)
    col = (t % GRID_W).astype(F32)
    axis_dim = rot_dim // 2
    inv_freq = ROPE_BASE ** (-jnp.arange(0, axis_dim, 2, dtype=F32) / axis_dim)
    ang_r = row[:, None] * inv_freq
    ang_c = col[:, None] * inv_freq
    ang = jnp.concatenate([ang_r, ang_r, ang_c, ang_c], axis=-1)
    return jnp.cos(ang), jnp.sin(ang)


def rotate_half(u):
    u1, u2 = jnp.split(u, 2, axis=-1)
    return jnp.concatenate([-u2, u1], axis=-1)


def apply_axial_rope(x, cos, sin):
    xr, xc = jnp.split(x, 2, axis=-1)
    rot = jnp.concatenate([rotate_half(xr), rotate_half(xc)], axis=-1)
    return (x.astype(F32) * cos + rot.astype(F32) * sin).astype(x.dtype)


def softmax_attend(q, k, v, scale):
    s = jnp.einsum('bhqd,bhkd->bhqk', q, k).astype(F32) * scale
    p = jax.nn.softmax(s, axis=-1)
    return jnp.einsum('bhqk,bhkd->bhqd', p.astype(v.dtype), v)


def sweep_query_blocks(fn, *qs):
    B, H, N = qs[0].shape[:3]
    nb = N // ATTN_QBLK
    blocks = tuple(q.reshape(B, H, nb, ATTN_QBLK, q.shape[-1]).transpose(2, 0, 1, 3, 4) for q in qs)
    out = lax.map(lambda blk: fn(*blk), blocks)
    return out.transpose(1, 2, 0, 3, 4).reshape(B, H, N, out.shape[-1])


def short_conv(x, w, b):
    pad = ML_CONV // 2
    y = lax.conv_general_dilated(x, w[:, None, :].astype(x.dtype), window_strides=(1,),
                                 padding=[(pad, pad)], dimension_numbers=('NWC', 'WIO', 'NWC'),
                                 feature_group_count=x.shape[-1])
    return y + b


def diff_lambda_init(layer_idx):
    return 0.8 - 0.6 * math.exp(-0.3 * layer_idx)


def mixer_diff(pc, pl, q_gain, k_gain, lam_params, out_gain, lam_init, rope, ctx_out):
    cos, sin = rope
    scale = DA_QK ** -0.5

    def qk_heads(a, gain):
        B, T, _ = a.shape
        return rms_norm(a.reshape(B, T, DA_HEADS, 2, DA_QK), gain).transpose(3, 0, 2, 1, 4)

    def v_heads(a):
        B, T, _ = a.shape
        return a.reshape(B, T, DA_HEADS, DA_V).transpose(0, 2, 1, 3)

    def merge(o):
        B, H, T, dv = o.shape
        o = rms_norm(o, out_gain) * (1.0 - lam_init)
        return o.transpose(0, 2, 1, 3).reshape(B, T, H * dv)

    lq1, lk1, lq2, lk2 = lam_params.astype(F32)
    lam = jnp.exp(jnp.sum(lq1 * lk1)) - jnp.exp(jnp.sum(lq2 * lk2)) + lam_init

    def diff_attend(q1, q2, k1, k2, v):
        s1 = jnp.einsum('bhqd,bhkd->bhqk', q1, k1).astype(F32) * scale
        s2 = jnp.einsum('bhqd,bhkd->bhqk', q2, k2).astype(F32) * scale
        p = jax.nn.softmax(s1, axis=-1) - lam * jax.nn.softmax(s2, axis=-1)
        return jnp.einsum('bhqk,bhkd->bhqd', p.astype(v.dtype), v)

    q_c, k_c, v_c = pc
    q_l, k_l, v_l = pl
    k_ctx = qk_heads(k_c, k_gain)
    v_ctx = v_heads(v_c)
    q_lat = apply_axial_rope(qk_heads(q_l, q_gain), cos, sin)
    k_lat = apply_axial_rope(qk_heads(k_l, k_gain), cos, sin)
    k_all = jnp.concatenate([k_lat, k_ctx], axis=3)
    v_all = jnp.concatenate([v_heads(v_l), v_ctx], axis=2)
    out_l = merge(sweep_query_blocks(
        lambda a, b: diff_attend(a, b, k_all[0], k_all[1], v_all), q_lat[0], q_lat[1]))
    out_c = None
    if ctx_out:
        q_ctx = qk_heads(q_c, q_gain)
        out_c = merge(diff_attend(q_ctx[0], q_ctx[1], k_ctx[0], k_ctx[1], v_ctx))
    return out_c, out_l


def mlstm_chunk_scan(q, k, v, ig, fg, state, with_out):
    B, H, T, d = q.shape
    L = ML_CHUNK
    nc = T // L

    def chunks(a):
        return jnp.moveaxis(a.reshape(a.shape[:2] + (nc, L) + a.shape[3:]), 2, 0)

    xs = tuple(chunks(a) for a in (q, k, v, ig, fg))
    tril = jnp.tril(jnp.ones((L, L), dtype=bool))

    def body(carry, inp):
        C, n, m = carry
        qc, kc, vc, ic, fc = inp
        b = jnp.cumsum(jax.nn.log_sigmoid(fc), axis=-1)
        b_end = b[..., -1]
        w_end = b_end[..., None] - b + ic
        m_new = jnp.maximum(b_end + m, jnp.max(w_end, axis=-1))
        decay = jnp.exp(b_end + m - m_new)
        we = jnp.exp(w_end - m_new[..., None])
        C_new = decay[..., None, None] * C + jnp.einsum('bhs,bhsd,bhse->bhde', we, vc, kc)
        n_new = decay[..., None] * n + jnp.einsum('bhs,bhse->bhe', we, kc)
        h = None
        if with_out:
            log_w = jnp.where(tril, b[..., :, None] - b[..., None, :] + ic[..., None, :], -jnp.inf)
            inter = b + m[..., None]
            m_t = jnp.maximum(inter, jnp.max(log_w, axis=-1))
            s = jnp.einsum('bhte,bhse->bhts', qc, kc) * jnp.exp(log_w - m_t[..., None])
            a = jnp.exp(inter - m_t)
            num = a[..., None] * jnp.einsum('bhde,bhte->bhtd', C, qc) + jnp.einsum('bhts,bhsd->bhtd', s, vc)
            den = a * jnp.einsum('bhe,bhte->bht', n, qc) + jnp.sum(s, axis=-1)
            h = num / jnp.maximum(jnp.abs(den), jnp.exp(-m_t))[..., None]
        return (C_new, n_new, m_new), h

    state, hs = lax.scan(body, state, xs)
    h = jnp.moveaxis(hs, 0, 2).reshape(B, H, T, d) if with_out else None
    return h, state


def mixer_mlstm(pc, pl, conv_w, conv_b, gate_b, out_gain, ctx_out):
    def prep(q, k, v, o, g):
        B, T, _ = q.shape
        qk = jax.nn.silu(short_conv(jnp.concatenate([q, k], axis=-1), conv_w, conv_b))
        q, k = jnp.split(qk, 2, axis=-1)
        heads = lambda a: a.reshape(B, T, ML_HEADS, ML_DIM).transpose(0, 2, 1, 3).astype(F32)
        g = (g.reshape(B, T, 2, 2, ML_HEADS) + gate_b).astype(F32).transpose(2, 3, 0, 4, 1)
        return heads(q), heads(k) * ML_DIM ** -0.5, heads(v), o, g

    qc, kc, vc, oc, gc = prep(*pc)
    ql, kl, vl, ol, gl = prep(*pl)
    B = ql.shape[0]
    zero = (jnp.zeros((B, ML_HEADS, ML_DIM, ML_DIM), F32), jnp.zeros((B, ML_HEADS, ML_DIM), F32),
            jnp.zeros((B, ML_HEADS), F32))
    flip = lambda a: jnp.flip(a, axis=2)
    hf_c, st_f = mlstm_chunk_scan(qc, kc, vc, gc[0, 0], gc[0, 1], zero, ctx_out)
    hf_l, _ = mlstm_chunk_scan(ql, kl, vl, gl[0, 0], gl[0, 1], st_f, True)
    hb_c, st_b = mlstm_chunk_scan(flip(qc), flip(kc), flip(vc), flip(gc[1, 0]), flip(gc[1, 1]), zero, ctx_out)
    hb_l, _ = mlstm_chunk_scan(flip(ql), flip(kl), flip(vl), flip(gl[1, 0]), flip(gl[1, 1]), st_b, True)

    def finish(h, o):
        B_, H, T, d = h.shape
        h = rms_norm(h.transpose(0, 2, 1, 3), out_gain).reshape(B_, T, H * d)
        return (h * jax.nn.sigmoid(o.astype(F32))).astype(o.dtype)

    out_l = finish(hf_l + flip(hb_l), ol)
    out_c = finish(hf_c + flip(hb_c), oc) if ctx_out else None
    return out_c, out_l


def mixer_mla(pc, pl, cq_gain, ckv_gain, w_uq, w_ukv, q_gain, k_gain, rope, ctx_out):
    cos, sin = rope
    scale = (MLA_NOPE + MLA_ROPE) ** -0.5

    def rope_tail(a):
        return jnp.concatenate([a[..., :MLA_NOPE], apply_axial_rope(a[..., MLA_NOPE:], cos, sin)], axis=-1)

    def q_heads(cq):
        B, T, _ = cq.shape
        q = (rms_norm(cq, cq_gain) @ w_uq).reshape(B, T, MLA_HEADS, MLA_NOPE + MLA_ROPE)
        return rms_norm(q, q_gain).transpose(0, 2, 1, 3)

    def kv_heads(ckv, kr):
        B, T, _ = ckv.shape
        kv = (rms_norm(ckv, ckv_gain) @ w_ukv).reshape(B, T, MLA_HEADS, MLA_NOPE + MLA_V)
        k_nope, v = jnp.split(kv, [MLA_NOPE], axis=-1)
        k_rope = jnp.broadcast_to(kr[:, :, None, :], (B, T, MLA_HEADS, MLA_ROPE))
        k = rms_norm(jnp.concatenate([k_nope, k_rope], axis=-1), k_gain)
        return k.transpose(0, 2, 1, 3), v.transpose(0, 2, 1, 3)

    def merge(o):
        B, H, T, dv = o.shape
        return o.transpose(0, 2, 1, 3).reshape(B, T, H * dv)

    cq_c, ckv_c, kr_c = pc
    cq_l, ckv_l, kr_l = pl
    k_ctx, v_ctx = kv_heads(ckv_c, kr_c)
    k_lat, v_lat = kv_heads(ckv_l, kr_l)
    q_lat = rope_tail(q_heads(cq_l))
    k_all = jnp.concatenate([rope_tail(k_lat), k_ctx], axis=2)
    v_all = jnp.concatenate([v_lat, v_ctx], axis=2)
    out_l = merge(sweep_query_blocks(lambda q: softmax_attend(q, k_all, v_all, scale), q_lat))
    out_c = merge(softmax_attend(q_heads(cq_c), k_ctx, v_ctx, scale)) if ctx_out else None
    return out_c, out_l


def neighbourhood_attend(q, k, v, k_ctx, v_ctx, rpb):
    B, H, N, d = q.shape
    rows = N // GRID_W
    win_r = min(NA_WIN_R, rows)
    q_r = min(NA_QBLK_R, rows)
    key_r = min(q_r + win_r, rows)
    key_c = min(NA_QBLK_C + NA_WIN_C, GRID_W)
    n_rb, n_cb = rows // q_r, GRID_W // NA_QBLK_C
    q_rows = np.arange(n_rb)[:, None] * q_r + np.arange(q_r)
    q_cols = np.arange(n_cb)[:, None] * NA_QBLK_C + np.arange(NA_QBLK_C)
    key_rows = np.clip(np.arange(n_rb) * q_r - win_r // 2, 0, rows - key_r)[:, None] + np.arange(key_r)
    key_cols = np.clip(np.arange(n_cb) * NA_QBLK_C - NA_WIN_C // 2, 0, GRID_W - key_c)[:, None] + np.arange(key_c)
    r0 = np.clip(q_rows - win_r // 2, 0, rows - win_r)[:, :, None]
    c0 = np.clip(q_cols - NA_WIN_C // 2, 0, GRID_W - NA_WIN_C)[:, :, None]
    kr = key_rows[:, None, :]
    kcol = key_cols[:, None, :]
    ok_r = (kr >= r0) & (kr < r0 + win_r)
    ok_c = (kcol >= c0) & (kcol < c0 + NA_WIN_C)
    rel_r = np.clip(kr - q_rows[:, :, None] + NA_WIN_R - 1, 0, 2 * NA_WIN_R - 2)
    rel_c = np.clip(kcol - q_cols[:, :, None] + NA_WIN_C - 1, 0, 2 * NA_WIN_C - 2)
    mask = ok_r[:, None, :, None, :, None] & ok_c[None, :, None, :, None, :]
    bias = rpb[:, rel_r[:, None, :, None, :, None], rel_c[None, :, None, :, None, :]]
    qb_n, kb_n = q_r * NA_QBLK_C, key_r * key_c
    bias = jnp.where(mask, bias.astype(F32), -jnp.inf).reshape(H, n_rb, n_cb, qb_n, kb_n)

    def gather(a):
        g = a.reshape(B, H, rows, GRID_W, d)[:, :, key_rows[:, None, :, None], key_cols[None, :, None, :]]
        return g.reshape(B, H, n_rb, n_cb, kb_n, d)

    kb, vb = gather(k), gather(v)
    qb = q.reshape(B, H, n_rb, q_r, n_cb, NA_QBLK_C, d).transpose(0, 1, 2, 4, 3, 5, 6)
    qb = qb.reshape(B, H, n_rb, n_cb, qb_n, d)
    scale = NA_DIM ** -0.5
    s_loc = jnp.einsum('bhrcqd,bhrckd->bhrcqk', qb, kb).astype(F32) * scale + bias[None]
    s_ctx = jnp.einsum('bhrcqd,bhkd->bhrcqk', qb, k_ctx).astype(F32) * scale
    p = jax.nn.softmax(jnp.concatenate([s_loc, s_ctx], axis=-1), axis=-1).astype(v.dtype)
    out = (jnp.einsum('bhrcqk,bhrckd->bhrcqd', p[..., :kb_n], vb)
           + jnp.einsum('bhrcqk,bhkd->bhrcqd', p[..., kb_n:], v_ctx))
    out = out.reshape(B, H, n_rb, n_cb, q_r, NA_QBLK_C, d).transpose(0, 1, 2, 4, 3, 5, 6)
    return out.reshape(B, H, N, d)


def mixer_na(pc, pl, q_gain, k_gain, rpb, ctx_out):
    def heads(a, gain=None):
        B, T, _ = a.shape
        a = a.reshape(B, T, NA_HEADS, NA_DIM)
        if gain is not None:
            a = rms_norm(a, gain)
        return a.transpose(0, 2, 1, 3)

    def merge(o):
        B, H, T, dv = o.shape
        return o.transpose(0, 2, 1, 3).reshape(B, T, H * dv)

    q_c, k_c, v_c = pc
    q_l, k_l, v_l = pl
    k_ctx, v_ctx = heads(k_c, k_gain), heads(v_c)
    out_l = merge(neighbourhood_attend(heads(q_l, q_gain), heads(k_l, k_gain), heads(v_l), k_ctx, v_ctx, rpb))
    out_c = merge(softmax_attend(heads(q_c, q_gain), k_ctx, v_ctx, NA_DIM ** -0.5)) if ctx_out else None
    return out_c, out_l


def swiglu(h, w1, w3, w2):
    return (jax.nn.silu(h @ w1) * (h @ w3)) @ w2


def moe_swiglu(h, w_router, w1, w3, w2):
    logits = (h @ w_router).astype(F32)
    top_val, top_idx = lax.top_k(logits, TOP_K)
    top_w = jax.nn.softmax(top_val, axis=-1)
    gate = jnp.sum(jax.nn.one_hot(top_idx, N_EXPERTS, dtype=F32) * top_w[..., None], axis=-2)
    out = jnp.zeros(h.shape, F32)
    for e in range(N_EXPERTS):
        out = out + gate[..., e:e + 1] * swiglu(h, w1[e], w3[e], w2[e]).astype(F32)
    return out.astype(h.dtype)


def setup_inputs(seed: int = 0) -> dict:
    key = jax.random.key(seed)
    ks = iter(jax.random.split(key, 48))
    D, F, E = D_MODEL, D_FF, N_EXPERTS

    def nrm(shape, scale):
        return jax.random.normal(next(ks), shape, F32) * scale

    def gain(shape):
        return 1.0 + 0.1 * jax.random.normal(next(ks), shape, F32)

    gate_base = jnp.stack([jnp.zeros((ML_HEADS,), F32), jnp.linspace(3.0, 6.0, ML_HEADS, dtype=F32)])
    return {
        "x": nrm((BATCH, SEQ, D), 1.0),
        "c": nrm((BATCH, D), 1.0),
        "ctx": nrm((BATCH, CTX_LEN, D), 1.0),
        "c_ctx": nrm((D,), 1.0),
        "w_ada": nrm((DEPTH, D, 6 * D), 0.5 * D ** -0.5),
        "b_ada": nrm((DEPTH, 6 * D), 0.02),
        "w_in": nrm((DEPTH, D, D_IN), D ** -0.5),
        "w_out": nrm((DEPTH, MIX_WIDTH, D), MIX_WIDTH ** -0.5),
        "da_q_gain": gain((DEPTH, DA_QK)),
        "da_k_gain": gain((DEPTH, DA_QK)),
        "da_lambda": nrm((DEPTH, 4, DA_QK), 0.1),
        "da_out_gain": gain((DEPTH, DA_V)),
        "ml_conv_w": nrm((DEPTH, ML_CONV, 2 * ML_HEADS * ML_DIM), 0.5),
        "ml_conv_b": nrm((DEPTH, 2 * ML_HEADS * ML_DIM), 0.02),
        "ml_gate_b": gate_base[None, None] + nrm((DEPTH, 2, 2, ML_HEADS), 0.1),
        "ml_out_gain": gain((DEPTH, ML_DIM)),
        "mla_cq_gain": gain((DEPTH, MLA_Q_RANK)),
        "mla_ckv_gain": gain((DEPTH, MLA_KV_RANK)),
        "mla_w_uq": nrm((DEPTH, MLA_Q_RANK, MLA_HEADS * (MLA_NOPE + MLA_ROPE)), MLA_Q_RANK ** -0.5),
        "mla_w_ukv": nrm((DEPTH, MLA_KV_RANK, MLA_HEADS * (MLA_NOPE + MLA_V)), MLA_KV_RANK ** -0.5),
        "mla_q_gain": gain((DEPTH, MLA_NOPE + MLA_ROPE)),
        "mla_k_gain": gain((DEPTH, MLA_NOPE + MLA_ROPE)),
        "na_q_gain": gain((DEPTH, NA_DIM)),
        "na_k_gain": gain((DEPTH, NA_DIM)),
        "na_rpb": nrm((DEPTH, NA_HEADS, 2 * NA_WIN_R - 1, 2 * NA_WIN_C - 1), 0.1),
        "ffn_w1": nrm((N_DENSE, D, F), D ** -0.5),
        "ffn_w3": nrm((N_DENSE, D, F), D ** -0.5),
        "ffn_w2": nrm((N_DENSE, F, D), F ** -0.5),
        "moe_router": nrm((N_MOE, D, E), D ** -0.5),
        "moe_w1": nrm((N_MOE, E, D, F), D ** -0.5),
        "moe_w3": nrm((N_MOE, E, D, F), D ** -0.5),
        "moe_w2": nrm((N_MOE, E, F, D), F ** -0.5),
    }


def reference(x, c, ctx, c_ctx, w_ada, b_ada, w_in, w_out, da_q_gain, da_k_gain, da_lambda, da_out_gain,
              ml_conv_w, ml_conv_b, ml_gate_b, ml_out_gain, mla_cq_gain, mla_ckv_gain, mla_w_uq, mla_w_ukv,
              mla_q_gain, mla_k_gain, na_q_gain, na_k_gain, na_rpb, ffn_w1, ffn_w3, ffn_w2,
              moe_router, moe_w1, moe_w3, moe_w2):
    n_lat = x.shape[1]
    rope_da = axial_rope_tables(n_lat, DA_QK)
    rope_mla = axial_rope_tables(n_lat, MLA_ROPE)
    cond_l = jax.nn.silu(c)
    cond_c = jax.nn.silu(c_ctx)
    x_lat, x_ctx = x, ctx
    for l in range(DEPTH):
        ctx_out = l < DEPTH - 1
        mod_l = jnp.split((cond_l @ w_ada[l] + b_ada[l])[:, None, :], 6, axis=-1)
        mod_c = jnp.split((cond_c @ w_ada[l] + b_ada[l])[None, None, :], 6, axis=-1)
        p_l = jnp.split(adaln(x_lat, mod_l[0], mod_l[1]) @ w_in[l], IN_OFFSETS, axis=-1)
        p_c = jnp.split(adaln(x_ctx, mod_c[0], mod_c[1]) @ w_in[l], IN_OFFSETS, axis=-1)
        oa = mixer_diff(p_c[0:3], p_l[0:3], da_q_gain[l], da_k_gain[l], da_lambda[l], da_out_gain[l],
                        diff_lambda_init(l), rope_da, ctx_out)
        ob = mixer_mlstm(p_c[3:8], p_l[3:8], ml_conv_w[l], ml_conv_b[l], ml_gate_b[l], ml_out_gain[l], ctx_out)
        oc = mixer_mla(p_c[8:11], p_l[8:11], mla_cq_gain[l], mla_ckv_gain[l], mla_w_uq[l], mla_w_ukv[l],
                       mla_q_gain[l], mla_k_gain[l], rope_mla, ctx_out)
        od = mixer_na(p_c[11:14], p_l[11:14], na_q_gain[l], na_k_gain[l], na_rpb[l], ctx_out)
        y_l = jnp.concatenate([oa[1], ob[1], oc[1], od[1]], axis=-1) @ w_out[l]
        x_lat = x_lat + mod_l[2] * y_l
        if ctx_out:
            y_c = jnp.concatenate([oa[0], ob[0], oc[0], od[0]], axis=-1) @ w_out[l]
            x_ctx = x_ctx + mod_c[2] * y_c
        h = adaln(x_lat, mod_l[3], mod_l[4])
        if ctx_out:
            h = jnp.concatenate([adaln(x_ctx, mod_c[3], mod_c[4]), h], axis=1)
        if l % 2 == 0:
            f = swiglu(h, ffn_w1[l // 2], ffn_w3[l // 2], ffn_w2[l // 2])
        else:
            f = moe_swiglu(h, moe_router[l // 2], moe_w1[l // 2], moe_w3[l // 2], moe_w2[l // 2])
        n_c = h.shape[1] - n_lat
        x_lat = x_lat + mod_l[5] * f[:, n_c:]
        if ctx_out:
            x_ctx = x_ctx + mod_c[5] * f[:, :n_c]
    return x_lat
```

```python
import functools
import math

import numpy as np
import jax
import jax.numpy as jnp
from jax import lax
from jax.experimental import pallas as pl
from jax.experimental.pallas import tpu as pltpu

F32 = jnp.float32
BF16 = jnp.bfloat16

GRID_W = 64
DA_HEADS, DA_QK, DA_V = 4, 64, 128
ML_HEADS, ML_DIM, ML_CONV = 4, 128, 3
MLA_HEADS, MLA_NOPE, MLA_ROPE, MLA_V = 8, 64, 32, 64
MLA_Q_RANK, MLA_KV_RANK = 384, 128
NA_HEADS, NA_DIM, NA_WIN_R, NA_WIN_C = 8, 64, 8, 16
N_EXPERTS, TOP_K = 8, 2
ROPE_BASE = 10000.0
RMS_EPS = 1e-6

LANES = 128
VMEM_LIMIT = 56 * 1024 * 1024

P_CQ, P_CKV = 0, 384
P_AQ, P_AK, P_AV = 512, 1024, 1536
P_BQ, P_BK, P_BV, P_BO = 2048, 2560, 3072, 3584
P_DQ, P_DK, P_DV = 4096, 4608, 5120
P_LAST = 5632
P_WIDTH = 5760
KR_LANE = 64

TQ = 256
ML_L = 256
NA_QR, NA_KR = 2, 10


def _params(sem):
    return pltpu.CompilerParams(dimension_semantics=sem, vmem_limit_bytes=VMEM_LIMIT)


def _silu(x):
    return x * (1.0 / (1.0 + jnp.exp(-x)))


def _sigmoid(x):
    return 1.0 / (1.0 + jnp.exp(-x))


def _rms(x, n=None):
    n = x.shape[-1] if n is None else n
    return x * lax.rsqrt(jnp.sum(x * x, axis=-1, keepdims=True) * (1.0 / n) + RMS_EPS)


def _rms_halves(x, gain):
    lo = lax.broadcasted_iota(jnp.int32, x.shape, 1) < 64
    x2 = x * x
    s_lo = jnp.sum(jnp.where(lo, x2, 0.0), axis=-1, keepdims=True)
    s_hi = jnp.sum(jnp.where(lo, 0.0, x2), axis=-1, keepdims=True)
    ms = jnp.where(lo, s_lo, s_hi) * (1.0 / 64)
    return x * lax.rsqrt(ms + RMS_EPS) * gain


def _rope(x, cos, sin_next, sin_prev, seg):
    return x * cos + pltpu.roll(x, LANES - seg, 1) * sin_next + pltpu.roll(x, seg, 1) * sin_prev


def _softmax_rows(s):
    e = jnp.exp(s - jnp.max(s, axis=-1, keepdims=True))
    return e / jnp.sum(e, axis=-1, keepdims=True)


def _dot(a, b):
    return jnp.dot(a, b, preferred_element_type=F32)


def _dot_nt(a, b):
    return lax.dot_general(a, b, (((1,), (1,)), ((), ())), preferred_element_type=F32)


def _dot_tn(a, b):
    return lax.dot_general(a, b, (((0,), (0,)), ((), ())), preferred_element_type=F32)


def _split3(a):
    a1 = a.astype(BF16)
    r = a - a1.astype(F32)
    a2 = r.astype(BF16)
    a3 = (r - a2.astype(F32)).astype(BF16)
    return a1, a2, a3


def _dot_f32(a, b):
    a1, a2, a3 = _split3(a)
    b1, b2, b3 = _split3(b)
    return (_dot(a1, b1) + (_dot(a1, b2) + _dot(a2, b1))
            + (_dot(a1, b3) + _dot(a2, b2) + _dot(a3, b1)))


def _adaln_tile(x, mod_ref, i, tm, tpb, lc, c_shift, c_scale):
    d = x.shape[1]
    b = i // tpb
    row = (i % tpb) * tm + lax.broadcasted_iota(jnp.int32, (tm, 1), 0)
    is_ctx = row < lc
    shift = jnp.where(is_ctx, mod_ref[4:5, c_shift * d:(c_shift + 1) * d],
                      mod_ref[pl.ds(b, 1), c_shift * d:(c_shift + 1) * d])
    scale = jnp.where(is_ctx, mod_ref[4:5, c_scale * d:(c_scale + 1) * d],
                      mod_ref[pl.ds(b, 1), c_scale * d:(c_scale + 1) * d])
    return _rms(x) * (1.0 + scale) + shift


def _gate_tile(mod_ref, i, tm, tpb, lc, c_gate, col0, width, d):
    b = i // tpb
    row = (i % tpb) * tm + lax.broadcasted_iota(jnp.int32, (tm, 1), 0)
    lo = c_gate * d + col0
    return jnp.where(row < lc, mod_ref[4:5, lo:lo + width], mod_ref[pl.ds(b, 1), lo:lo + width])


def _mod_kernel(c_ref, w_ref, b_ref, o_ref):
    s = _silu(c_ref[...]).astype(BF16)
    o_ref[...] = _dot(s, w_ref[...].astype(BF16)) + b_ref[...]


def _mod_table(cond, w_ada, b_ada):
    depth, d, n = w_ada.shape
    tn = 1024
    return pl.pallas_call(
        _mod_kernel,
        grid=(depth, n // tn),
        in_specs=[pl.BlockSpec((8, d), lambda l, j: (0, 0)),
                  pl.BlockSpec((None, d, tn), lambda l, j: (l, 0, j)),
                  pl.BlockSpec((None, 1, tn), lambda l, j: (l, 0, j))],
        out_specs=pl.BlockSpec((None, 8, tn), lambda l, j: (l, 0, j)),
        out_shape=jax.ShapeDtypeStruct((depth, 8, n), F32),
        compiler_params=_params(("arbitrary", "arbitrary")),
        name="mod_table",
    )(cond, w_ada, b_ada.reshape(depth, 1, n))


def _inproj_kernel(x_ref, mod_ref, w_ref, o_ref, h_scr, *, tm, tpb, lc):
    i = pl.program_id(0)

    @pl.when(pl.program_id(1) == 0)
    def _():
        h_scr[...] = _adaln_tile(x_ref[...], mod_ref, i, tm, tpb, lc, 0, 1).astype(BF16)

    o_ref[...] = _dot(h_scr[...], w_ref[...])


def _inproj(x, mod, w, t, lc):
    m, d = x.shape
    n = w.shape[1]
    tm, tn = 768, 640
    return pl.pallas_call(
        functools.partial(_inproj_kernel, tm=tm, tpb=t // tm, lc=lc),
        grid=(m // tm, n // tn),
        in_specs=[pl.BlockSpec((tm, d), lambda i, j: (i, 0)),
                  pl.BlockSpec(mod.shape, lambda i, j: (0, 0)),
                  pl.BlockSpec((d, tn), lambda i, j: (0, j))],
        out_specs=pl.BlockSpec((tm, tn), lambda i, j: (i, j)),
        out_shape=jax.ShapeDtypeStruct((m, n), F32),
        scratch_shapes=[pltpu.VMEM((tm, d), BF16)],
        compiler_params=_params(("arbitrary", "arbitrary")),
        name="inproj",
    )(x, mod, w)


def _diff_attn_kernel(q_ref, k_ref, v_ref, cos_ref, sn_ref, sp_ref, qg_ref, kg_ref, og_ref, lam_ref,
                      o_ref, kn_scr, vb_scr, *, tq, lc, lam_init, scale):
    qi = pl.program_id(2)
    t = k_ref.shape[0]

    @pl.when(qi == 0)
    def _():
        k = _rms_halves(k_ref[...], kg_ref[...])
        kn_scr[...] = _rope(k, cos_ref[...], sn_ref[...], sp_ref[...], 16).astype(BF16)
        vb_scr[...] = v_ref[...].astype(BF16)

    r0 = pl.multiple_of(qi * tq, tq)
    q = _rms_halves(q_ref[...], qg_ref[...])
    q = _rope(q, cos_ref[pl.ds(r0, tq), :], sn_ref[pl.ds(r0, tq), :], sp_ref[pl.ds(r0, tq), :], 16) * scale
    lo = lax.broadcasted_iota(jnp.int32, q.shape, 1) < 64
    q1 = jnp.where(lo, q, 0.0).astype(BF16)
    q2 = jnp.where(lo, 0.0, q).astype(BF16)
    lp = lam_ref[...]
    lam = (jnp.exp(jnp.sum(lp[0:1] * lp[1:2], axis=-1, keepdims=True))
           - jnp.exp(jnp.sum(lp[2:3] * lp[3:4], axis=-1, keepdims=True)) + lam_init)

    def attend(nk):
        kk = kn_scr[0:nk, :]
        p = _softmax_rows(_dot_nt(q1, kk)) - lam * _softmax_rows(_dot_nt(q2, kk))
        o = _dot(p.astype(BF16), vb_scr[0:nk, :])
        o_ref[...] = (_rms(o) * og_ref[...] * (1.0 - lam_init)).astype(BF16)

    @pl.when(qi * tq < lc)
    def _():
        attend(lc)

    @pl.when(qi * tq >= lc)
    def _():
        attend(t)


def _mixer_diff(p, tabs, q_gain, k_gain, lam_params, out_gain, lam_init, b, t, lc):
    m = p.shape[0]
    nq = t // TQ
    cos, sn, sp = tabs
    tile2 = lambda g: jnp.tile(g, 2).reshape(1, LANES)
    full = lambda a: pl.BlockSpec(a.shape, lambda bb, h, qi: (0,) * a.ndim)
    args = (cos, sn, sp, tile2(q_gain), tile2(k_gain), out_gain.reshape(1, LANES), lam_params)
    return pl.pallas_call(
        functools.partial(_diff_attn_kernel, tq=TQ, lc=lc, lam_init=lam_init, scale=DA_QK ** -0.5),
        grid=(b, DA_HEADS, nq),
        in_specs=[pl.BlockSpec((TQ, LANES), lambda bb, h, qi: (bb * nq + qi, P_AQ // LANES + h)),
                  pl.BlockSpec((t, LANES), lambda bb, h, qi: (bb, P_AK // LANES + h)),
                  pl.BlockSpec((t, LANES), lambda bb, h, qi: (bb, P_AV // LANES + h))]
                 + [full(a) for a in args],
        out_specs=pl.BlockSpec((TQ, LANES), lambda bb, h, qi: (bb * nq + qi, h)),
        out_shape=jax.ShapeDtypeStruct((m, DA_HEADS * DA_V), BF16),
        scratch_shapes=[pltpu.VMEM((t, LANES), BF16), pltpu.VMEM((t, LANES), BF16)],
        compiler_params=_params(("arbitrary", "arbitrary", "arbitrary")),
        name="diff_attn",
    )(p, p, p, *args)


def _mla_prep_kernel(cq_ref, ckv_ref, last_ref, cos_ref, sn_ref, sp_ref, cqg_ref, ckvg_ref, wq_ref, wk_ref,
                     wv_ref, qg_ref, kg_ref, q_out, k_out, v_out, *, scale):
    hd = MLA_NOPE + MLA_ROPE
    cos, sn, sp = cos_ref[...], sn_ref[...], sp_ref[...]
    cqn = (_rms(cq_ref[...]) * cqg_ref[...]).astype(BF16)
    ckvn = (_rms(ckv_ref[...]) * ckvg_ref[...]).astype(BF16)
    q = _dot(cqn, wq_ref[...])
    kk = _dot(ckvn, wk_ref[...])
    v_out[...] = _dot(ckvn, wv_ref[...]).astype(BF16)
    last = last_ref[...]
    lane = lax.broadcasted_iota(jnp.int32, last.shape, 1)
    kr = jnp.where((lane >= KR_LANE) & (lane < KR_LANE + MLA_ROPE), last, 0.0)
    for h in range(MLA_HEADS):
        sl = slice(h * LANES, (h + 1) * LANES)
        qh = _rms(q[:, sl], hd) * qg_ref[...]
        q_out[:, sl] = (_rope(qh, cos, sn, sp, 8) * scale).astype(BF16)
        kh = _rms(kk[:, sl] + kr, hd) * kg_ref[...]
        k_out[:, sl] = _rope(kh, cos, sn, sp, 8).astype(BF16)


def _mla_prep(p, tabs, cq_gain, ckv_gain, wq, wk, wv, q_gain, k_gain, t):
    m = p.shape[0]
    tm = 768
    tpb = t // tm
    hw = MLA_HEADS * LANES
    cos, sn, sp = tabs
    tab = pl.BlockSpec((tm, LANES), lambda i: (i % tpb, 0))
    full = lambda a: pl.BlockSpec(a.shape, lambda i: (0,) * a.ndim)
    pad = lambda g: jnp.pad(g, (0, LANES - g.shape[0])).reshape(1, LANES)
    consts = (cq_gain.reshape(1, -1), ckv_gain.reshape(1, -1), wq, wk, wv, pad(q_gain), pad(k_gain))
    return pl.pallas_call(
        functools.partial(_mla_prep_kernel, scale=(MLA_NOPE + MLA_ROPE) ** -0.5),
        grid=(m // tm,),
        in_specs=[pl.BlockSpec((tm, MLA_Q_RANK), lambda i: (i, P_CQ // MLA_Q_RANK)),
                  pl.BlockSpec((tm, LANES), lambda i: (i, P_CKV // LANES)),
                  pl.BlockSpec((tm, LANES), lambda i: (i, P_LAST // LANES)),
                  tab, tab, tab] + [full(a) for a in consts],
        out_specs=[pl.BlockSpec((tm, hw), lambda i: (i, 0)),
                   pl.BlockSpec((tm, hw), lambda i: (i, 0)),
                   pl.BlockSpec((tm, MLA_HEADS * MLA_V), lambda i: (i, 0))],
        out_shape=[jax.ShapeDtypeStruct((m, hw), BF16), jax.ShapeDtypeStruct((m, hw), BF16),
                   jax.ShapeDtypeStruct((m, MLA_HEADS * MLA_V), BF16)],
        compiler_params=_params(("arbitrary",)),
        name="mla_prep",
    )(p, p, p, cos, sn, sp, *consts)


def _mla_attn_kernel(q_ref, k_ref, v_ref, o_ref, *, tq, lc):
    qi = pl.program_id(2)
    t = k_ref.shape[0]

    def attend(nk):
        vv = v_ref[0:nk, :]
        outs = []
        for h in range(2):
            sl = slice(h * LANES, (h + 1) * LANES)
            s = _dot_nt(q_ref[:, sl], k_ref[0:nk, sl])
            outs.append(_dot(_softmax_rows(s).astype(BF16), vv))
        lo = lax.broadcasted_iota(jnp.int32, outs[0].shape, 1) < MLA_V
        o_ref[...] = jnp.where(lo, outs[0], outs[1]).astype(BF16)

    @pl.when(qi * tq < lc)
    def _():
        attend(lc)

    @pl.when(qi * tq >= lc)
    def _():
        attend(t)


def _mla_attn(q, k, v, b, t, lc):
    m = q.shape[0]
    nq = t // TQ
    return pl.pallas_call(
        functools.partial(_mla_attn_kernel, tq=TQ, lc=lc),
        grid=(b, MLA_HEADS // 2, nq),
        in_specs=[pl.BlockSpec((TQ, 2 * LANES), lambda bb, hp, qi: (bb * nq + qi, hp)),
                  pl.BlockSpec((t, 2 * LANES), lambda bb, hp, qi: (bb, hp)),
                  pl.BlockSpec((t, LANES), lambda bb, hp, qi: (bb, hp))],
        out_specs=pl.BlockSpec((TQ, LANES), lambda bb, hp, qi: (bb * nq + qi, hp)),
        out_shape=jax.ShapeDtypeStruct((m, MLA_HEADS * MLA_V), BF16),
        compiler_params=_params(("arbitrary", "arbitrary", "arbitrary")),
        name="mla_attn",
    )(q, k, v)


def _na_kernel(q_ref, k_ref, v_ref, bias_ref, qg_ref, kg_ref, o_ref, kn_scr, vb_scr, *, lc, rows, scale):
    s = pl.program_id(1)
    tq = q_ref.shape[0]
    n_ctx_blk = lc // tq
    band = NA_KR * GRID_W
    npair = NA_HEADS // 2

    @pl.when(s == 0)
    def _():
        for pp in range(npair):
            sl = slice(pp * LANES, (pp + 1) * LANES)
            kn_scr[:, sl] = _rms_halves(k_ref[:, sl], kg_ref[:, sl]).astype(BF16)
        vb_scr[...] = v_ref[...].astype(BF16)

    def heads(score_fn):
        for pp in range(npair):
            sl = slice(pp * LANES, (pp + 1) * LANES)
            q = _rms_halves(q_ref[:, sl], qg_ref[:, sl]) * scale
            lo = lax.broadcasted_iota(jnp.int32, q.shape, 1) < NA_DIM
            o0 = score_fn(jnp.where(lo, q, 0.0).astype(BF16), sl, 2 * pp)
            o1 = score_fn(jnp.where(lo, 0.0, q).astype(BF16), sl, 2 * pp + 1)
            o_ref[:, sl] = jnp.where(lo, o0, o1).astype(BF16)

    @pl.when(s < n_ctx_blk)
    def _():
        def f(qm, sl, h):
            sc = _dot_nt(qm, kn_scr[0:lc, sl])
            e = jnp.exp(sc - jnp.max(sc, axis=-1, keepdims=True))
            return _dot(e.astype(BF16), vb_scr[0:lc, sl]) / jnp.sum(e, axis=-1, keepdims=True)
        heads(f)

    @pl.when(s >= n_ctx_blk)
    def _():
        rb = s - n_ctx_blk
        krow = jnp.clip(rb * NA_QR - NA_WIN_R // 2, 0, rows - NA_KR)
        k0 = pl.multiple_of(lc + krow * GRID_W, GRID_W)

        def f(qm, sl, h):
            s_loc = _dot_nt(qm, kn_scr[pl.ds(k0, band), sl]) + bias_ref[h]
            s_ctx = _dot_nt(qm, kn_scr[0:lc, sl])
            mx = jnp.maximum(jnp.max(s_loc, axis=-1, keepdims=True), jnp.max(s_ctx, axis=-1, keepdims=True))
            e_loc = jnp.exp(s_loc - mx)
            e_ctx = jnp.exp(s_ctx - mx)
            den = jnp.sum(e_loc, axis=-1, keepdims=True) + jnp.sum(e_ctx, axis=-1, keepdims=True)
            num = _dot(e_loc.astype(BF16), vb_scr[pl.ds(k0, band), sl]) + _dot(e_ctx.astype(BF16), vb_scr[0:lc, sl])
            return num / den
        heads(f)


def _na_bias_table(rpb, rows):
    n_rb = rows // NA_QR
    q_rows = np.arange(n_rb)[:, None] * NA_QR + np.arange(NA_QR)
    key_rows = np.clip(np.arange(n_rb) * NA_QR - NA_WIN_R // 2, 0, rows - NA_KR)[:, None] + np.arange(NA_KR)
    cols = np.arange(GRID_W)
    r0 = np.clip(q_rows - NA_WIN_R // 2, 0, rows - NA_WIN_R)[:, :, None]
    c0 = np.clip(cols - NA_WIN_C // 2, 0, GRID_W - NA_WIN_C)[:, None]
    kr = key_rows[:, None, :]
    ok_r = (kr >= r0) & (kr < r0 + NA_WIN_R)
    ok_c = (cols[None, :] >= c0) & (cols[None, :] < c0 + NA_WIN_C)
    rel_r = np.clip(kr - q_rows[:, :, None] + NA_WIN_R - 1, 0, 2 * NA_WIN_R - 2)
    rel_c = np.clip(cols[None, :] - cols[:, None] + NA_WIN_C - 1, 0, 2 * NA_WIN_C - 2)
    oh_r = jnp.asarray(np.eye(2 * NA_WIN_R - 1, dtype=np.float32)[rel_r])
    oh_c = jnp.asarray(np.eye(2 * NA_WIN_C - 1, dtype=np.float32)[rel_c])
    bias = jnp.einsum('rikd,hde,qce->hriqkc', oh_r, rpb, oh_c, precision=lax.Precision.HIGHEST)
    mask = ok_r[:, :, None, :, None] & ok_c[None, None, :, None, :]
    bias = jnp.where(jnp.asarray(mask)[None], bias, -jnp.inf)
    return bias.reshape(NA_HEADS, n_rb, NA_QR * GRID_W, NA_KR * GRID_W)


def _mixer_na(p, q_gain, k_gain, rpb, b, t, lc):
    m = p.shape[0]
    rows = (t - lc) // GRID_W
    tq = NA_QR * GRID_W
    nblk = t // tq
    n_ctx_blk = lc // tq
    bias = _na_bias_table(rpb, rows)
    tile8 = lambda g: jnp.tile(g, NA_HEADS).reshape(1, NA_HEADS * NA_DIM)
    w = NA_HEADS * NA_DIM
    return pl.pallas_call(
        functools.partial(_na_kernel, lc=lc, rows=rows, scale=NA_DIM ** -0.5),
        grid=(b, nblk),
        in_specs=[pl.BlockSpec((tq, w), lambda bb, s: (bb * nblk + s, P_DQ // w)),
                  pl.BlockSpec((t, w), lambda bb, s: (bb, P_DK // w)),
                  pl.BlockSpec((t, w), lambda bb, s: (bb, P_DV // w)),
                  pl.BlockSpec((NA_HEADS, None, tq, NA_KR * GRID_W),
                               lambda bb, s: (0, jnp.maximum(s - n_ctx_blk, 0), 0, 0)),
                  pl.BlockSpec((1, w), lambda bb, s: (0, 0)),
                  pl.BlockSpec((1, w), lambda bb, s: (0, 0))],
        out_specs=pl.BlockSpec((tq, w), lambda bb, s: (bb * nblk + s, 0)),
        out_shape=jax.ShapeDtypeStruct((m, w), BF16),
        scratch_shapes=[pltpu.VMEM((t, w), BF16), pltpu.VMEM((t, w), BF16)],
        compiler_params=_params(("arbitrary", "arbitrary")),
        name="na_attn",
    )(p, p, p, bias, tile8(q_gain), tile8(k_gain))


def _ml_prep_kernel(x_ref, w_ref, b_ref, o_ref, *, lc):
    t = x_ref.shape[0]
    x = x_ref[...]
    row = lax.broadcasted_iota(jnp.int32, (t, 1), 0)
    prev = jnp.where((row == 0) | (row == lc), 0.0, pltpu.roll(x, 1, 0))
    nxt = jnp.where((row == lc - 1) | (row == t - 1), 0.0, pltpu.roll(x, t - 1, 0))
    y = _silu(prev * w_ref[0:1, :] + x * w_ref[1:2, :] + nxt * w_ref[2:3, :] + b_ref[...])
    o_ref[...] = y * jnp.where(pl.program_id(1) == 1, ML_DIM ** -0.5, 1.0)


def _ml_prep(p, conv_w, conv_b, b, t, lc):
    m = p.shape[0]
    w = ML_HEADS * ML_DIM
    return pl.pallas_call(
        functools.partial(_ml_prep_kernel, lc=lc),
        grid=(b, 2),
        in_specs=[pl.BlockSpec((t, w), lambda bb, c: (bb, P_BQ // w + c)),
                  pl.BlockSpec((ML_CONV, w), lambda bb, c: (0, c)),
                  pl.BlockSpec((1, w), lambda bb, c: (0, c))],
        out_specs=pl.BlockSpec((t, w), lambda bb, c: (bb, c)),
        out_shape=jax.ShapeDtypeStruct((m, 2 * w), F32),
        compiler_params=_params(("arbitrary", "arbitrary")),
        name="mlstm_prep",
    )(p, conv_w, conv_b.reshape(1, -1))


def _ml_scan_kernel(q_ref, k_ref, v_ref, g_ref, o_ref, c_scr, n_scr, m_scr):
    rev = pl.program_id(2) == 1
    L = q_ref.shape[0]

    @pl.when(pl.program_id(3) == 0)
    def _():
        c_scr[...] = jnp.zeros_like(c_scr)
        n_scr[...] = jnp.zeros_like(n_scr)
        m_scr[...] = jnp.zeros_like(m_scr)

    q = q_ref[...]
    k = k_ref[...]
    v = v_ref[...]
    i_row = g_ref[0:1, :]
    f_row = g_ref[1:2, :]
    lf_row = jnp.minimum(f_row, 0.0) - jnp.log(1.0 + jnp.exp(-jnp.abs(f_row)))
    tt = lax.broadcasted_iota(jnp.int32, (L, L), 0)
    ss = lax.broadcasted_iota(jnp.int32, (L, L), 1)
    eye = tt == ss
    sgn = jnp.where(rev, -1, 1)
    cm = (ss - tt) * sgn <= 0
    to_col = lambda r: jnp.sum(jnp.where(eye, r, 0.0), axis=-1, keepdims=True)
    lf_col = to_col(lf_row)
    i_col = to_col(i_row)
    b_col = jnp.sum(jnp.where(cm, lf_row, 0.0), axis=-1, keepdims=True)
    b_row = jnp.sum(jnp.where((tt - ss) * sgn <= 0, lf_col, 0.0), axis=0, keepdims=True)
    b_end = jnp.sum(lf_row, axis=-1, keepdims=True)
    m_old = m_scr[...]
    c_old = c_scr[...]
    n_old = n_scr[...]

    log_w = jnp.where(cm, b_col - b_row + i_row, -jnp.inf)
    inter = b_col + m_old
    m_t = jnp.maximum(inter, jnp.max(log_w, axis=-1, keepdims=True))
    qb = q.astype(BF16)
    kb = k.astype(BF16)
    vb = v.astype(BF16)
    s = _dot_nt(qb, kb) * jnp.exp(log_w - m_t)
    a = jnp.exp(inter - m_t)
    num = a * _dot(qb, c_old.astype(BF16)) + _dot(s.astype(BF16), vb)
    den = a * jnp.sum(q * n_old, axis=-1, keepdims=True) + jnp.sum(s, axis=-1, keepdims=True)
    o_ref[...] = num / jnp.maximum(jnp.abs(den), jnp.exp(-m_t))

    w_end = b_end - b_col + i_col
    m_new = jnp.maximum(b_end + m_old, jnp.max(w_end, axis=0, keepdims=True))
    decay = jnp.exp(b_end + m_old - m_new)
    we = jnp.exp(w_end - m_new)
    c_scr[...] = decay * c_old + _dot_tn(kb, (we * v).astype(BF16))
    n_scr[...] = decay * n_old + jnp.sum(we * k, axis=0, keepdims=True)
    m_scr[...] = m_new


def _ml_scan(qk, p, gates, b, t, lc):
    m = p.shape[0]
    L = ML_L
    nct, ncc = t // L, lc // L
    w = ML_HEADS * ML_DIM

    def chunk(d, j):
        return jnp.where(d == 0, j, jnp.where(j < ncc, ncc - 1 - j, nct - 1 - (j - ncc)))

    blk = lambda col0: pl.BlockSpec((L, ML_DIM), lambda bb, h, d, j: (bb * nct + chunk(d, j), col0 // ML_DIM + h))
    return pl.pallas_call(
        _ml_scan_kernel,
        grid=(b, ML_HEADS, 2, nct),
        in_specs=[blk(0), blk(w), blk(P_BV),
                  pl.BlockSpec((None, None, None, None, 2, L), lambda bb, h, d, j: (bb, h, d, chunk(d, j), 0, 0))],
        out_specs=pl.BlockSpec((None, L, ML_DIM), lambda bb, h, d, j: (d, bb * nct + chunk(d, j), h)),
        out_shape=jax.ShapeDtypeStruct((2, m, w), F32),
        scratch_shapes=[pltpu.VMEM((ML_DIM, ML_DIM), F32), pltpu.VMEM((1, ML_DIM), F32), pltpu.VMEM((1, 1), F32)],
        compiler_params=_params(("arbitrary",) * 4),
        name="mlstm_scan",
    )(qk, qk, p, gates)


def _ml_finish_kernel(h_ref, o_ref, g_ref, out_ref):
    hsum = h_ref[0] + h_ref[1]
    og = _sigmoid(o_ref[...])
    for h in range(ML_HEADS):
        sl = slice(h * ML_DIM, (h + 1) * ML_DIM)
        out_ref[:, sl] = (_rms(hsum[:, sl]) * g_ref[...] * og[:, sl]).astype(BF16)


def _ml_finish(hdir, p, out_gain):
    m = p.shape[0]
    w = ML_HEADS * ML_DIM
    tm = 768
    return pl.pallas_call(
        _ml_finish_kernel,
        grid=(m // tm,),
        in_specs=[pl.BlockSpec((2, tm, w), lambda i: (0, i, 0)),
                  pl.BlockSpec((tm, w), lambda i: (i, P_BO // w)),
                  pl.BlockSpec((1, ML_DIM), lambda i: (0, 0))],
        out_specs=pl.BlockSpec((tm, w), lambda i: (i, 0)),
        out_shape=jax.ShapeDtypeStruct((m, w), BF16),
        compiler_params=_params(("arbitrary",)),
        name="mlstm_finish",
    )(hdir, p, out_gain.reshape(1, ML_DIM))


def _mixer_mlstm(p, conv_w, conv_b, gate_b, out_gain, b, t, lc):
    qk = _ml_prep(p, conv_w, conv_b, b, t, lc)
    nct = t // ML_L
    g = p[:, P_LAST:P_LAST + 4 * ML_HEADS].reshape(b, nct, ML_L, 2, 2, ML_HEADS) + gate_b
    gates = g.transpose(0, 5, 3, 1, 4, 2)
    hdir = _ml_scan(qk, p, gates, b, t, lc)
    return _ml_finish(hdir, p, out_gain)


def _outproj_kernel(a_ref, b_ref, c_ref, d_ref, w_ref, x_ref, mod_ref, o_ref, *, tm, tpb, lc, tn, d):
    i = pl.program_id(0)
    j = pl.program_id(1)
    kw = a_ref.shape[1]
    y = _dot(a_ref[...], w_ref[0:kw, :])
    y += _dot(b_ref[...], w_ref[kw:2 * kw, :])
    y += _dot(c_ref[...], w_ref[2 * kw:3 * kw, :])
    y += _dot(d_ref[...], w_ref[3 * kw:4 * kw, :])
    row = (i % tpb) * tm + lax.broadcasted_iota(jnp.int32, (tm, 1), 0)
    b = i // tpb
    col = pl.multiple_of(2 * d + j * tn, LANES)
    gate = jnp.where(row < lc, mod_ref[4:5, pl.ds(col, tn)], mod_ref[pl.ds(b, 1), pl.ds(col, tn)])
    o_ref[...] = x_ref[...] + gate * y


def _outproj(mixes, w, x, mod, t, lc):
    m, d = x.shape
    tm, tn = 768, min(1024, d)
    kw = mixes[0].shape[1]
    mix_spec = pl.BlockSpec((tm, kw), lambda i, j: (i, 0))
    return pl.pallas_call(
        functools.partial(_outproj_kernel, tm=tm, tpb=t // tm, lc=lc, tn=tn, d=d),
        grid=(m // tm, d // tn),
        in_specs=[mix_spec] * 4 + [pl.BlockSpec((4 * kw, tn), lambda i, j: (0, j)),
                                   pl.BlockSpec((tm, tn), lambda i, j: (i, j)),
                                   pl.BlockSpec(mod.shape, lambda i, j: (0, 0))],
        out_specs=pl.BlockSpec((tm, tn), lambda i, j: (i, j)),
        out_shape=jax.ShapeDtypeStruct((m, d), F32),
        compiler_params=_params(("arbitrary", "arbitrary")),
        name="outproj",
    )(*mixes, w, x, mod)


def _ffn_kernel(x_ref, mod_ref, w1_ref, w3_ref, w2_ref, o_ref, h_scr, acc_scr, *, tm, tpb, lc):
    i = pl.program_id(0)
    j = pl.program_id(1)

    @pl.when(j == 0)
    def _():
        h_scr[...] = _adaln_tile(x_ref[...], mod_ref, i, tm, tpb, lc, 3, 4).astype(BF16)
        acc_scr[...] = jnp.zeros_like(acc_scr)

    h = h_scr[...]
    act = _silu(_dot(h, w1_ref[...])) * _dot(h, w3_ref[...])
    acc_scr[...] += _dot(act.astype(BF16), w2_ref[...])

    @pl.when(j == pl.num_programs(1) - 1)
    def _():
        d = x_ref.shape[1]
        gate = _gate_tile(mod_ref, i, tm, tpb, lc, 5, 0, d, d)
        o_ref[...] = x_ref[...] + gate * acc_scr[...]


def _ffn_dense(x, mod, w1, w3, w2, t, lc):
    m, d = x.shape
    f = w1.shape[1]
    tm, tf = 768, 512
    w1, w3, w2 = w1.astype(BF16), w3.astype(BF16), w2.astype(BF16)
    return pl.pallas_call(
        functools.partial(_ffn_kernel, tm=tm, tpb=t // tm, lc=lc),
        grid=(m // tm, f // tf),
        in_specs=[pl.BlockSpec((tm, d), lambda i, j: (i, 0), pipeline_mode=pl.Buffered(1)),
                  pl.BlockSpec(mod.shape, lambda i, j: (0, 0)),
                  pl.BlockSpec((d, tf), lambda i, j: (0, j)),
                  pl.BlockSpec((d, tf), lambda i, j: (0, j)),
                  pl.BlockSpec((tf, d), lambda i, j: (j, 0))],
        out_specs=pl.BlockSpec((tm, d), lambda i, j: (i, 0), pipeline_mode=pl.Buffered(1)),
        out_shape=jax.ShapeDtypeStruct((m, d), F32),
        scratch_shapes=[pltpu.VMEM((tm, d), BF16), pltpu.VMEM((tm, d), F32)],
        compiler_params=_params(("arbitrary", "arbitrary")),
        name="ffn_dense",
    )(x, mod, w1, w3, w2)


def _router_kernel(x_ref, mod_ref, wr_ref, h_ref, r_ref, *, tm, tpb, lc):
    i = pl.program_id(0)
    h = _adaln_tile(x_ref[...], mod_ref, i, tm, tpb, lc, 3, 4)
    h_ref[...] = h
    lane = lax.broadcasted_iota(jnp.int32, (tm, LANES), 1).astype(F32)
    logits = jnp.where(lane < N_EXPERTS, _dot_f32(h, wr_ref[...]), -jnp.inf)
    v1 = jnp.max(logits, axis=-1, keepdims=True)
    i1 = jnp.min(jnp.where(logits == v1, lane, float(LANES)), axis=-1, keepdims=True)
    rest = jnp.where(lane == i1, -jnp.inf, logits)
    v2 = jnp.max(rest, axis=-1, keepdims=True)
    i2 = jnp.min(jnp.where(rest == v2, lane, float(LANES)), axis=-1, keepdims=True)
    e2 = jnp.exp(v2 - v1)
    g1 = 1.0 / (1.0 + e2)
    g2 = e2 / (1.0 + e2)
    r_ref[...] = jnp.where(lane == 0, i1, jnp.where(lane == 1, i2, jnp.where(lane == 2, g1, jnp.where(lane == 3, g2, 0.0))))


def _router(x, mod, w_router, t, lc):
    m, d = x.shape
    tm = 768
    wr = jnp.pad(w_router, ((0, 0), (0, LANES - w_router.shape[1])))
    return pl.pallas_call(
        functools.partial(_router_kernel, tm=tm, tpb=t // tm, lc=lc),
        grid=(m // tm,),
        in_specs=[pl.BlockSpec((tm, d), lambda i: (i, 0)),
                  pl.BlockSpec(mod.shape, lambda i: (0, 0)),
                  pl.BlockSpec((d, LANES), lambda i: (0, 0))],
        out_specs=[pl.BlockSpec((tm, d), lambda i: (i, 0)),
                   pl.BlockSpec((tm, LANES), lambda i: (i, 0))],
        out_shape=[jax.ShapeDtypeStruct((m, d), F32), jax.ShapeDtypeStruct((m, LANES), F32)],
        compiler_params=_params(("arbitrary",)),
        name="moe_router",
    )(x, mod, wr)


def _gather_kernel(src_ref, h_ref, z_ref, o_ref, sem, *, tg):
    base = pl.program_id(0) * tg

    def issue(r, carry):
        idx = src_ref[base + r]

        @pl.when(idx >= 0)
        def _():
            pltpu.make_async_copy(h_ref.at[pl.ds(idx, 1)], o_ref.at[pl.ds(r, 1)], sem).start()

        @pl.when(idx < 0)
        def _():
            pltpu.make_async_copy(z_ref.at[pl.ds(0, 1)], o_ref.at[pl.ds(r, 1)], sem).start()
        return carry

    lax.fori_loop(0, tg, issue, 0)

    def drain(r, carry):
        pltpu.make_async_copy(z_ref.at[pl.ds(0, 1)], o_ref.at[pl.ds(r, 1)], sem).wait()
        return carry

    lax.fori_loop(0, tg, drain, 0)


def _gather_rows(src, h, n_rows):
    d = h.shape[1]
    tg = 256
    zeros = jnp.zeros((8, d), h.dtype)
    return pl.pallas_call(
        functools.partial(_gather_kernel, tg=tg),
        grid_spec=pltpu.PrefetchScalarGridSpec(
            num_scalar_prefetch=1,
            grid=(n_rows // tg,),
            in_specs=[pl.BlockSpec(memory_space=pl.ANY), pl.BlockSpec(memory_space=pl.ANY)],
            out_specs=pl.BlockSpec((tg, d), lambda i, src: (i, 0)),
            scratch_shapes=[pltpu.SemaphoreType.DMA(())]),
        out_shape=jax.ShapeDtypeStruct((n_rows, d), h.dtype),
        compiler_params=_params(("arbitrary",)),
        name="moe_gather",
    )(src, h, zeros)


def _moe_ffn_kernel(te_ref, tr_ref, nv_ref, hs_ref, w1_ref, w3_ref, w2_ref, o_ref, h_scr, acc_scr):
    i = pl.program_id(0)
    j = pl.program_id(1)

    @pl.when(i < nv_ref[0])
    def _():
        @pl.when(j == 0)
        def _():
            h_scr[...] = hs_ref[...].astype(BF16)
            acc_scr[...] = jnp.zeros_like(acc_scr)

        h = h_scr[...]
        act = _silu(_dot(h, w1_ref[...].astype(BF16))) * _dot(h, w3_ref[...].astype(BF16))
        acc_scr[...] += _dot(act.astype(BF16), w2_ref[...].astype(BF16))

        @pl.when(j == pl.num_programs(1) - 1)
        def _():
            o_ref[...] = acc_scr[...]

    @pl.when((i >= nv_ref[0]) & (j == 0))
    def _():
        o_ref[...] = jnp.zeros_like(o_ref)


def _moe_ffn(tile_expert, tile_row, n_valid, hs, w1, w3, w2, tm):
    r, d = hs.shape
    f = w1.shape[2]
    tf = 256
    nf = f // tf
    jj = lambda i, j, nv: jnp.where(i < nv[0], j, nf - 1)
    return pl.pallas_call(
        _moe_ffn_kernel,
        grid_spec=pltpu.PrefetchScalarGridSpec(
            num_scalar_prefetch=3,
            grid=(r // tm, nf),
            in_specs=[pl.BlockSpec((tm, d), lambda i, j, te, tr, nv: (tr[i], 0), pipeline_mode=pl.Buffered(1)),
                      pl.BlockSpec((None, d, tf), lambda i, j, te, tr, nv: (te[i], 0, jj(i, j, nv))),
                      pl.BlockSpec((None, d, tf), lambda i, j, te, tr, nv: (te[i], 0, jj(i, j, nv))),
                      pl.BlockSpec((None, tf, d), lambda i, j, te, tr, nv: (te[i], jj(i, j, nv), 0))],
            out_specs=pl.BlockSpec((tm, d), lambda i, j, te, tr, nv: (i, 0), pipeline_mode=pl.Buffered(1)),
            scratch_shapes=[pltpu.VMEM((tm, d), BF16), pltpu.VMEM((tm, d), F32)]),
        out_shape=jax.ShapeDtypeStruct((r, d), F32),
        compiler_params=_params(("arbitrary", "arbitrary")),
        name="moe_ffn",
    )(tile_expert, tile_row, n_valid, hs, w1, w3, w2)


def _combine_kernel(p1_ref, p2_ref, y_ref, x_ref, r_ref, mod_ref, o_ref, y1_scr, y2_scr, sem, *, tc, tpb, lc):
    i = pl.program_id(0)
    base = i * tc

    def issue(r, carry):
        pltpu.make_async_copy(y_ref.at[pl.ds(p1_ref[base + r], 1)], y1_scr.at[pl.ds(r, 1)], sem).start()
        pltpu.make_async_copy(y_ref.at[pl.ds(p2_ref[base + r], 1)], y2_scr.at[pl.ds(r, 1)], sem).start()
        return carry

    lax.fori_loop(0, tc, issue, 0)

    def drain(r, carry):
        pltpu.make_async_copy(y_ref.at[pl.ds(0, 1)], y1_scr.at[pl.ds(r, 1)], sem).wait()
        pltpu.make_async_copy(y_ref.at[pl.ds(0, 1)], y2_scr.at[pl.ds(r, 1)], sem).wait()
        return carry

    lax.fori_loop(0, tc, drain, 0)
    d = x_ref.shape[1]
    gate = _gate_tile(mod_ref, i, tc, tpb, lc, 5, 0, d, d)
    route = r_ref[...]
    f = route[:, 2:3] * y1_scr[...] + route[:, 3:4] * y2_scr[...]
    o_ref[...] = x_ref[...] + gate * f


def _combine(pos1, pos2, y, x, route, mod, t, lc):
    m, d = x.shape
    tc = 256
    return pl.pallas_call(
        functools.partial(_combine_kernel, tc=tc, tpb=t // tc, lc=lc),
        grid_spec=pltpu.PrefetchScalarGridSpec(
            num_scalar_prefetch=2,
            grid=(m // tc,),
            in_specs=[pl.BlockSpec(memory_space=pl.ANY),
                      pl.BlockSpec((tc, d), lambda i, p1, p2: (i, 0)),
                      pl.BlockSpec((tc, LANES), lambda i, p1, p2: (i, 0)),
                      pl.BlockSpec(mod.shape, lambda i, p1, p2: (0, 0))],
            out_specs=pl.BlockSpec((tc, d), lambda i, p1, p2: (i, 0)),
            scratch_shapes=[pltpu.VMEM((tc, d), F32), pltpu.VMEM((tc, d), F32), pltpu.SemaphoreType.DMA(())]),
        out_shape=jax.ShapeDtypeStruct((m, d), F32),
        compiler_params=_params(("arbitrary",)),
        name="moe_combine",
    )(pos1, pos2, y, x, route, mod)


def _ffn_moe(x, mod, w_router, w1, w3, w2, t, lc):
    m, d = x.shape
    tm = 768
    h, route = _router(x, mod, w_router, t, lc)
    experts = route[:, 0:TOP_K].astype(jnp.int32).reshape(-1)
    onehot = (experts[:, None] == jnp.arange(N_EXPERTS)[None, :]).astype(jnp.int32)
    rank = jnp.sum((jnp.cumsum(onehot, axis=0) - onehot) * onehot, axis=1)
    count = jnp.sum(onehot, axis=0)
    tiles = (count + tm - 1) // tm
    tile_end = jnp.cumsum(tiles)
    start = (tile_end - tiles) * tm
    pos = start[experts] + rank
    n_tiles = (TOP_K * m + N_EXPERTS * (tm - 1)) // tm
    n_rows = n_tiles * tm
    src = jnp.full((n_rows,), -1, jnp.int32).at[pos].set(jnp.arange(TOP_K * m, dtype=jnp.int32) // TOP_K)
    n_valid = tile_end[-1]
    tile_ids = jnp.minimum(jnp.arange(n_tiles), n_valid - 1)
    tile_expert = jnp.sum((tile_ids[:, None] >= tile_end[None, :]).astype(jnp.int32), axis=1)
    hs = _gather_rows(src, h, n_rows)
    y = _moe_ffn(tile_expert.astype(jnp.int32), tile_ids.astype(jnp.int32),
                 n_valid.reshape(1).astype(jnp.int32), hs, w1, w3, w2, tm)
    pos = pos.reshape(m, TOP_K)
    return _combine(pos[:, 0], pos[:, 1], y, x, route, mod, t, lc)


def _rope_tables(n_lat, lc, rot_dim, lane0, period):
    tok = jnp.arange(n_lat)
    axis_dim = rot_dim // 2
    inv_freq = ROPE_BASE ** (-jnp.arange(0, axis_dim, 2, dtype=F32) / axis_dim)
    ang_r = (tok // GRID_W).astype(F32)[:, None] * inv_freq
    ang_c = (tok % GRID_W).astype(F32)[:, None] * inv_freq
    ang = jnp.concatenate([ang_r, ang_r, ang_c, ang_c], axis=-1)
    cos, sin = jnp.cos(ang), jnp.sin(ang)
    seg = rot_dim // 4
    lane = np.arange(LANES)
    rel = (lane - lane0) % period
    active = (lane >= lane0) & (rel < rot_dim)
    even = ((rel // seg) % 2 == 0)
    idx = np.where(active, rel, 0)
    cos_t = jnp.where(active[None, :], cos[:, idx], 1.0)
    sin_t = jnp.where(active[None, :], sin[:, idx], 0.0)
    sin_next = jnp.where(even[None, :], -sin_t, 0.0)
    sin_prev = jnp.where(even[None, :], 0.0, sin_t)
    ctx = lambda a, v: jnp.concatenate([jnp.full((lc, LANES), v, F32), a], axis=0)
    return ctx(cos_t, 1.0), ctx(sin_next, 0.0), ctx(sin_prev, 0.0)


def _relayout_w_in(w):
    d = w.shape[0]
    seg = lambda lo, n: w[:, lo:lo + n]
    z = lambda n: jnp.zeros((d, n), w.dtype)
    a0, b0, g0, c0, d0 = 0, 1536, 3584, 3600, 4144
    parts = [seg(c0, 384), seg(c0 + 384, 128),
             seg(a0, 1536), seg(b0, 2048), seg(d0, 1536),
             seg(g0, 16), z(KR_LANE - 16), seg(c0 + 512, 32), z(LANES - KR_LANE - 32)]
    return jnp.concatenate(parts, axis=1).astype(BF16)


def _mla_weights(w_uq, w_ukv):
    hd = MLA_NOPE + MLA_ROPE
    wq = jnp.pad(w_uq.reshape(MLA_Q_RANK, MLA_HEADS, hd), ((0, 0), (0, 0), (0, LANES - hd)))
    wkv = w_ukv.reshape(MLA_KV_RANK, MLA_HEADS, MLA_NOPE + MLA_V)
    wk = jnp.pad(wkv[:, :, :MLA_NOPE], ((0, 0), (0, 0), (0, LANES - MLA_NOPE)))
    wv = wkv[:, :, MLA_NOPE:]
    return (wq.reshape(MLA_Q_RANK, -1).astype(BF16), wk.reshape(MLA_KV_RANK, -1).astype(BF16),
            wv.reshape(MLA_KV_RANK, -1).astype(BF16))


def kernel(x, c, ctx, c_ctx, w_ada, b_ada, w_in, w_out, da_q_gain, da_k_gain, da_lambda, da_out_gain, ml_conv_w, ml_conv_b, ml_gate_b, ml_out_gain, mla_cq_gain, mla_ckv_gain, mla_w_uq, mla_w_ukv, mla_q_gain, mla_k_gain, na_q_gain, na_k_gain, na_rpb, ffn_w1, ffn_w3, ffn_w2, moe_router, moe_w1, moe_w3, moe_w2):
    b, n_lat, d = x.shape
    lc = ctx.shape[1]
    t = lc + n_lat
    depth = w_in.shape[0]
    assert b <= 4 and lc == TQ and n_lat % (NA_QR * GRID_W) == 0 and t % 768 == 0

    xs = jnp.concatenate([ctx, x], axis=1).reshape(b * t, d)
    cond = jnp.zeros((8, d), F32).at[:b].set(c).at[4].set(c_ctx)
    mod = _mod_table(cond, w_ada, b_ada)
    rope_da = _rope_tables(n_lat, lc, DA_QK, 0, DA_QK)
    rope_mla = _rope_tables(n_lat, lc, MLA_ROPE, KR_LANE, LANES)

    for l in range(depth):
        lam_init = 0.8 - 0.6 * math.exp(-0.3 * l)
        p = _inproj(xs, mod[l], _relayout_w_in(w_in[l]), t, lc)
        mix_a = _mixer_diff(p, rope_da, da_q_gain[l], da_k_gain[l], da_lambda[l], da_out_gain[l], lam_init, b, t, lc)
        mix_b = _mixer_mlstm(p, ml_conv_w[l], ml_conv_b[l], ml_gate_b[l], ml_out_gain[l], b, t, lc)
        wq, wk, wv = _mla_weights(mla_w_uq[l], mla_w_ukv[l])
        qc, kc, vc = _mla_prep(p, rope_mla, mla_cq_gain[l], mla_ckv_gain[l], wq, wk, wv,
                               mla_q_gain[l], mla_k_gain[l], t)
        mix_c = _mla_attn(qc, kc, vc, b, t, lc)
        mix_d = _mixer_na(p, na_q_gain[l], na_k_gain[l], na_rpb[l], b, t, lc)
        xs = _outproj((mix_a, mix_b, mix_c, mix_d), w_out[l].astype(BF16), xs, mod[l], t, lc)
        if l % 2 == 0:
            xs = _ffn_dense(xs, mod[l], ffn_w1[l // 2], ffn_w3[l // 2], ffn_w2[l // 2], t, lc)
        else:
            xs = _ffn_moe(xs, mod[l], moe_router[l // 2], moe_w1[l // 2], moe_w3[l // 2], moe_w2[l // 2], t, lc)
    return xs.reshape(b, t, d)[:, lc:]
```

```python
import functools
import math

import numpy as np
import jax
import jax.numpy as jnp
from jax import lax
from jax.experimental import pallas as pl
from jax.experimental.pallas import tpu as pltpu

F32 = jnp.float32
BF16 = jnp.bfloat16

GRID_W = 64
DA_HEADS, DA_QK, DA_V = 4, 64, 128
ML_HEADS, ML_DIM, ML_CONV = 4, 128, 3
MLA_HEADS, MLA_NOPE, MLA_ROPE, MLA_V = 8, 64, 32, 64
MLA_Q_RANK, MLA_KV_RANK = 384, 128
NA_HEADS, NA_DIM, NA_WIN_R, NA_WIN_C = 8, 64, 8, 16
N_EXPERTS, TOP_K = 8, 2
ROPE_BASE = 10000.0
RMS_EPS = 1e-6
LOG2E = math.log2(math.e)

LANES = 128
VMEM_LIMIT = 56 * 1024 * 1024

P_CQ, P_CKV = 0, 384
P_AQ, P_AK, P_AV = 512, 1024, 1536
P_BQ, P_BK, P_BV, P_BO = 2048, 2560, 3072, 3584
P_DQ, P_DK, P_DV = 4096, 4608, 5120
P_LAST = 5632
P_WIDTH = 5760
KR_LANE = 64

TQ = 256
ML_L = 256
NA_QR, NA_KR = 2, 10
MOE_TM, MOE_TS = 1280, 256


def _params(sem):
    return pltpu.CompilerParams(dimension_semantics=sem, vmem_limit_bytes=VMEM_LIMIT)


def _silu(x):
    return x * (1.0 / (1.0 + jnp.exp(-x)))


def _sigmoid(x):
    return 1.0 / (1.0 + jnp.exp(-x))


def _rms(x, n=None):
    n = x.shape[-1] if n is None else n
    return x * lax.rsqrt(jnp.sum(x * x, axis=-1, keepdims=True) * (1.0 / n) + RMS_EPS)


def _rms_halves(x, gain):
    lo = lax.broadcasted_iota(jnp.int32, x.shape, 1) < 64
    x2 = x * x
    s_lo = jnp.sum(jnp.where(lo, x2, 0.0), axis=-1, keepdims=True)
    s_hi = jnp.sum(jnp.where(lo, 0.0, x2), axis=-1, keepdims=True)
    ms = jnp.where(lo, s_lo, s_hi) * (1.0 / 64)
    return x * lax.rsqrt(ms + RMS_EPS) * gain


def _rope(x, cos, sin_next, sin_prev, seg):
    return x * cos + pltpu.roll(x, LANES - seg, 1) * sin_next + pltpu.roll(x, seg, 1) * sin_prev


def _softmax2_pv(s2, v):
    e = jnp.exp2(s2 - jnp.max(s2, axis=-1, keepdims=True))
    return _dot(e.astype(BF16), v) * (1.0 / jnp.sum(e, axis=-1, keepdims=True))


def _dot(a, b):
    return jnp.dot(a, b, preferred_element_type=F32)


def _dot_nt(a, b):
    return lax.dot_general(a, b, (((1,), (1,)), ((), ())), preferred_element_type=F32)


def _dot_tn(a, b):
    return lax.dot_general(a, b, (((0,), (0,)), ((), ())), preferred_element_type=F32)


def _split3(a):
    a1 = a.astype(BF16)
    r = a - a1.astype(F32)
    a2 = r.astype(BF16)
    a3 = (r - a2.astype(F32)).astype(BF16)
    return a1, a2, a3


def _dot_f32(a, b):
    a1, a2, a3 = _split3(a)
    b1, b2, b3 = _split3(b)
    return (_dot(a1, b1) + (_dot(a1, b2) + _dot(a2, b1))
            + (_dot(a1, b3) + _dot(a2, b2) + _dot(a3, b1)))


def _adaln_tile(x, mod_ref, i, tm, tpb, lc, c_shift, c_scale):
    d = x.shape[1]
    b = i // tpb
    row = (i % tpb) * tm + lax.broadcasted_iota(jnp.int32, (tm, 1), 0)
    is_ctx = row < lc
    shift = jnp.where(is_ctx, mod_ref[4:5, c_shift * d:(c_shift + 1) * d],
                      mod_ref[pl.ds(b, 1), c_shift * d:(c_shift + 1) * d])
    scale = jnp.where(is_ctx, mod_ref[4:5, c_scale * d:(c_scale + 1) * d],
                      mod_ref[pl.ds(b, 1), c_scale * d:(c_scale + 1) * d])
    return _rms(x) * (1.0 + scale) + shift


def _gate_tile(mod_ref, i, tm, tpb, lc, c_gate, col0, width, d):
    b = i // tpb
    row = (i % tpb) * tm + lax.broadcasted_iota(jnp.int32, (tm, 1), 0)
    lo = c_gate * d + col0
    return jnp.where(row < lc, mod_ref[4:5, lo:lo + width], mod_ref[pl.ds(b, 1), lo:lo + width])


def _mod_kernel(c_ref, w_ref, b_ref, o_ref):
    s = _silu(c_ref[...]).astype(BF16)
    o_ref[...] = _dot(s, w_ref[...].astype(BF16)) + b_ref[...]


def _mod_table(cond, w_ada, b_ada):
    depth, d, n = w_ada.shape
    tn = 1024
    return pl.pallas_call(
        _mod_kernel,
        grid=(depth, n // tn),
        in_specs=[pl.BlockSpec((8, d), lambda l, j: (0, 0)),
                  pl.BlockSpec((None, d, tn), lambda l, j: (l, 0, j)),
                  pl.BlockSpec((None, 1, tn), lambda l, j: (l, 0, j))],
        out_specs=pl.BlockSpec((None, 8, tn), lambda l, j: (l, 0, j)),
        out_shape=jax.ShapeDtypeStruct((depth, 8, n), F32),
        compiler_params=_params(("arbitrary", "arbitrary")),
        name="mod_table",
    )(cond, w_ada, b_ada.reshape(depth, 1, n))


def _inproj_kernel(x_ref, mod_ref, w_ref, o_ref, h_scr, *, tm, tpb, lc):
    i = pl.program_id(0)

    @pl.when(pl.program_id(1) == 0)
    def _():
        h_scr[...] = _adaln_tile(x_ref[...], mod_ref, i, tm, tpb, lc, 0, 1).astype(BF16)

    o_ref[...] = _dot(h_scr[...], w_ref[...])


def _inproj(x, mod, w, t, lc):
    m, d = x.shape
    n = w.shape[1]
    tm, tn = 768, 1152
    return pl.pallas_call(
        functools.partial(_inproj_kernel, tm=tm, tpb=t // tm, lc=lc),
        grid=(m // tm, n // tn),
        in_specs=[pl.BlockSpec((tm, d), lambda i, j: (i, 0)),
                  pl.BlockSpec(mod.shape, lambda i, j: (0, 0)),
                  pl.BlockSpec((d, tn), lambda i, j: (0, j))],
        out_specs=pl.BlockSpec((tm, tn), lambda i, j: (i, j)),
        out_shape=jax.ShapeDtypeStruct((m, n), F32),
        scratch_shapes=[pltpu.VMEM((tm, d), BF16)],
        compiler_params=_params(("arbitrary", "arbitrary")),
        name="inproj",
    )(x, mod, w)


def _diff_attn_kernel(q_ref, k_ref, v_ref, cos_ref, sn_ref, sp_ref, qg_ref, kg_ref, og_ref, lam_ref,
                      o_ref, kn_scr, vb_scr, *, tq, lc, lam_init, scale):
    qi = pl.program_id(2)
    t = k_ref.shape[0]

    @pl.when(qi == 0)
    def _():
        k = _rms_halves(k_ref[...], kg_ref[...])
        kn_scr[...] = _rope(k, cos_ref[...], sn_ref[...], sp_ref[...], 16).astype(BF16)
        vb_scr[...] = v_ref[...].astype(BF16)

    r0 = pl.multiple_of(qi * tq, tq)
    q = _rms_halves(q_ref[...], qg_ref[...])
    q = _rope(q, cos_ref[pl.ds(r0, tq), :], sn_ref[pl.ds(r0, tq), :], sp_ref[pl.ds(r0, tq), :], 16) * scale
    lo = lax.broadcasted_iota(jnp.int32, q.shape, 1) < 64
    q1 = jnp.where(lo, q, 0.0).astype(BF16)
    q2 = jnp.where(lo, 0.0, q).astype(BF16)
    lp = lam_ref[...]
    lam = (jnp.exp(jnp.sum(lp[0:1] * lp[1:2], axis=-1, keepdims=True))
           - jnp.exp(jnp.sum(lp[2:3] * lp[3:4], axis=-1, keepdims=True)) + lam_init)

    def attend(nk):
        kk = kn_scr[0:nk, :]
        vv = vb_scr[0:nk, :]
        o = _softmax2_pv(_dot_nt(q1, kk), vv) - lam * _softmax2_pv(_dot_nt(q2, kk), vv)
        o_ref[...] = (_rms(o) * og_ref[...] * (1.0 - lam_init)).astype(BF16)

    @pl.when(qi * tq < lc)
    def _():
        attend(lc)

    @pl.when(qi * tq >= lc)
    def _():
        attend(t)


def _mixer_diff(p, tabs, q_gain, k_gain, lam_params, out_gain, lam_init, b, t, lc):
    m = p.shape[0]
    nq = t // TQ
    cos, sn, sp = tabs
    tile2 = lambda g: jnp.tile(g, 2).reshape(1, LANES)
    full = lambda a: pl.BlockSpec(a.shape, lambda bb, h, qi: (0,) * a.ndim)
    args = (cos, sn, sp, tile2(q_gain), tile2(k_gain), out_gain.reshape(1, LANES), lam_params)
    return pl.pallas_call(
        functools.partial(_diff_attn_kernel, tq=TQ, lc=lc, lam_init=lam_init, scale=DA_QK ** -0.5 * LOG2E),
        grid=(b, DA_HEADS, nq),
        in_specs=[pl.BlockSpec((TQ, LANES), lambda bb, h, qi: (bb * nq + qi, P_AQ // LANES + h)),
                  pl.BlockSpec((t, LANES), lambda bb, h, qi: (bb, P_AK // LANES + h)),
                  pl.BlockSpec((t, LANES), lambda bb, h, qi: (bb, P_AV // LANES + h))]
                 + [full(a) for a in args],
        out_specs=pl.BlockSpec((TQ, LANES), lambda bb, h, qi: (bb * nq + qi, h)),
        out_shape=jax.ShapeDtypeStruct((m, DA_HEADS * DA_V), BF16),
        scratch_shapes=[pltpu.VMEM((t, LANES), BF16), pltpu.VMEM((t, LANES), BF16)],
        compiler_params=_params(("arbitrary", "arbitrary", "arbitrary")),
        name="diff_attn",
    )(p, p, p, *args)


def _mla_prep_kernel(cq_ref, ckv_ref, last_ref, cos_ref, sn_ref, sp_ref, cqg_ref, ckvg_ref, wq_ref, wk_ref,
                     wv_ref, qg_ref, kg_ref, q_out, k_out, v_out, *, scale):
    hd = MLA_NOPE + MLA_ROPE
    cos, sn, sp = cos_ref[...], sn_ref[...], sp_ref[...]
    cqn = (_rms(cq_ref[...]) * cqg_ref[...]).astype(BF16)
    ckvn = (_rms(ckv_ref[...]) * ckvg_ref[...]).astype(BF16)
    q = _dot(cqn, wq_ref[...])
    kk = _dot(ckvn, wk_ref[...])
    v_out[...] = _dot(ckvn, wv_ref[...]).astype(BF16)
    last = last_ref[...]
    lane = lax.broadcasted_iota(jnp.int32, last.shape, 1)
    kr = jnp.where((lane >= KR_LANE) & (lane < KR_LANE + MLA_ROPE), last, 0.0)
    for h in range(MLA_HEADS):
        sl = slice(h * LANES, (h + 1) * LANES)
        qh = _rms(q[:, sl], hd) * qg_ref[...]
        q_out[:, sl] = (_rope(qh, cos, sn, sp, 8) * scale).astype(BF16)
        kh = _rms(kk[:, sl] + kr, hd) * kg_ref[...]
        k_out[:, sl] = _rope(kh, cos, sn, sp, 8).astype(BF16)


def _mla_prep(p, tabs, cq_gain, ckv_gain, wq, wk, wv, q_gain, k_gain, t):
    m = p.shape[0]
    tm = 768
    tpb = t // tm
    hw = MLA_HEADS * LANES
    cos, sn, sp = tabs
    tab = pl.BlockSpec((tm, LANES), lambda i: (i % tpb, 0))
    full = lambda a: pl.BlockSpec(a.shape, lambda i: (0,) * a.ndim)
    pad = lambda g: jnp.pad(g, (0, LANES - g.shape[0])).reshape(1, LANES)
    consts = (cq_gain.reshape(1, -1), ckv_gain.reshape(1, -1), wq, wk, wv, pad(q_gain), pad(k_gain))
    return pl.pallas_call(
        functools.partial(_mla_prep_kernel, scale=(MLA_NOPE + MLA_ROPE) ** -0.5 * LOG2E),
        grid=(m // tm,),
        in_specs=[pl.BlockSpec((tm, MLA_Q_RANK), lambda i: (i, P_CQ // MLA_Q_RANK)),
                  pl.BlockSpec((tm, LANES), lambda i: (i, P_CKV // LANES)),
                  pl.BlockSpec((tm, LANES), lambda i: (i, P_LAST // LANES)),
                  tab, tab, tab] + [full(a) for a in consts],
        out_specs=[pl.BlockSpec((tm, hw), lambda i: (i, 0)),
                   pl.BlockSpec((tm, hw), lambda i: (i, 0)),
                   pl.BlockSpec((tm, MLA_HEADS * MLA_V), lambda i: (i, 0))],
        out_shape=[jax.ShapeDtypeStruct((m, hw), BF16), jax.ShapeDtypeStruct((m, hw), BF16),
                   jax.ShapeDtypeStruct((m, MLA_HEADS * MLA_V), BF16)],
        compiler_params=_params(("arbitrary",)),
        name="mla_prep",
    )(p, p, p, cos, sn, sp, *consts)


def _mla_attn_kernel(q_ref, k_ref, v_ref, o_ref, *, tq, lc):
    qi = pl.program_id(2)
    t = k_ref.shape[0]

    def attend(nk):
        vv = v_ref[0:nk, :]
        outs = []
        for h in range(2):
            sl = slice(h * LANES, (h + 1) * LANES)
            outs.append(_softmax2_pv(_dot_nt(q_ref[:, sl], k_ref[0:nk, sl]), vv))
        lo = lax.broadcasted_iota(jnp.int32, outs[0].shape, 1) < MLA_V
        o_ref[...] = jnp.where(lo, outs[0], outs[1]).astype(BF16)

    @pl.when(qi * tq < lc)
    def _():
        attend(lc)

    @pl.when(qi * tq >= lc)
    def _():
        attend(t)


def _mla_attn(q, k, v, b, t, lc):
    m = q.shape[0]
    nq = t // TQ
    return pl.pallas_call(
        functools.partial(_mla_attn_kernel, tq=TQ, lc=lc),
        grid=(b, MLA_HEADS // 2, nq),
        in_specs=[pl.BlockSpec((TQ, 2 * LANES), lambda bb, hp, qi: (bb * nq + qi, hp)),
                  pl.BlockSpec((t, 2 * LANES), lambda bb, hp, qi: (bb, hp)),
                  pl.BlockSpec((t, LANES), lambda bb, hp, qi: (bb, hp))],
        out_specs=pl.BlockSpec((TQ, LANES), lambda bb, hp, qi: (bb * nq + qi, hp)),
        out_shape=jax.ShapeDtypeStruct((m, MLA_HEADS * MLA_V), BF16),
        compiler_params=_params(("arbitrary", "arbitrary", "arbitrary")),
        name="mla_attn",
    )(q, k, v)


def _na_kernel(q_ref, k_ref, v_ref, bias_ref, qg_ref, kg_ref, o_ref, kn_scr, vb_scr, *, lc, rows, scale):
    s = pl.program_id(1)
    tq = q_ref.shape[0]
    n_ctx_blk = lc // tq
    band = NA_KR * GRID_W
    npair = NA_HEADS // 2

    @pl.when(s == 0)
    def _():
        for pp in range(npair):
            sl = slice(pp * LANES, (pp + 1) * LANES)
            kn_scr[:, sl] = _rms_halves(k_ref[:, sl], kg_ref[:, sl]).astype(BF16)
        vb_scr[...] = v_ref[...].astype(BF16)

    def heads(score_fn):
        for pp in range(npair):
            sl = slice(pp * LANES, (pp + 1) * LANES)
            q = _rms_halves(q_ref[:, sl], qg_ref[:, sl]) * scale
            lo = lax.broadcasted_iota(jnp.int32, q.shape, 1) < NA_DIM
            o0 = score_fn(jnp.where(lo, q, 0.0).astype(BF16), sl, 2 * pp)
            o1 = score_fn(jnp.where(lo, 0.0, q).astype(BF16), sl, 2 * pp + 1)
            o_ref[:, sl] = jnp.where(lo, o0, o1).astype(BF16)

    @pl.when(s < n_ctx_blk)
    def _():
        def f(qm, sl, h):
            return _softmax2_pv(_dot_nt(qm, kn_scr[0:lc, sl]), vb_scr[0:lc, sl])
        heads(f)

    @pl.when(s >= n_ctx_blk)
    def _():
        rb = s - n_ctx_blk
        krow = jnp.clip(rb * NA_QR - NA_WIN_R // 2, 0, rows - NA_KR)
        k0 = pl.multiple_of(lc + krow * GRID_W, GRID_W)

        def f(qm, sl, h):
            s_loc = _dot_nt(qm, kn_scr[pl.ds(k0, band), sl]) + bias_ref[h]
            s_ctx = _dot_nt(qm, kn_scr[0:lc, sl])
            mx = jnp.maximum(jnp.max(s_loc, axis=-1, keepdims=True), jnp.max(s_ctx, axis=-1, keepdims=True))
            e_loc = jnp.exp2(s_loc - mx)
            e_ctx = jnp.exp2(s_ctx - mx)
            den = jnp.sum(e_loc, axis=-1, keepdims=True) + jnp.sum(e_ctx, axis=-1, keepdims=True)
            num = _dot(e_loc.astype(BF16), vb_scr[pl.ds(k0, band), sl]) + _dot(e_ctx.astype(BF16), vb_scr[0:lc, sl])
            return num * (1.0 / den)
        heads(f)


def _na_variant(rb, n_rb):
    return jnp.where(rb < 2, rb, jnp.where(rb >= n_rb - 2, rb - (n_rb - 5), 2))


def _na_bias_table(rpb, rows):
    n_rb = rows // NA_QR
    rbs = np.array([0, 1, 2, n_rb - 2, n_rb - 1])
    q_rows = rbs[:, None] * NA_QR + np.arange(NA_QR)
    key_rows = np.clip(rbs * NA_QR - NA_WIN_R // 2, 0, rows - NA_KR)[:, None] + np.arange(NA_KR)
    cols = np.arange(GRID_W)
    r0 = np.clip(q_rows - NA_WIN_R // 2, 0, rows - NA_WIN_R)[:, :, None]
    c0 = np.clip(cols - NA_WIN_C // 2, 0, GRID_W - NA_WIN_C)[:, None]
    kr = key_rows[:, None, :]
    ok_r = (kr >= r0) & (kr < r0 + NA_WIN_R)
    ok_c = (cols[None, :] >= c0) & (cols[None, :] < c0 + NA_WIN_C)
    rel_r = np.clip(kr - q_rows[:, :, None] + NA_WIN_R - 1, 0, 2 * NA_WIN_R - 2)
    rel_c = np.clip(cols[None, :] - cols[:, None] + NA_WIN_C - 1, 0, 2 * NA_WIN_C - 2)
    oh_r = jnp.asarray(np.eye(2 * NA_WIN_R - 1, dtype=np.float32)[rel_r])
    oh_c = jnp.asarray(np.eye(2 * NA_WIN_C - 1, dtype=np.float32)[rel_c])
    bias = jnp.einsum('rikd,hde,qce->hriqkc', oh_r, rpb, oh_c, precision=lax.Precision.HIGHEST)
    mask = ok_r[:, :, None, :, None] & ok_c[None, None, :, None, :]
    bias = jnp.where(jnp.asarray(mask)[None], bias * LOG2E, -jnp.inf)
    return bias.reshape(NA_HEADS, len(rbs), NA_QR * GRID_W, NA_KR * GRID_W)


def _mixer_na(p, q_gain, k_gain, rpb, b, t, lc):
    m = p.shape[0]
    rows = (t - lc) // GRID_W
    tq = NA_QR * GRID_W
    nblk = t // tq
    n_ctx_blk = lc // tq
    bias = _na_bias_table(rpb, rows)
    tile8 = lambda g: jnp.tile(g, NA_HEADS).reshape(1, NA_HEADS * NA_DIM)
    w = NA_HEADS * NA_DIM
    return pl.pallas_call(
        functools.partial(_na_kernel, lc=lc, rows=rows, scale=NA_DIM ** -0.5 * LOG2E),
        grid=(b, nblk),
        in_specs=[pl.BlockSpec((tq, w), lambda bb, s: (bb * nblk + s, P_DQ // w)),
                  pl.BlockSpec((t, w), lambda bb, s: (bb, P_DK // w)),
                  pl.BlockSpec((t, w), lambda bb, s: (bb, P_DV // w)),
                  pl.BlockSpec((NA_HEADS, None, tq, NA_KR * GRID_W),
                               lambda bb, s: (0, _na_variant(jnp.maximum(s - n_ctx_blk, 0), rows // NA_QR), 0, 0)),
                  pl.BlockSpec((1, w), lambda bb, s: (0, 0)),
                  pl.BlockSpec((1, w), lambda bb, s: (0, 0))],
        out_specs=pl.BlockSpec((tq, w), lambda bb, s: (bb * nblk + s, 0)),
        out_shape=jax.ShapeDtypeStruct((m, w), BF16),
        scratch_shapes=[pltpu.VMEM((t, w), BF16), pltpu.VMEM((t, w), BF16)],
        compiler_params=_params(("arbitrary", "arbitrary")),
        name="na_attn",
    )(p, p, p, bias, tile8(q_gain), tile8(k_gain))


def _ml_prep_kernel(x_ref, w_ref, b_ref, o_ref, *, lc):
    t = x_ref.shape[0]
    x = x_ref[...]
    row = lax.broadcasted_iota(jnp.int32, (t, 1), 0)
    prev = jnp.where((row == 0) | (row == lc), 0.0, pltpu.roll(x, 1, 0))
    nxt = jnp.where((row == lc - 1) | (row == t - 1), 0.0, pltpu.roll(x, t - 1, 0))
    y = _silu(prev * w_ref[0:1, :] + x * w_ref[1:2, :] + nxt * w_ref[2:3, :] + b_ref[...])
    o_ref[...] = y * jnp.where(pl.program_id(1) == 1, ML_DIM ** -0.5, 1.0)


def _ml_prep(p, conv_w, conv_b, b, t, lc):
    m = p.shape[0]
    w = ML_HEADS * ML_DIM
    return pl.pallas_call(
        functools.partial(_ml_prep_kernel, lc=lc),
        grid=(b, 2),
        in_specs=[pl.BlockSpec((t, w), lambda bb, c: (bb, P_BQ // w + c)),
                  pl.BlockSpec((ML_CONV, w), lambda bb, c: (0, c)),
                  pl.BlockSpec((1, w), lambda bb, c: (0, c))],
        out_specs=pl.BlockSpec((t, w), lambda bb, c: (bb, c)),
        out_shape=jax.ShapeDtypeStruct((m, 2 * w), F32),
        compiler_params=_params(("arbitrary", "arbitrary")),
        name="mlstm_prep",
    )(p, conv_w, conv_b.reshape(1, -1))


def _ml_scan_kernel(q_ref, k_ref, v_ref, g_ref, o_ref, c_scr, n_scr, m_scr):
    rev = pl.program_id(2) == 1
    L = q_ref.shape[0]

    @pl.when(pl.program_id(3) == 0)
    def _():
        c_scr[...] = jnp.zeros_like(c_scr)
        n_scr[...] = jnp.zeros_like(n_scr)
        m_scr[...] = jnp.zeros_like(m_scr)

    q = q_ref[...]
    k = k_ref[...]
    v = v_ref[...]
    i_row = g_ref[0:1, :]
    f_row = g_ref[1:2, :]
    lf_row = jnp.minimum(f_row, 0.0) - jnp.log(1.0 + jnp.exp(-jnp.abs(f_row)))
    tt = lax.broadcasted_iota(jnp.int32, (L, L), 0)
    ss = lax.broadcasted_iota(jnp.int32, (L, L), 1)
    eye = tt == ss
    sgn = jnp.where(rev, -1, 1)
    cm = (ss - tt) * sgn <= 0
    to_col = lambda r: jnp.sum(jnp.where(eye, r, 0.0), axis=-1, keepdims=True)
    lf_col = to_col(lf_row)
    i_col = to_col(i_row)
    b_col = jnp.sum(jnp.where(cm, lf_row, 0.0), axis=-1, keepdims=True)
    b_row = jnp.sum(jnp.where((tt - ss) * sgn <= 0, lf_col, 0.0), axis=0, keepdims=True)
    b_end = jnp.sum(lf_row, axis=-1, keepdims=True)
    m_old = m_scr[...]
    c_old = c_scr[...]
    n_old = n_scr[...]

    log_w = jnp.where(cm, b_col - b_row + i_row, -jnp.inf)
    inter = b_col + m_old
    m_t = jnp.maximum(inter, jnp.max(log_w, axis=-1, keepdims=True))
    qb = q.astype(BF16)
    kb = k.astype(BF16)
    vb = v.astype(BF16)
    s = _dot_nt(qb, kb) * jnp.exp(log_w - m_t)
    a = jnp.exp(inter - m_t)
    num = a * _dot(qb, c_old.astype(BF16)) + _dot(s.astype(BF16), vb)
    den = a * jnp.sum(q * n_old, axis=-1, keepdims=True) + jnp.sum(s, axis=-1, keepdims=True)
    o_ref[...] = num / jnp.maximum(jnp.abs(den), jnp.exp(-m_t))

    w_end = b_end - b_col + i_col
    m_new = jnp.maximum(b_end + m_old, jnp.max(w_end, axis=0, keepdims=True))
    decay = jnp.exp(b_end + m_old - m_new)
    we = jnp.exp(w_end - m_new)
    c_scr[...] = decay * c_old + _dot_tn(kb, (we * v).astype(BF16))
    n_scr[...] = decay * n_old + jnp.sum(we * k, axis=0, keepdims=True)
    m_scr[...] = m_new


def _ml_scan(qk, p, gates, b, t, lc):
    m = p.shape[0]
    L = ML_L
    nct, ncc = t // L, lc // L
    w = ML_HEADS * ML_DIM

    def chunk(d, j):
        return jnp.where(d == 0, j, jnp.where(j < ncc, ncc - 1 - j, nct - 1 - (j - ncc)))

    blk = lambda col0: pl.BlockSpec((L, ML_DIM), lambda bb, h, d, j: (bb * nct + chunk(d, j), col0 // ML_DIM + h))
    return pl.pallas_call(
        _ml_scan_kernel,
        grid=(b, ML_HEADS, 2, nct),
        in_specs=[blk(0), blk(w), blk(P_BV),
                  pl.BlockSpec((None, None, None, None, 2, L), lambda bb, h, d, j: (bb, h, d, chunk(d, j), 0, 0))],
        out_specs=pl.BlockSpec((None, L, ML_DIM), lambda bb, h, d, j: (d, bb * nct + chunk(d, j), h)),
        out_shape=jax.ShapeDtypeStruct((2, m, w), F32),
        scratch_shapes=[pltpu.VMEM((ML_DIM, ML_DIM), F32), pltpu.VMEM((1, ML_DIM), F32), pltpu.VMEM((1, 1), F32)],
        compiler_params=_params(("arbitrary",) * 4),
        name="mlstm_scan",
    )(qk, qk, p, gates)


def _ml_finish_kernel(h_ref, o_ref, g_ref, out_ref):
    hsum = h_ref[0] + h_ref[1]
    og = _sigmoid(o_ref[...])
    for h in range(ML_HEADS):
        sl = slice(h * ML_DIM, (h + 1) * ML_DIM)
        out_ref[:, sl] = (_rms(hsum[:, sl]) * g_ref[...] * og[:, sl]).astype(BF16)


def _ml_finish(hdir, p, out_gain):
    m = p.shape[0]
    w = ML_HEADS * ML_DIM
    tm = 768
    return pl.pallas_call(
        _ml_finish_kernel,
        grid=(m // tm,),
        in_specs=[pl.BlockSpec((2, tm, w), lambda i: (0, i, 0)),
                  pl.BlockSpec((tm, w), lambda i: (i, P_BO // w)),
                  pl.BlockSpec((1, ML_DIM), lambda i: (0, 0))],
        out_specs=pl.BlockSpec((tm, w), lambda i: (i, 0)),
        out_shape=jax.ShapeDtypeStruct((m, w), BF16),
        compiler_params=_params(("arbitrary",)),
        name="mlstm_finish",
    )(hdir, p, out_gain.reshape(1, ML_DIM))


def _mixer_mlstm(p, conv_w, conv_b, gate_b, out_gain, b, t, lc):
    qk = _ml_prep(p, conv_w, conv_b, b, t, lc)
    nct = t // ML_L
    g = p[:, P_LAST:P_LAST + 4 * ML_HEADS].reshape(b, nct, ML_L, 2, 2, ML_HEADS) + gate_b
    gates = g.transpose(0, 5, 3, 1, 4, 2)
    hdir = _ml_scan(qk, p, gates, b, t, lc)
    return _ml_finish(hdir, p, out_gain)


def _outproj_kernel(a_ref, b_ref, c_ref, d_ref, w_ref, x_ref, mod_ref, o_ref, *, tm, tpb, lc, tn, d):
    i = pl.program_id(0)
    j = pl.program_id(1)
    kw = a_ref.shape[1]
    y = _dot(a_ref[...], w_ref[0:kw, :])
    y += _dot(b_ref[...], w_ref[kw:2 * kw, :])
    y += _dot(c_ref[...], w_ref[2 * kw:3 * kw, :])
    y += _dot(d_ref[...], w_ref[3 * kw:4 * kw, :])
    row = (i % tpb) * tm + lax.broadcasted_iota(jnp.int32, (tm, 1), 0)
    b = i // tpb
    col = pl.multiple_of(2 * d + j * tn, LANES)
    gate = jnp.where(row < lc, mod_ref[4:5, pl.ds(col, tn)], mod_ref[pl.ds(b, 1), pl.ds(col, tn)])
    o_ref[...] = x_ref[...] + gate * y


def _outproj(mixes, w, x, mod, t, lc):
    m, d = x.shape
    tm, tn = 768, min(1024, d)
    kw = mixes[0].shape[1]
    mix_spec = pl.BlockSpec((tm, kw), lambda i, j: (i, 0))
    return pl.pallas_call(
        functools.partial(_outproj_kernel, tm=tm, tpb=t // tm, lc=lc, tn=tn, d=d),
        grid=(m // tm, d // tn),
        in_specs=[mix_spec] * 4 + [pl.BlockSpec((4 * kw, tn), lambda i, j: (0, j)),
                                   pl.BlockSpec((tm, tn), lambda i, j: (i, j)),
                                   pl.BlockSpec(mod.shape, lambda i, j: (0, 0))],
        out_specs=pl.BlockSpec((tm, tn), lambda i, j: (i, j)),
        out_shape=jax.ShapeDtypeStruct((m, d), F32),
        compiler_params=_params(("arbitrary", "arbitrary")),
        name="outproj",
    )(*mixes, w, x, mod)


def _ffn_kernel(x_ref, mod_ref, w1_ref, w3_ref, w2_ref, o_ref, h_scr, acc_scr, *, tm, tpb, lc):
    i = pl.program_id(0)
    j = pl.program_id(1)

    @pl.when(j == 0)
    def _():
        h_scr[...] = _adaln_tile(x_ref[...], mod_ref, i, tm, tpb, lc, 3, 4).astype(BF16)
        acc_scr[...] = jnp.zeros_like(acc_scr)

    h = h_scr[...]
    act = _silu(_dot(h, w1_ref[...])) * _dot(h, w3_ref[...])
    acc_scr[...] += _dot(act.astype(BF16), w2_ref[...])

    @pl.when(j == pl.num_programs(1) - 1)
    def _():
        d = x_ref.shape[1]
        gate = _gate_tile(mod_ref, i, tm, tpb, lc, 5, 0, d, d)
        o_ref[...] = x_ref[...] + gate * acc_scr[...]


def _ffn_dense(x, mod, w1, w3, w2, t, lc):
    m, d = x.shape
    f = w1.shape[1]
    tm, tf = 768, 512
    w1, w3, w2 = w1.astype(BF16), w3.astype(BF16), w2.astype(BF16)
    return pl.pallas_call(
        functools.partial(_ffn_kernel, tm=tm, tpb=t // tm, lc=lc),
        grid=(m // tm, f // tf),
        in_specs=[pl.BlockSpec((tm, d), lambda i, j: (i, 0), pipeline_mode=pl.Buffered(1)),
                  pl.BlockSpec(mod.shape, lambda i, j: (0, 0)),
                  pl.BlockSpec((d, tf), lambda i, j: (0, j)),
                  pl.BlockSpec((d, tf), lambda i, j: (0, j)),
                  pl.BlockSpec((tf, d), lambda i, j: (j, 0))],
        out_specs=pl.BlockSpec((tm, d), lambda i, j: (i, 0), pipeline_mode=pl.Buffered(1)),
        out_shape=jax.ShapeDtypeStruct((m, d), F32),
        scratch_shapes=[pltpu.VMEM((tm, d), BF16), pltpu.VMEM((tm, d), F32)],
        compiler_params=_params(("arbitrary", "arbitrary")),
        name="ffn_dense",
    )(x, mod, w1, w3, w2)


def _router_kernel(x_ref, mod_ref, wr_ref, h_ref, r_ref, *, tm, tpb, lc):
    i = pl.program_id(0)
    h = _adaln_tile(x_ref[...], mod_ref, i, tm, tpb, lc, 3, 4)
    h_ref[...] = h
    lane = lax.broadcasted_iota(jnp.int32, (tm, LANES), 1).astype(F32)
    logits = jnp.where(lane < N_EXPERTS, _dot_f32(h, wr_ref[...]), -jnp.inf)
    v1 = jnp.max(logits, axis=-1, keepdims=True)
    i1 = jnp.min(jnp.where(logits == v1, lane, float(LANES)), axis=-1, keepdims=True)
    rest = jnp.where(lane == i1, -jnp.inf, logits)
    v2 = jnp.max(rest, axis=-1, keepdims=True)
    i2 = jnp.min(jnp.where(rest == v2, lane, float(LANES)), axis=-1, keepdims=True)
    e2 = jnp.exp(v2 - v1)
    g1 = 1.0 / (1.0 + e2)
    g2 = e2 / (1.0 + e2)
    r_ref[...] = jnp.where(lane == 0, i1, jnp.where(lane == 1, i2, jnp.where(lane == 2, g1, jnp.where(lane == 3, g2, 0.0))))


def _router(x, mod, w_router, t, lc):
    m, d = x.shape
    tm = 768
    wr = jnp.pad(w_router, ((0, 0), (0, LANES - w_router.shape[1])))
    return pl.pallas_call(
        functools.partial(_router_kernel, tm=tm, tpb=t // tm, lc=lc),
        grid=(m // tm,),
        in_specs=[pl.BlockSpec((tm, d), lambda i: (i, 0)),
                  pl.BlockSpec(mod.shape, lambda i: (0, 0)),
                  pl.BlockSpec((d, LANES), lambda i: (0, 0))],
        out_specs=[pl.BlockSpec((tm, d), lambda i: (i, 0)),
                   pl.BlockSpec((tm, LANES), lambda i: (i, 0))],
        out_shape=[jax.ShapeDtypeStruct((m, d), F32), jax.ShapeDtypeStruct((m, LANES), F32)],
        compiler_params=_params(("arbitrary",)),
        name="moe_router",
    )(x, mod, wr)


def _gather_kernel(src_ref, h_ref, o_ref, buf, sem, *, tg):
    g = pl.program_id(0)
    last = pl.num_programs(0) - 1
    slot = g % 2

    def row_copy(idx, r, s):
        return pltpu.make_async_copy(h_ref.at[pl.ds(idx, 1)], buf.at[s, pl.ds(r, 1)], sem.at[s])

    def group_valid(gg):
        return src_ref[gg * tg] >= 0

    def issue(gg, s):
        def body(r, carry):
            row_copy(jnp.maximum(src_ref[gg * tg + r], 0), r, s).start()
            return carry
        lax.fori_loop(0, tg, body, 0, unroll=8)

    @pl.when((g == 0) & group_valid(0))
    def _():
        issue(0, 0)

    nxt = jnp.minimum(g + 1, last)

    @pl.when((g < last) & group_valid(nxt))
    def _():
        issue(nxt, 1 - slot)

    @pl.when(group_valid(g))
    def _():
        def body(r, carry):
            row_copy(0, r, slot).wait()
            return carry
        lax.fori_loop(0, tg, body, 0, unroll=8)
        o_ref[...] = buf[slot].astype(BF16)

    @pl.when(jnp.logical_not(group_valid(g)))
    def _():
        o_ref[...] = jnp.zeros_like(o_ref)


def _gather_rows(src, h, n_rows):
    d = h.shape[1]
    tg = MOE_TS
    return pl.pallas_call(
        functools.partial(_gather_kernel, tg=tg),
        grid_spec=pltpu.PrefetchScalarGridSpec(
            num_scalar_prefetch=1,
            grid=(n_rows // tg,),
            in_specs=[pl.BlockSpec(memory_space=pl.ANY)],
            out_specs=pl.BlockSpec((tg, d), lambda i, src: (i, 0)),
            scratch_shapes=[pltpu.VMEM((2, tg, d), h.dtype), pltpu.SemaphoreType.DMA((2,))]),
        out_shape=jax.ShapeDtypeStruct((n_rows, d), BF16),
        compiler_params=_params(("arbitrary",)),
        name="moe_gather",
    )(src, h)


def _moe_ffn_kernel(te_ref, tr_ref, nr_ref, hs_ref, w1_ref, w3_ref, w2_ref, o_ref, *, ts):
    i = pl.program_id(0)
    j = pl.program_id(1)
    n_rows = nr_ref[i]

    @pl.when(j == 0)
    def _():
        o_ref[...] = jnp.zeros_like(o_ref)

    @pl.when(n_rows > 0)
    def _():
        w1 = w1_ref[...].astype(BF16)
        w3 = w3_ref[...].astype(BF16)
        w2 = w2_ref[...].astype(BF16)
        for s in range(hs_ref.shape[0] // ts):
            @pl.when(s * ts < n_rows)
            def _():
                rows = slice(s * ts, (s + 1) * ts)
                h = hs_ref[rows, :]
                act = _silu(_dot(h, w1)) * _dot(h, w3)
                o_ref[rows, :] += _dot(act.astype(BF16), w2)


def _moe_ffn(tile_expert, tile_row, tile_rows, hs, w1, w3, w2, layer):
    r, d = hs.shape
    f = w1.shape[3]
    tm, tf = MOE_TM, 256
    nf = f // tf
    jj = lambda i, j, nr: jnp.where(nr[i] > 0, j, nf - 1)
    return pl.pallas_call(
        functools.partial(_moe_ffn_kernel, ts=MOE_TS),
        grid_spec=pltpu.PrefetchScalarGridSpec(
            num_scalar_prefetch=3,
            grid=(r // tm, nf),
            in_specs=[pl.BlockSpec((tm, d), lambda i, j, te, tr, nr: (tr[i], 0)),
                      pl.BlockSpec((None, None, d, tf), lambda i, j, te, tr, nr: (layer, te[i], 0, jj(i, j, nr))),
                      pl.BlockSpec((None, None, d, tf), lambda i, j, te, tr, nr: (layer, te[i], 0, jj(i, j, nr))),
                      pl.BlockSpec((None, None, tf, d), lambda i, j, te, tr, nr: (layer, te[i], jj(i, j, nr), 0))],
            out_specs=pl.BlockSpec((tm, d), lambda i, j, te, tr, nr: (i, 0), pipeline_mode=pl.Buffered(1))),
        out_shape=jax.ShapeDtypeStruct((r, d), F32),
        compiler_params=_params(("arbitrary", "arbitrary")),
        name="moe_ffn",
    )(tile_expert, tile_row, tile_rows, hs, w1, w3, w2)


def _combine_kernel(p1_ref, p2_ref, y_ref, x_ref, r_ref, mod_ref, o_ref, y1_scr, y2_scr, sem, *, tc, tpb, lc):
    i = pl.program_id(0)
    last = pl.num_programs(0) - 1
    slot = i % 2

    def copies(idx1, idx2, r, s):
        return (pltpu.make_async_copy(y_ref.at[pl.ds(idx1, 1)], y1_scr.at[s, pl.ds(r, 1)], sem.at[s]),
                pltpu.make_async_copy(y_ref.at[pl.ds(idx2, 1)], y2_scr.at[s, pl.ds(r, 1)], sem.at[s]))

    def issue(ii, s):
        def body(r, carry):
            c1, c2 = copies(p1_ref[ii * tc + r], p2_ref[ii * tc + r], r, s)
            c1.start()
            c2.start()
            return carry
        lax.fori_loop(0, tc, body, 0, unroll=8)

    @pl.when(i == 0)
    def _():
        issue(0, 0)

    @pl.when(i < last)
    def _():
        issue(jnp.minimum(i + 1, last), 1 - slot)

    def drain(r, carry):
        c1, c2 = copies(0, 0, r, slot)
        c1.wait()
        c2.wait()
        return carry

    lax.fori_loop(0, tc, drain, 0, unroll=8)
    d = x_ref.shape[1]
    gate = _gate_tile(mod_ref, i, tc, tpb, lc, 5, 0, d, d)
    route = r_ref[...]
    f = route[:, 2:3] * y1_scr[slot] + route[:, 3:4] * y2_scr[slot]
    o_ref[...] = x_ref[...] + gate * f


def _combine(pos1, pos2, y, x, route, mod, t, lc):
    m, d = x.shape
    tc = 256
    return pl.pallas_call(
        functools.partial(_combine_kernel, tc=tc, tpb=t // tc, lc=lc),
        grid_spec=pltpu.PrefetchScalarGridSpec(
            num_scalar_prefetch=2,
            grid=(m // tc,),
            in_specs=[pl.BlockSpec(memory_space=pl.ANY),
                      pl.BlockSpec((tc, d), lambda i, p1, p2: (i, 0)),
                      pl.BlockSpec((tc, LANES), lambda i, p1, p2: (i, 0)),
                      pl.BlockSpec(mod.shape, lambda i, p1, p2: (0, 0))],
            out_specs=pl.BlockSpec((tc, d), lambda i, p1, p2: (i, 0)),
            scratch_shapes=[pltpu.VMEM((2, tc, d), F32), pltpu.VMEM((2, tc, d), F32),
                            pltpu.SemaphoreType.DMA((2,))]),
        out_shape=jax.ShapeDtypeStruct((m, d), F32),
        compiler_params=_params(("arbitrary",)),
        name="moe_combine",
    )(pos1, pos2, y, x, route, mod)


def _ffn_moe(x, mod, w_router, w1, w3, w2, layer, t, lc):
    m, d = x.shape
    tm = MOE_TM
    h, route = _router(x, mod, w_router, t, lc)
    experts = route[:, 0:TOP_K].astype(jnp.int32).reshape(-1)
    onehot = (experts[:, None] == jnp.arange(N_EXPERTS)[None, :]).astype(jnp.int32)
    rank = jnp.sum((jnp.cumsum(onehot, axis=0) - onehot) * onehot, axis=1)
    count = jnp.sum(onehot, axis=0)
    tiles = (count + tm - 1) // tm
    tile_end = jnp.cumsum(tiles)
    tile_start = tile_end - tiles
    pos = (tile_start * tm)[experts] + rank
    n_tiles = (TOP_K * m + N_EXPERTS * (tm - 1)) // tm
    n_rows = n_tiles * tm
    src = jnp.full((n_rows,), -1, jnp.int32).at[pos].set(jnp.arange(TOP_K * m, dtype=jnp.int32) // TOP_K)
    tile_ids = jnp.arange(n_tiles)
    tile_row = jnp.minimum(tile_ids, tile_end[-1] - 1)
    tile_expert = jnp.sum((tile_row[:, None] >= tile_end[None, :]).astype(jnp.int32), axis=1)
    tile_rows = jnp.clip(count[tile_expert] - (tile_ids - tile_start[tile_expert]) * tm, 0, tm)
    tile_rows = jnp.where(tile_ids < tile_end[-1], tile_rows, 0)
    hs = _gather_rows(src, h, n_rows)
    y = _moe_ffn(tile_expert.astype(jnp.int32), tile_row.astype(jnp.int32), tile_rows.astype(jnp.int32),
                 hs, w1, w3, w2, layer)
    pos = pos.reshape(m, TOP_K)
    return _combine(pos[:, 0], pos[:, 1], y, x, route, mod, t, lc)


def _rope_tables(n_lat, lc, rot_dim, lane0, period):
    tok = jnp.arange(n_lat)
    axis_dim = rot_dim // 2
    inv_freq = ROPE_BASE ** (-jnp.arange(0, axis_dim, 2, dtype=F32) / axis_dim)
    ang_r = (tok // GRID_W).astype(F32)[:, None] * inv_freq
    ang_c = (tok % GRID_W).astype(F32)[:, None] * inv_freq
    ang = jnp.concatenate([ang_r, ang_r, ang_c, ang_c], axis=-1)
    cos, sin = jnp.cos(ang), jnp.sin(ang)
    seg = rot_dim // 4
    lane = np.arange(LANES)
    rel = (lane - lane0) % period
    active = (lane >= lane0) & (rel < rot_dim)
    even = ((rel // seg) % 2 == 0)
    idx = np.where(active, rel, 0)
    cos_t = jnp.where(active[None, :], cos[:, idx], 1.0)
    sin_t = jnp.where(active[None, :], sin[:, idx], 0.0)
    sin_next = jnp.where(even[None, :], -sin_t, 0.0)
    sin_prev = jnp.where(even[None, :], 0.0, sin_t)
    ctx = lambda a, v: jnp.concatenate([jnp.full((lc, LANES), v, F32), a], axis=0)
    return ctx(cos_t, 1.0), ctx(sin_next, 0.0), ctx(sin_prev, 0.0)


def _relayout_w_in(w):
    d = w.shape[0]
    seg = lambda lo, n: w[:, lo:lo + n]
    z = lambda n: jnp.zeros((d, n), w.dtype)
    a0, b0, g0, c0, d0 = 0, 1536, 3584, 3600, 4144
    parts = [seg(c0, 384), seg(c0 + 384, 128),
             seg(a0, 1536), seg(b0, 2048), seg(d0, 1536),
             seg(g0, 16), z(KR_LANE - 16), seg(c0 + 512, 32), z(LANES - KR_LANE - 32)]
    return jnp.concatenate(parts, axis=1).astype(BF16)


def _mla_weights(w_uq, w_ukv):
    hd = MLA_NOPE + MLA_ROPE
    wq = jnp.pad(w_uq.reshape(MLA_Q_RANK, MLA_HEADS, hd), ((0, 0), (0, 0), (0, LANES - hd)))
    wkv = w_ukv.reshape(MLA_KV_RANK, MLA_HEADS, MLA_NOPE + MLA_V)
    wk = jnp.pad(wkv[:, :, :MLA_NOPE], ((0, 0), (0, 0), (0, LANES - MLA_NOPE)))
    wv = wkv[:, :, MLA_NOPE:]
    return (wq.reshape(MLA_Q_RANK, -1).astype(BF16), wk.reshape(MLA_KV_RANK, -1).astype(BF16),
            wv.reshape(MLA_KV_RANK, -1).astype(BF16))


def kernel(x, c, ctx, c_ctx, w_ada, b_ada, w_in, w_out, da_q_gain, da_k_gain, da_lambda, da_out_gain, ml_conv_w, ml_conv_b, ml_gate_b, ml_out_gain, mla_cq_gain, mla_ckv_gain, mla_w_uq, mla_w_ukv, mla_q_gain, mla_k_gain, na_q_gain, na_k_gain, na_rpb, ffn_w1, ffn_w3, ffn_w2, moe_router, moe_w1, moe_w3, moe_w2):
    b, n_lat, d = x.shape
    lc = ctx.shape[1]
    t = lc + n_lat
    depth = w_in.shape[0]
    assert b <= 4 and lc == TQ and n_lat % (NA_QR * GRID_W) == 0 and t % 768 == 0

    xs = jnp.concatenate([ctx, x], axis=1).reshape(b * t, d)
    cond = jnp.zeros((8, d), F32).at[:b].set(c).at[4].set(c_ctx)
    mod = _mod_table(cond, w_ada, b_ada)
    rope_da = _rope_tables(n_lat, lc, DA_QK, 0, DA_QK)
    rope_mla = _rope_tables(n_lat, lc, MLA_ROPE, KR_LANE, LANES)

    for l in range(depth):
        lam_init = 0.8 - 0.6 * math.exp(-0.3 * l)
        p = _inproj(xs, mod[l], _relayout_w_in(w_in[l]), t, lc)
        mix_a = _mixer_diff(p, rope_da, da_q_gain[l], da_k_gain[l], da_lambda[l], da_out_gain[l], lam_init, b, t, lc)
        mix_b = _mixer_mlstm(p, ml_conv_w[l], ml_conv_b[l], ml_gate_b[l], ml_out_gain[l], b, t, lc)
        wq, wk, wv = _mla_weights(mla_w_uq[l], mla_w_ukv[l])
        qc, kc, vc = _mla_prep(p, rope_mla, mla_cq_gain[l], mla_ckv_gain[l], wq, wk, wv,
                               mla_q_gain[l], mla_k_gain[l], t)
        mix_c = _mla_attn(qc, kc, vc, b, t, lc)
        mix_d = _mixer_na(p, na_q_gain[l], na_k_gain[l], na_rpb[l], b, t, lc)
        xs = _outproj((mix_a, mix_b, mix_c, mix_d), w_out[l].astype(BF16), xs, mod[l], t, lc)
        if l % 2 == 0:
            xs = _ffn_dense(xs, mod[l], ffn_w1[l // 2], ffn_w3[l // 2], ffn_w2[l // 2], t, lc)
        else:
            xs = _ffn_moe(xs, mod[l], moe_router[l // 2], moe_w1, moe_w3, moe_w2, l // 2, t, lc)
    return xs.reshape(b, t, d)[:, lc:]
```

```python
import functools
import math

import numpy as np
import jax
import jax.numpy as jnp
from jax import lax
from jax.experimental import pallas as pl
from jax.experimental.pallas import tpu as pltpu

F32 = jnp.float32
BF16 = jnp.bfloat16

GRID_W = 64
DA_HEADS, DA_QK, DA_V = 4, 64, 128
ML_HEADS, ML_DIM, ML_CONV = 4, 128, 3
MLA_HEADS, MLA_NOPE, MLA_ROPE, MLA_V = 8, 64, 32, 64
MLA_Q_RANK, MLA_KV_RANK = 384, 128
NA_HEADS, NA_DIM, NA_WIN_R, NA_WIN_C = 8, 64, 8, 16
N_EXPERTS, TOP_K = 8, 2
ROPE_BASE = 10000.0
RMS_EPS = 1e-6
LOG2E = math.log2(math.e)

LANES = 128
VMEM_LIMIT = 56 * 1024 * 1024

P_CQ, P_CKV = 0, 384
P_AQ, P_AK, P_AV = 512, 1024, 1536
P_BQ, P_BK, P_BV, P_BO = 2048, 2560, 3072, 3584
P_DQ, P_DK, P_DV = 4096, 4608, 5120
P_LAST = 5632
P_WIDTH = 5760
KR_LANE = 64

TQ_DIFF, TQ_MLA = 384, 768
ML_L = 256
NA_QR, NA_KR = 4, 12
MOE_TM, MOE_TS = 1280, 256


def _params(sem):
    return pltpu.CompilerParams(dimension_semantics=sem, vmem_limit_bytes=VMEM_LIMIT)


def _silu(x):
    return x * (1.0 / (1.0 + jnp.exp(-x)))


def _sigmoid(x):
    return 1.0 / (1.0 + jnp.exp(-x))


def _rms(x, n=None):
    n = x.shape[-1] if n is None else n
    return x * lax.rsqrt(jnp.sum(x * x, axis=-1, keepdims=True) * (1.0 / n) + RMS_EPS)


def _rms_halves(x, gain):
    lo = lax.broadcasted_iota(jnp.int32, x.shape, 1) < 64
    x2 = x * x
    s_lo = jnp.sum(jnp.where(lo, x2, 0.0), axis=-1, keepdims=True)
    s_hi = jnp.sum(jnp.where(lo, 0.0, x2), axis=-1, keepdims=True)
    ms = jnp.where(lo, s_lo, s_hi) * (1.0 / 64)
    return x * lax.rsqrt(ms + RMS_EPS) * gain


def _rope(x, cos, sin_next, sin_prev, seg):
    return x * cos + pltpu.roll(x, LANES - seg, 1) * sin_next + pltpu.roll(x, seg, 1) * sin_prev


def _softmax2_pv(s2, v):
    e = jnp.exp2(s2 - jnp.max(s2, axis=-1, keepdims=True))
    return _dot(e.astype(BF16), v) * (1.0 / jnp.sum(e, axis=-1, keepdims=True))


def _hide_latent_keys(s, tq, lc):
    row = lax.broadcasted_iota(jnp.int32, s.shape, 0) % tq
    key = lax.broadcasted_iota(jnp.int32, s.shape, 1)
    return jnp.where(row < lc, jnp.where(key < lc, s, -jnp.inf), s)


def _dot(a, b):
    return jnp.dot(a, b, preferred_element_type=F32)


def _dot_nt(a, b):
    return lax.dot_general(a, b, (((1,), (1,)), ((), ())), preferred_element_type=F32)


def _dot_tn(a, b):
    return lax.dot_general(a, b, (((0,), (0,)), ((), ())), preferred_element_type=F32)


def _split3(a):
    a1 = a.astype(BF16)
    r = a - a1.astype(F32)
    a2 = r.astype(BF16)
    a3 = (r - a2.astype(F32)).astype(BF16)
    return a1, a2, a3


def _dot_f32(a, b):
    a1, a2, a3 = _split3(a)
    b1, b2, b3 = _split3(b)
    return (_dot(a1, b1) + (_dot(a1, b2) + _dot(a2, b1))
            + (_dot(a1, b3) + _dot(a2, b2) + _dot(a3, b1)))


def _adaln_tile(x, mod_ref, i, tm, tpb, lc, c_shift, c_scale):
    d = x.shape[1]
    b = i // tpb
    row = (i % tpb) * tm + lax.broadcasted_iota(jnp.int32, (tm, 1), 0)
    is_ctx = row < lc
    shift = jnp.where(is_ctx, mod_ref[4:5, c_shift * d:(c_shift + 1) * d],
                      mod_ref[pl.ds(b, 1), c_shift * d:(c_shift + 1) * d])
    scale = jnp.where(is_ctx, mod_ref[4:5, c_scale * d:(c_scale + 1) * d],
                      mod_ref[pl.ds(b, 1), c_scale * d:(c_scale + 1) * d])
    return _rms(x) * (1.0 + scale) + shift


def _gate_tile(mod_ref, i, tm, tpb, lc, c_gate, col0, width, d):
    b = i // tpb
    row = (i % tpb) * tm + lax.broadcasted_iota(jnp.int32, (tm, 1), 0)
    lo = c_gate * d + col0
    return jnp.where(row < lc, mod_ref[4:5, lo:lo + width], mod_ref[pl.ds(b, 1), lo:lo + width])


def _mod_kernel(c_ref, w_ref, b_ref, o_ref):
    s = _silu(c_ref[...]).astype(BF16)
    o_ref[...] = _dot(s, w_ref[...].astype(BF16)) + b_ref[...]


def _mod_table(cond, w_ada, b_ada):
    depth, d, n = w_ada.shape
    tn = 1024
    return pl.pallas_call(
        _mod_kernel,
        grid=(depth, n // tn),
        in_specs=[pl.BlockSpec((8, d), lambda l, j: (0, 0)),
                  pl.BlockSpec((None, d, tn), lambda l, j: (l, 0, j)),
                  pl.BlockSpec((None, 1, tn), lambda l, j: (l, 0, j))],
        out_specs=pl.BlockSpec((None, 8, tn), lambda l, j: (l, 0, j)),
        out_shape=jax.ShapeDtypeStruct((depth, 8, n), F32),
        compiler_params=_params(("arbitrary", "arbitrary")),
        name="mod_table",
    )(cond, w_ada, b_ada.reshape(depth, 1, n))


def _inproj_kernel(x_ref, mod_ref, w_ref, o_ref, h_scr, *, tm, tpb, lc):
    i = pl.program_id(0)

    @pl.when(pl.program_id(1) == 0)
    def _():
        h_scr[...] = _adaln_tile(x_ref[...], mod_ref, i, tm, tpb, lc, 0, 1).astype(BF16)

    o_ref[...] = _dot(h_scr[...], w_ref[...])


def _inproj(x, mod, w, t, lc):
    m, d = x.shape
    n = w.shape[1]
    tm, tn = 768, 1152
    return pl.pallas_call(
        functools.partial(_inproj_kernel, tm=tm, tpb=t // tm, lc=lc),
        grid=(m // tm, n // tn),
        in_specs=[pl.BlockSpec((tm, d), lambda i, j: (i, 0)),
                  pl.BlockSpec(mod.shape, lambda i, j: (0, 0)),
                  pl.BlockSpec((d, tn), lambda i, j: (0, j))],
        out_specs=pl.BlockSpec((tm, tn), lambda i, j: (i, j)),
        out_shape=jax.ShapeDtypeStruct((m, n), F32),
        scratch_shapes=[pltpu.VMEM((tm, d), BF16)],
        compiler_params=_params(("arbitrary", "arbitrary")),
        name="inproj",
    )(x, mod, w)


def _diff_attn_kernel(q_ref, k_ref, v_ref, cos_ref, sn_ref, sp_ref, qg_ref, kg_ref, og_ref, lam_ref,
                      o_ref, kn_scr, vb_scr, *, tq, lc, lam_init, scale):
    qi = pl.program_id(2)
    t = k_ref.shape[0]

    @pl.when(qi == 0)
    def _():
        k = _rms_halves(k_ref[...], kg_ref[...])
        kn_scr[...] = _rope(k, cos_ref[...], sn_ref[...], sp_ref[...], 16).astype(BF16)
        vb_scr[...] = v_ref[...].astype(BF16)

    r0 = pl.multiple_of(qi * tq, tq)
    q = _rms_halves(q_ref[...], qg_ref[...])
    q = _rope(q, cos_ref[pl.ds(r0, tq), :], sn_ref[pl.ds(r0, tq), :], sp_ref[pl.ds(r0, tq), :], 16) * scale
    lo = lax.broadcasted_iota(jnp.int32, q.shape, 1) < 64
    qs = jnp.concatenate([jnp.where(lo, q, 0.0), jnp.where(lo, 0.0, q)], axis=0).astype(BF16)
    lp = lam_ref[...]
    lam = (jnp.exp(jnp.sum(lp[0:1] * lp[1:2], axis=-1, keepdims=True))
           - jnp.exp(jnp.sum(lp[2:3] * lp[3:4], axis=-1, keepdims=True)) + lam_init)

    def attend(first):
        s = _dot_nt(qs, kn_scr[...])
        if first:
            s = _hide_latent_keys(s, tq, lc)
        o2 = _softmax2_pv(s, vb_scr[...])
        o = o2[0:tq] - lam * o2[tq:2 * tq]
        o_ref[...] = (_rms(o) * og_ref[...] * (1.0 - lam_init)).astype(BF16)

    @pl.when(qi == 0)
    def _():
        attend(True)

    @pl.when(qi > 0)
    def _():
        attend(False)


def _mixer_diff(p, tabs, q_gain, k_gain, lam_params, out_gain, lam_init, b, t, lc):
    m = p.shape[0]
    tq = TQ_DIFF
    nq = t // tq
    cos, sn, sp = tabs
    tile2 = lambda g: jnp.tile(g, 2).reshape(1, LANES)
    full = lambda a: pl.BlockSpec(a.shape, lambda bb, h, qi: (0,) * a.ndim)
    args = (cos, sn, sp, tile2(q_gain), tile2(k_gain), out_gain.reshape(1, LANES), lam_params)
    return pl.pallas_call(
        functools.partial(_diff_attn_kernel, tq=tq, lc=lc, lam_init=lam_init, scale=DA_QK ** -0.5 * LOG2E),
        grid=(b, DA_HEADS, nq),
        in_specs=[pl.BlockSpec((tq, LANES), lambda bb, h, qi: (bb * nq + qi, P_AQ // LANES + h)),
                  pl.BlockSpec((t, LANES), lambda bb, h, qi: (bb, P_AK // LANES + h)),
                  pl.BlockSpec((t, LANES), lambda bb, h, qi: (bb, P_AV // LANES + h))]
                 + [full(a) for a in args],
        out_specs=pl.BlockSpec((tq, LANES), lambda bb, h, qi: (bb * nq + qi, h)),
        out_shape=jax.ShapeDtypeStruct((m, DA_HEADS * DA_V), BF16),
        scratch_shapes=[pltpu.VMEM((t, LANES), BF16), pltpu.VMEM((t, LANES), BF16)],
        compiler_params=_params(("arbitrary", "arbitrary", "arbitrary")),
        name="diff_attn",
    )(p, p, p, *args)


def _mla_prep_kernel(cq_ref, ckv_ref, last_ref, cos_ref, sn_ref, sp_ref, cqg_ref, ckvg_ref, wq_ref, wk_ref,
                     wv_ref, qg_ref, kg_ref, q_out, k_out, v_out, *, scale):
    hd = MLA_NOPE + MLA_ROPE
    cos, sn, sp = cos_ref[...], sn_ref[...], sp_ref[...]
    cqn = (_rms(cq_ref[...]) * cqg_ref[...]).astype(BF16)
    ckvn = (_rms(ckv_ref[...]) * ckvg_ref[...]).astype(BF16)
    q = _dot(cqn, wq_ref[...])
    kk = _dot(ckvn, wk_ref[...])
    v_out[...] = _dot(ckvn, wv_ref[...]).astype(BF16)
    last = last_ref[...]
    lane = lax.broadcasted_iota(jnp.int32, last.shape, 1)
    kr = jnp.where((lane >= KR_LANE) & (lane < KR_LANE + MLA_ROPE), last, 0.0)
    for h in range(MLA_HEADS):
        sl = slice(h * LANES, (h + 1) * LANES)
        qh = _rms(q[:, sl], hd) * qg_ref[...]
        q_out[:, sl] = (_rope(qh, cos, sn, sp, 8) * scale).astype(BF16)
        kh = _rms(kk[:, sl] + kr, hd) * kg_ref[...]
        k_out[:, sl] = _rope(kh, cos, sn, sp, 8).astype(BF16)


def _mla_prep(p, tabs, cq_gain, ckv_gain, wq, wk, wv, q_gain, k_gain, t):
    m = p.shape[0]
    tm = 768
    tpb = t // tm
    hw = MLA_HEADS * LANES
    cos, sn, sp = tabs
    tab = pl.BlockSpec((tm, LANES), lambda i: (i % tpb, 0))
    full = lambda a: pl.BlockSpec(a.shape, lambda i: (0,) * a.ndim)
    pad = lambda g: jnp.pad(g, (0, LANES - g.shape[0])).reshape(1, LANES)
    consts = (cq_gain.reshape(1, -1), ckv_gain.reshape(1, -1), wq, wk, wv, pad(q_gain), pad(k_gain))
    return pl.pallas_call(
        functools.partial(_mla_prep_kernel, scale=(MLA_NOPE + MLA_ROPE) ** -0.5 * LOG2E),
        grid=(m // tm,),
        in_specs=[pl.BlockSpec((tm, MLA_Q_RANK), lambda i: (i, P_CQ // MLA_Q_RANK)),
                  pl.BlockSpec((tm, LANES), lambda i: (i, P_CKV // LANES)),
                  pl.BlockSpec((tm, LANES), lambda i: (i, P_LAST // LANES)),
                  tab, tab, tab] + [full(a) for a in consts],
        out_specs=[pl.BlockSpec((tm, hw), lambda i: (i, 0)),
                   pl.BlockSpec((tm, hw), lambda i: (i, 0)),
                   pl.BlockSpec((tm, MLA_HEADS * MLA_V), lambda i: (i, 0))],
        out_shape=[jax.ShapeDtypeStruct((m, hw), BF16), jax.ShapeDtypeStruct((m, hw), BF16),
                   jax.ShapeDtypeStruct((m, MLA_HEADS * MLA_V), BF16)],
        compiler_params=_params(("arbitrary",)),
        name="mla_prep",
    )(p, p, p, cos, sn, sp, *consts)


def _mla_attn_kernel(q_ref, k_ref, v_ref, o_ref, *, tq, lc):
    def attend(first):
        outs = []
        for h in range(2):
            sl = slice(h * LANES, (h + 1) * LANES)
            s = _dot_nt(q_ref[:, sl], k_ref[:, sl])
            if first:
                s = _hide_latent_keys(s, tq, lc)
            outs.append(_softmax2_pv(s, v_ref[...]))
        lo = lax.broadcasted_iota(jnp.int32, outs[0].shape, 1) < MLA_V
        o_ref[...] = jnp.where(lo, outs[0], outs[1]).astype(BF16)

    @pl.when(pl.program_id(2) == 0)
    def _():
        attend(True)

    @pl.when(pl.program_id(2) > 0)
    def _():
        attend(False)


def _mla_attn(q, k, v, b, t, lc):
    m = q.shape[0]
    tq = TQ_MLA
    nq = t // tq
    return pl.pallas_call(
        functools.partial(_mla_attn_kernel, tq=tq, lc=lc),
        grid=(b, MLA_HEADS // 2, nq),
        in_specs=[pl.BlockSpec((tq, 2 * LANES), lambda bb, hp, qi: (bb * nq + qi, hp)),
                  pl.BlockSpec((t, 2 * LANES), lambda bb, hp, qi: (bb, hp)),
                  pl.BlockSpec((t, LANES), lambda bb, hp, qi: (bb, hp))],
        out_specs=pl.BlockSpec((tq, LANES), lambda bb, hp, qi: (bb * nq + qi, hp)),
        out_shape=jax.ShapeDtypeStruct((m, MLA_HEADS * MLA_V), BF16),
        compiler_params=_params(("arbitrary", "arbitrary", "arbitrary")),
        name="mla_attn",
    )(q, k, v)


def _na_kernel(q_ref, k_ref, v_ref, bias_ref, qg_ref, kg_ref, o_ref, kn_scr, vb_scr, *, lc, rows, scale):
    s = pl.program_id(1)
    tq = q_ref.shape[0]
    n_ctx_blk = lc // tq
    band = NA_KR * GRID_W
    npair = NA_HEADS // 2

    @pl.when(s == 0)
    def _():
        for pp in range(npair):
            sl = slice(pp * LANES, (pp + 1) * LANES)
            kn_scr[:, sl] = _rms_halves(k_ref[:, sl], kg_ref[:, sl]).astype(BF16)
        vb_scr[...] = v_ref[...].astype(BF16)

    def heads(pv_fn):
        for pp in range(npair):
            sl = slice(pp * LANES, (pp + 1) * LANES)
            q = _rms_halves(q_ref[:, sl], qg_ref[:, sl]) * scale
            lo = lax.broadcasted_iota(jnp.int32, q.shape, 1) < NA_DIM
            qs = jnp.concatenate([jnp.where(lo, q, 0.0), jnp.where(lo, 0.0, q)], axis=0).astype(BF16)
            o2 = pv_fn(qs, sl, pp)
            o_ref[:, sl] = jnp.where(lo, o2[0:tq], o2[tq:2 * tq]).astype(BF16)

    @pl.when(s < n_ctx_blk)
    def _():
        def f(qs, sl, pp):
            return _softmax2_pv(_dot_nt(qs, kn_scr[0:lc, sl]), vb_scr[0:lc, sl])
        heads(f)

    @pl.when(s >= n_ctx_blk)
    def _():
        rb = s - n_ctx_blk
        krow = jnp.clip(rb * NA_QR - NA_WIN_R // 2, 0, rows - NA_KR)
        k0 = pl.multiple_of(lc + krow * GRID_W, GRID_W)

        def f(qs, sl, pp):
            s_loc = _dot_nt(qs, kn_scr[pl.ds(k0, band), sl]) + bias_ref[pp]
            s_ctx = _dot_nt(qs, kn_scr[0:lc, sl])
            mx = jnp.maximum(jnp.max(s_loc, axis=-1, keepdims=True), jnp.max(s_ctx, axis=-1, keepdims=True))
            e_loc = jnp.exp2(s_loc - mx)
            e_ctx = jnp.exp2(s_ctx - mx)
            den = jnp.sum(e_loc, axis=-1, keepdims=True) + jnp.sum(e_ctx, axis=-1, keepdims=True)
            num = _dot(e_loc.astype(BF16), vb_scr[pl.ds(k0, band), sl]) + _dot(e_ctx.astype(BF16), vb_scr[0:lc, sl])
            return num * (1.0 / den)
        heads(f)


def _na_variant(rb, n_rb):
    return jnp.where(rb == 0, 0, jnp.where(rb == n_rb - 1, 2, 1))


def _na_bias_table(rpb, rows):
    n_rb = rows // NA_QR
    rbs = np.array([0, 1, n_rb - 1])
    q_rows = rbs[:, None] * NA_QR + np.arange(NA_QR)
    key_rows = np.clip(rbs * NA_QR - NA_WIN_R // 2, 0, rows - NA_KR)[:, None] + np.arange(NA_KR)
    cols = np.arange(GRID_W)
    r0 = np.clip(q_rows - NA_WIN_R // 2, 0, rows - NA_WIN_R)[:, :, None]
    c0 = np.clip(cols - NA_WIN_C // 2, 0, GRID_W - NA_WIN_C)[:, None]
    kr = key_rows[:, None, :]
    ok_r = (kr >= r0) & (kr < r0 + NA_WIN_R)
    ok_c = (cols[None, :] >= c0) & (cols[None, :] < c0 + NA_WIN_C)
    rel_r = np.clip(kr - q_rows[:, :, None] + NA_WIN_R - 1, 0, 2 * NA_WIN_R - 2)
    rel_c = np.clip(cols[None, :] - cols[:, None] + NA_WIN_C - 1, 0, 2 * NA_WIN_C - 2)
    oh_r = jnp.asarray(np.eye(2 * NA_WIN_R - 1, dtype=np.float32)[rel_r])
    oh_c = jnp.asarray(np.eye(2 * NA_WIN_C - 1, dtype=np.float32)[rel_c])
    rpb2 = rpb.reshape(NA_HEADS // 2, 2, 2 * NA_WIN_R - 1, 2 * NA_WIN_C - 1)
    bias = jnp.einsum('rikd,phde,qce->prhiqkc', oh_r, rpb2, oh_c, precision=lax.Precision.HIGHEST)
    mask = ok_r[:, None, :, None, :, None] & ok_c[None, None, None, :, None, :]
    bias = jnp.where(jnp.asarray(mask)[None], bias * LOG2E, -jnp.inf)
    return bias.reshape(NA_HEADS // 2, len(rbs), 2 * NA_QR * GRID_W, NA_KR * GRID_W)


def _mixer_na(p, q_gain, k_gain, rpb, b, t, lc):
    m = p.shape[0]
    rows = (t - lc) // GRID_W
    tq = NA_QR * GRID_W
    nblk = t // tq
    n_ctx_blk = lc // tq
    bias = _na_bias_table(rpb, rows)
    tile8 = lambda g: jnp.tile(g, NA_HEADS).reshape(1, NA_HEADS * NA_DIM)
    w = NA_HEADS * NA_DIM
    return pl.pallas_call(
        functools.partial(_na_kernel, lc=lc, rows=rows, scale=NA_DIM ** -0.5 * LOG2E),
        grid=(b, nblk),
        in_specs=[pl.BlockSpec((tq, w), lambda bb, s: (bb * nblk + s, P_DQ // w)),
                  pl.BlockSpec((t, w), lambda bb, s: (bb, P_DK // w)),
                  pl.BlockSpec((t, w), lambda bb, s: (bb, P_DV // w)),
                  pl.BlockSpec((NA_HEADS // 2, None, 2 * tq, NA_KR * GRID_W),
                               lambda bb, s: (0, _na_variant(jnp.maximum(s - n_ctx_blk, 0), rows // NA_QR), 0, 0)),
                  pl.BlockSpec((1, w), lambda bb, s: (0, 0)),
                  pl.BlockSpec((1, w), lambda bb, s: (0, 0))],
        out_specs=pl.BlockSpec((tq, w), lambda bb, s: (bb * nblk + s, 0)),
        out_shape=jax.ShapeDtypeStruct((m, w), BF16),
        scratch_shapes=[pltpu.VMEM((t, w), BF16), pltpu.VMEM((t, w), BF16)],
        compiler_params=_params(("arbitrary", "arbitrary")),
        name="na_attn",
    )(p, p, p, bias, tile8(q_gain), tile8(k_gain))


def _ml_prep_kernel(x_ref, w_ref, b_ref, o_ref, *, lc):
    t = x_ref.shape[0]
    x = x_ref[...]
    row = lax.broadcasted_iota(jnp.int32, (t, 1), 0)
    prev = jnp.where((row == 0) | (row == lc), 0.0, pltpu.roll(x, 1, 0))
    nxt = jnp.where((row == lc - 1) | (row == t - 1), 0.0, pltpu.roll(x, t - 1, 0))
    y = _silu(prev * w_ref[0:1, :] + x * w_ref[1:2, :] + nxt * w_ref[2:3, :] + b_ref[...])
    o_ref[...] = y * jnp.where(pl.program_id(1) == 1, ML_DIM ** -0.5, 1.0)


def _ml_prep(p, conv_w, conv_b, b, t, lc):
    m = p.shape[0]
    w = ML_HEADS * ML_DIM
    return pl.pallas_call(
        functools.partial(_ml_prep_kernel, lc=lc),
        grid=(b, 2),
        in_specs=[pl.BlockSpec((t, w), lambda bb, c: (bb, P_BQ // w + c)),
                  pl.BlockSpec((ML_CONV, w), lambda bb, c: (0, c)),
                  pl.BlockSpec((1, w), lambda bb, c: (0, c))],
        out_specs=pl.BlockSpec((t, w), lambda bb, c: (bb, c)),
        out_shape=jax.ShapeDtypeStruct((m, 2 * w), F32),
        compiler_params=_params(("arbitrary", "arbitrary")),
        name="mlstm_prep",
    )(p, conv_w, conv_b.reshape(1, -1))


def _ml_scan_kernel(qf_ref, kf_ref, vf_ref, gf_ref, qb_ref, kb_ref, vb_ref, gb_ref, of_ref, ob_ref,
                    c_scr, n_scr, m_scr):
    @pl.when(pl.program_id(1) == 0)
    def _():
        c_scr[...] = jnp.zeros_like(c_scr)
        n_scr[...] = jnp.zeros_like(n_scr)
        m_scr[...] = jnp.zeros_like(m_scr)

    L = qf_ref.shape[0]
    tt = lax.broadcasted_iota(jnp.int32, (L, L), 0)
    ss = lax.broadcasted_iota(jnp.int32, (L, L), 1)
    eye = tt == ss
    dirs = ((qf_ref, kf_ref, vf_ref, gf_ref, of_ref, ss <= tt, tt <= ss),
            (qb_ref, kb_ref, vb_ref, gb_ref, ob_ref, ss >= tt, tt >= ss))
    for d, (q_ref, k_ref, v_ref, g_ref, o_ref, cm, cm_t) in enumerate(dirs):
        for h in range(ML_HEADS):
            sl = slice(h * ML_DIM, (h + 1) * ML_DIM)
            _ml_chain(q_ref[:, sl], k_ref[:, sl], v_ref[:, sl], g_ref[h], o_ref.at[:, sl],
                      c_scr.at[d, h], n_scr.at[d, h], m_scr.at[d, h], eye, cm, cm_t)


def _ml_chain(q, k, v, g, o_ref, c_scr, n_scr, m_scr, eye, cm, cm_t):
    i_row = g[0:1, :]
    f_row = g[1:2, :]
    lf_row = jnp.minimum(f_row, 0.0) - jnp.log(1.0 + jnp.exp(-jnp.abs(f_row)))
    to_col = lambda r: jnp.sum(jnp.where(eye, r, 0.0), axis=-1, keepdims=True)
    lf_col = to_col(lf_row)
    i_col = to_col(i_row)
    b_col = jnp.sum(jnp.where(cm, lf_row, 0.0), axis=-1, keepdims=True)
    b_row = jnp.sum(jnp.where(cm_t, lf_col, 0.0), axis=0, keepdims=True)
    b_end = jnp.sum(lf_row, axis=-1, keepdims=True)
    m_old = m_scr[...]
    c_old = c_scr[...]
    n_old = n_scr[...]

    log_w = jnp.where(cm, b_col - b_row + i_row, -jnp.inf)
    inter = b_col + m_old
    m_t = jnp.maximum(inter, jnp.max(log_w, axis=-1, keepdims=True))
    qb = q.astype(BF16)
    kb = k.astype(BF16)
    vb = v.astype(BF16)
    s = _dot_nt(qb, kb) * jnp.exp(log_w - m_t)
    a = jnp.exp(inter - m_t)
    num = a * _dot(qb, c_old.astype(BF16)) + _dot(s.astype(BF16), vb)
    den = a * jnp.sum(q * n_old, axis=-1, keepdims=True) + jnp.sum(s, axis=-1, keepdims=True)
    o_ref[...] = num / jnp.maximum(jnp.abs(den), jnp.exp(-m_t))

    w_end = b_end - b_col + i_col
    m_new = jnp.maximum(b_end + m_old, jnp.max(w_end, axis=0, keepdims=True))
    decay = jnp.exp(b_end + m_old - m_new)
    we = jnp.exp(w_end - m_new)
    c_scr[...] = decay * c_old + _dot_tn(kb, (we * v).astype(BF16))
    n_scr[...] = decay * n_old + jnp.sum(we * k, axis=0, keepdims=True)
    m_scr[...] = m_new


def _ml_scan(qk, p, gates, b, t, lc):
    m = p.shape[0]
    L = ML_L
    nct, ncc = t // L, lc // L
    w = ML_HEADS * ML_DIM

    fwd = lambda j: j
    bwd = lambda j: jnp.where(j < ncc, ncc - 1 - j, nct - 1 - (j - ncc))

    def specs(chunk, d):
        blk = lambda col: pl.BlockSpec((L, w), lambda bb, j: (bb * nct + chunk(j), col))
        return [blk(0), blk(1), blk(P_BV // w),
                pl.BlockSpec((None, None, None, ML_HEADS, 2, L), lambda bb, j: (bb, d, chunk(j), 0, 0, 0))]

    out = lambda chunk: pl.BlockSpec((L, w), lambda bb, j: (bb * nct + chunk(j), 0))
    nh = ML_HEADS
    return pl.pallas_call(
        _ml_scan_kernel,
        grid=(b, nct),
        in_specs=specs(fwd, 0) + specs(bwd, 1),
        out_specs=[out(fwd), out(bwd)],
        out_shape=[jax.ShapeDtypeStruct((m, w), F32)] * 2,
        scratch_shapes=[pltpu.VMEM((2, nh, ML_DIM, ML_DIM), F32), pltpu.VMEM((2, nh, 1, ML_DIM), F32),
                        pltpu.VMEM((2, nh, 1, 1), F32)],
        compiler_params=_params(("arbitrary", "arbitrary")),
        name="mlstm_scan",
    )(qk, qk, p, gates, qk, qk, p, gates)


def _ml_finish_kernel(hf_ref, hb_ref, o_ref, g_ref, out_ref):
    hsum = hf_ref[...] + hb_ref[...]
    og = _sigmoid(o_ref[...])
    for h in range(ML_HEADS):
        sl = slice(h * ML_DIM, (h + 1) * ML_DIM)
        out_ref[:, sl] = (_rms(hsum[:, sl]) * g_ref[...] * og[:, sl]).astype(BF16)


def _ml_finish(hf, hb, p, out_gain):
    m = p.shape[0]
    w = ML_HEADS * ML_DIM
    tm = 768
    return pl.pallas_call(
        _ml_finish_kernel,
        grid=(m // tm,),
        in_specs=[pl.BlockSpec((tm, w), lambda i: (i, 0)),
                  pl.BlockSpec((tm, w), lambda i: (i, 0)),
                  pl.BlockSpec((tm, w), lambda i: (i, P_BO // w)),
                  pl.BlockSpec((1, ML_DIM), lambda i: (0, 0))],
        out_specs=pl.BlockSpec((tm, w), lambda i: (i, 0)),
        out_shape=jax.ShapeDtypeStruct((m, w), BF16),
        compiler_params=_params(("arbitrary",)),
        name="mlstm_finish",
    )(hf, hb, p, out_gain.reshape(1, ML_DIM))


def _mixer_mlstm(p, conv_w, conv_b, gate_b, out_gain, b, t, lc):
    qk = _ml_prep(p, conv_w, conv_b, b, t, lc)
    nct = t // ML_L
    g = p[:, P_LAST:P_LAST + 4 * ML_HEADS].reshape(b, nct, ML_L, 2, 2, ML_HEADS) + gate_b
    gates = g.transpose(0, 3, 1, 5, 4, 2)
    hf, hb = _ml_scan(qk, p, gates, b, t, lc)
    return _ml_finish(hf, hb, p, out_gain)


def _outproj_kernel(a_ref, b_ref, c_ref, d_ref, w_ref, x_ref, mod_ref, o_ref, *, tm, tpb, lc, tn, d):
    i = pl.program_id(0)
    j = pl.program_id(1)
    kw = a_ref.shape[1]
    y = _dot(a_ref[...], w_ref[0:kw, :])
    y += _dot(b_ref[...], w_ref[kw:2 * kw, :])
    y += _dot(c_ref[...], w_ref[2 * kw:3 * kw, :])
    y += _dot(d_ref[...], w_ref[3 * kw:4 * kw, :])
    row = (i % tpb) * tm + lax.broadcasted_iota(jnp.int32, (tm, 1), 0)
    b = i // tpb
    col = pl.multiple_of(2 * d + j * tn, LANES)
    gate = jnp.where(row < lc, mod_ref[4:5, pl.ds(col, tn)], mod_ref[pl.ds(b, 1), pl.ds(col, tn)])
    o_ref[...] = x_ref[...] + gate * y


def _outproj(mixes, w, x, mod, t, lc):
    m, d = x.shape
    tm, tn = 768, min(1024, d)
    kw = mixes[0].shape[1]
    mix_spec = pl.BlockSpec((tm, kw), lambda i, j: (i, 0))
    return pl.pallas_call(
        functools.partial(_outproj_kernel, tm=tm, tpb=t // tm, lc=lc, tn=tn, d=d),
        grid=(m // tm, d // tn),
        in_specs=[mix_spec] * 4 + [pl.BlockSpec((4 * kw, tn), lambda i, j: (0, j)),
                                   pl.BlockSpec((tm, tn), lambda i, j: (i, j)),
                                   pl.BlockSpec(mod.shape, lambda i, j: (0, 0))],
        out_specs=pl.BlockSpec((tm, tn), lambda i, j: (i, j)),
        out_shape=jax.ShapeDtypeStruct((m, d), F32),
        compiler_params=_params(("arbitrary", "arbitrary")),
        name="outproj",
    )(*mixes, w, x, mod)


def _ffn_kernel(x_ref, mod_ref, w1_ref, w3_ref, w2_ref, o_ref, h_scr, acc_scr, *, tm, tpb, lc):
    i = pl.program_id(0)
    j = pl.program_id(1)

    @pl.when(j == 0)
    def _():
        h_scr[...] = _adaln_tile(x_ref[...], mod_ref, i, tm, tpb, lc, 3, 4).astype(BF16)
        acc_scr[...] = jnp.zeros_like(acc_scr)

    h = h_scr[...]
    act = _silu(_dot(h, w1_ref[...])) * _dot(h, w3_ref[...])
    acc_scr[...] += _dot(act.astype(BF16), w2_ref[...])

    @pl.when(j == pl.num_programs(1) - 1)
    def _():
        d = x_ref.shape[1]
        gate = _gate_tile(mod_ref, i, tm, tpb, lc, 5, 0, d, d)
        o_ref[...] = x_ref[...] + gate * acc_scr[...]


def _ffn_dense(x, mod, w1, w3, w2, t, lc):
    m, d = x.shape
    f = w1.shape[1]
    tm, tf = 768, 512
    w1, w3, w2 = w1.astype(BF16), w3.astype(BF16), w2.astype(BF16)
    return pl.pallas_call(
        functools.partial(_ffn_kernel, tm=tm, tpb=t // tm, lc=lc),
        grid=(m // tm, f // tf),
        in_specs=[pl.BlockSpec((tm, d), lambda i, j: (i, 0), pipeline_mode=pl.Buffered(1)),
                  pl.BlockSpec(mod.shape, lambda i, j: (0, 0)),
                  pl.BlockSpec((d, tf), lambda i, j: (0, j)),
                  pl.BlockSpec((d, tf), lambda i, j: (0, j)),
                  pl.BlockSpec((tf, d), lambda i, j: (j, 0))],
        out_specs=pl.BlockSpec((tm, d), lambda i, j: (i, 0), pipeline_mode=pl.Buffered(1)),
        out_shape=jax.ShapeDtypeStruct((m, d), F32),
        scratch_shapes=[pltpu.VMEM((tm, d), BF16), pltpu.VMEM((tm, d), F32)],
        compiler_params=_params(("arbitrary", "arbitrary")),
        name="ffn_dense",
    )(x, mod, w1, w3, w2)


def _router_kernel(x_ref, mod_ref, wr_ref, h_ref, r_ref, *, tm, tpb, lc):
    i = pl.program_id(0)
    h = _adaln_tile(x_ref[...], mod_ref, i, tm, tpb, lc, 3, 4)
    h_ref[...] = h
    lane = lax.broadcasted_iota(jnp.int32, (tm, LANES), 1).astype(F32)
    logits = jnp.where(lane < N_EXPERTS, _dot_f32(h, wr_ref[...]), -jnp.inf)
    v1 = jnp.max(logits, axis=-1, keepdims=True)
    i1 = jnp.min(jnp.where(logits == v1, lane, float(LANES)), axis=-1, keepdims=True)
    rest = jnp.where(lane == i1, -jnp.inf, logits)
    v2 = jnp.max(rest, axis=-1, keepdims=True)
    i2 = jnp.min(jnp.where(rest == v2, lane, float(LANES)), axis=-1, keepdims=True)
    e2 = jnp.exp(v2 - v1)
    g1 = 1.0 / (1.0 + e2)
    g2 = e2 / (1.0 + e2)
    r_ref[...] = jnp.where(lane == 0, i1, jnp.where(lane == 1, i2, jnp.where(lane == 2, g1, jnp.where(lane == 3, g2, 0.0))))


def _router(x, mod, w_router, t, lc):
    m, d = x.shape
    tm = 768
    wr = jnp.pad(w_router, ((0, 0), (0, LANES - w_router.shape[1])))
    return pl.pallas_call(
        functools.partial(_router_kernel, tm=tm, tpb=t // tm, lc=lc),
        grid=(m // tm,),
        in_specs=[pl.BlockSpec((tm, d), lambda i: (i, 0)),
                  pl.BlockSpec(mod.shape, lambda i: (0, 0)),
                  pl.BlockSpec((d, LANES), lambda i: (0, 0))],
        out_specs=[pl.BlockSpec((tm, d), lambda i: (i, 0)),
                   pl.BlockSpec((tm, LANES), lambda i: (i, 0))],
        out_shape=[jax.ShapeDtypeStruct((m, d), F32), jax.ShapeDtypeStruct((m, LANES), F32)],
        compiler_params=_params(("arbitrary",)),
        name="moe_router",
    )(x, mod, wr)


def _gather_kernel(src_ref, h_ref, o_ref, buf, sem, *, tg):
    g = pl.program_id(0)
    last = pl.num_programs(0) - 1
    slot = g % 2

    def row_copy(idx, r, s):
        return pltpu.make_async_copy(h_ref.at[pl.ds(idx, 1)], buf.at[s, pl.ds(r, 1)], sem.at[s])

    def group_valid(gg):
        return src_ref[gg * tg] >= 0

    def issue(gg, s):
        def body(r, carry):
            row_copy(jnp.maximum(src_ref[gg * tg + r], 0), r, s).start()
            return carry
        lax.fori_loop(0, tg, body, 0, unroll=8)

    @pl.when((g == 0) & group_valid(0))
    def _():
        issue(0, 0)

    nxt = jnp.minimum(g + 1, last)

    @pl.when((g < last) & group_valid(nxt))
    def _():
        issue(nxt, 1 - slot)

    @pl.when(group_valid(g))
    def _():
        def body(r, carry):
            row_copy(0, r, slot).wait()
            return carry
        lax.fori_loop(0, tg, body, 0, unroll=8)
        o_ref[...] = buf[slot].astype(BF16)

    @pl.when(jnp.logical_not(group_valid(g)))
    def _():
        o_ref[...] = jnp.zeros_like(o_ref)


def _gather_rows(src, h, n_rows):
    d = h.shape[1]
    tg = MOE_TS
    return pl.pallas_call(
        functools.partial(_gather_kernel, tg=tg),
        grid_spec=pltpu.PrefetchScalarGridSpec(
            num_scalar_prefetch=1,
            grid=(n_rows // tg,),
            in_specs=[pl.BlockSpec(memory_space=pl.ANY)],
            out_specs=pl.BlockSpec((tg, d), lambda i, src: (i, 0)),
            scratch_shapes=[pltpu.VMEM((2, tg, d), h.dtype), pltpu.SemaphoreType.DMA((2,))]),
        out_shape=jax.ShapeDtypeStruct((n_rows, d), BF16),
        compiler_params=_params(("arbitrary",)),
        name="moe_gather",
    )(src, h)


def _rows_switch(n_rows, ts, tm, body):
    n_sub = (n_rows + ts - 1) // ts
    for c in range(tm // ts + 1):
        pl.when(n_sub == c)(functools.partial(body, c * ts))


def _moe_up_kernel(te_ref, tr_ref, nr_ref, hs_ref, w1_ref, w3_ref, a_ref, *, ts):
    tm, tf = a_ref.shape

    def body(n):
        if n > 0:
            h = hs_ref[0:n, :]
            act = _silu(_dot(h, w1_ref[...].astype(BF16))) * _dot(h, w3_ref[...].astype(BF16))
            a_ref[0:n, :] = act.astype(BF16)
        if n < tm:
            a_ref[n:tm, :] = jnp.zeros((tm - n, tf), BF16)

    _rows_switch(nr_ref[pl.program_id(0)], ts, tm, body)


def _moe_down_kernel(te_ref, tr_ref, nr_ref, a_ref, w2_ref, y_ref, *, ts):
    tm, tn = y_ref.shape

    def body(n):
        if n > 0:
            y_ref[0:n, :] = _dot(a_ref[0:n, :], w2_ref[...].astype(BF16))
        if n < tm:
            y_ref[n:tm, :] = jnp.zeros((tm - n, tn), F32)

    _rows_switch(nr_ref[pl.program_id(0)], ts, tm, body)


def _moe_ffn(tile_expert, tile_row, tile_rows, hs, w1, w3, w2, layer):
    r, d = hs.shape
    f = w1.shape[3]
    tm, tf, tn = MOE_TM, 512, 256
    nf, nn = f // tf, d // tn
    prefetch = (tile_expert, tile_row, tile_rows)
    frozen = lambda last: (lambda i, j, nr: jnp.where(nr[i] > 0, j, last))
    jf, jn = frozen(nf - 1), frozen(nn - 1)
    act = pl.pallas_call(
        functools.partial(_moe_up_kernel, ts=MOE_TS),
        grid_spec=pltpu.PrefetchScalarGridSpec(
            num_scalar_prefetch=3,
            grid=(r // tm, nf),
            in_specs=[pl.BlockSpec((tm, d), lambda i, j, te, tr, nr: (tr[i], 0)),
                      pl.BlockSpec((None, None, d, tf), lambda i, j, te, tr, nr: (layer, te[i], 0, jf(i, j, nr))),
                      pl.BlockSpec((None, None, d, tf), lambda i, j, te, tr, nr: (layer, te[i], 0, jf(i, j, nr)))],
            out_specs=pl.BlockSpec((tm, tf), lambda i, j, te, tr, nr: (i, j))),
        out_shape=jax.ShapeDtypeStruct((r, f), BF16),
        compiler_params=_params(("arbitrary", "arbitrary")),
        name="moe_up",
    )(*prefetch, hs, w1, w3)
    return pl.pallas_call(
        functools.partial(_moe_down_kernel, ts=MOE_TS),
        grid_spec=pltpu.PrefetchScalarGridSpec(
            num_scalar_prefetch=3,
            grid=(r // tm, nn),
            in_specs=[pl.BlockSpec((tm, f), lambda i, j, te, tr, nr: (tr[i], 0)),
                      pl.BlockSpec((None, None, f, tn), lambda i, j, te, tr, nr: (layer, te[i], 0, jn(i, j, nr)))],
            out_specs=pl.BlockSpec((tm, tn), lambda i, j, te, tr, nr: (i, j))),
        out_shape=jax.ShapeDtypeStruct((r, d), F32),
        compiler_params=_params(("arbitrary", "arbitrary")),
        name="moe_down",
    )(*prefetch, act, w2)


def _combine_kernel(p1_ref, p2_ref, y_ref, x_ref, r_ref, mod_ref, o_ref, y1_scr, y2_scr, sem, *, tc, tpb, lc):
    i = pl.program_id(0)
    last = pl.num_programs(0) - 1
    slot = i % 2

    def copies(idx1, idx2, r, s):
        return (pltpu.make_async_copy(y_ref.at[pl.ds(idx1, 1)], y1_scr.at[s, pl.ds(r, 1)], sem.at[s]),
                pltpu.make_async_copy(y_ref.at[pl.ds(idx2, 1)], y2_scr.at[s, pl.ds(r, 1)], sem.at[s]))

    def issue(ii, s):
        def body(r, carry):
            c1, c2 = copies(p1_ref[ii * tc + r], p2_ref[ii * tc + r], r, s)
            c1.start()
            c2.start()
            return carry
        lax.fori_loop(0, tc, body, 0, unroll=8)

    @pl.when(i == 0)
    def _():
        issue(0, 0)

    @pl.when(i < last)
    def _():
        issue(jnp.minimum(i + 1, last), 1 - slot)

    def drain(r, carry):
        c1, c2 = copies(0, 0, r, slot)
        c1.wait()
        c2.wait()
        return carry

    lax.fori_loop(0, tc, drain, 0, unroll=8)
    d = x_ref.shape[1]
    gate = _gate_tile(mod_ref, i, tc, tpb, lc, 5, 0, d, d)
    route = r_ref[...]
    f = route[:, 2:3] * y1_scr[slot] + route[:, 3:4] * y2_scr[slot]
    o_ref[...] = x_ref[...] + gate * f


def _combine(pos1, pos2, y, x, route, mod, t, lc):
    m, d = x.shape
    tc = 256
    return pl.pallas_call(
        functools.partial(_combine_kernel, tc=tc, tpb=t // tc, lc=lc),
        grid_spec=pltpu.PrefetchScalarGridSpec(
            num_scalar_prefetch=2,
            grid=(m // tc,),
            in_specs=[pl.BlockSpec(memory_space=pl.ANY),
                      pl.BlockSpec((tc, d), lambda i, p1, p2: (i, 0)),
                      pl.BlockSpec((tc, LANES), lambda i, p1, p2: (i, 0)),
                      pl.BlockSpec(mod.shape, lambda i, p1, p2: (0, 0))],
            out_specs=pl.BlockSpec((tc, d), lambda i, p1, p2: (i, 0)),
            scratch_shapes=[pltpu.VMEM((2, tc, d), F32), pltpu.VMEM((2, tc, d), F32),
                            pltpu.SemaphoreType.DMA((2,))]),
        out_shape=jax.ShapeDtypeStruct((m, d), F32),
        compiler_params=_params(("arbitrary",)),
        name="moe_combine",
    )(pos1, pos2, y, x, route, mod)


def _ffn_moe(x, mod, w_router, w1, w3, w2, layer, t, lc):
    m, d = x.shape
    tm = MOE_TM
    h, route = _router(x, mod, w_router, t, lc)
    experts = route[:, 0:TOP_K].astype(jnp.int32).reshape(-1)
    onehot = (experts[:, None] == jnp.arange(N_EXPERTS)[None, :]).astype(jnp.int32)
    rank = jnp.sum((jnp.cumsum(onehot, axis=0) - onehot) * onehot, axis=1)
    count = jnp.sum(onehot, axis=0)
    tiles = (count + tm - 1) // tm
    tile_end = jnp.cumsum(tiles)
    tile_start = tile_end - tiles
    pos = (tile_start * tm)[experts] + rank
    n_tiles = (TOP_K * m + N_EXPERTS * (tm - 1)) // tm
    n_rows = n_tiles * tm
    src = jnp.full((n_rows,), -1, jnp.int32).at[pos].set(jnp.arange(TOP_K * m, dtype=jnp.int32) // TOP_K)
    tile_ids = jnp.arange(n_tiles)
    tile_row = jnp.minimum(tile_ids, tile_end[-1] - 1)
    tile_expert = jnp.sum((tile_row[:, None] >= tile_end[None, :]).astype(jnp.int32), axis=1)
    tile_rows = jnp.clip(count[tile_expert] - (tile_ids - tile_start[tile_expert]) * tm, 0, tm)
    tile_rows = jnp.where(tile_ids < tile_end[-1], tile_rows, 0)
    hs = _gather_rows(src, h, n_rows)
    y = _moe_ffn(tile_expert.astype(jnp.int32), tile_row.astype(jnp.int32), tile_rows.astype(jnp.int32),
                 hs, w1, w3, w2, layer)
    pos = pos.reshape(m, TOP_K)
    return _combine(pos[:, 0], pos[:, 1], y, x, route, mod, t, lc)


def _rope_tables(n_lat, lc, rot_dim, lane0, period):
    tok = jnp.arange(n_lat)
    axis_dim = rot_dim // 2
    inv_freq = ROPE_BASE ** (-jnp.arange(0, axis_dim, 2, dtype=F32) / axis_dim)
    ang_r = (tok // GRID_W).astype(F32)[:, None] * inv_freq
    ang_c = (tok % GRID_W).astype(F32)[:, None] * inv_freq
    ang = jnp.concatenate([ang_r, ang_r, ang_c, ang_c], axis=-1)
    cos, sin = jnp.cos(ang), jnp.sin(ang)
    seg = rot_dim // 4
    lane = np.arange(LANES)
    rel = (lane - lane0) % period
    active = (lane >= lane0) & (rel < rot_dim)
    even = ((rel // seg) % 2 == 0)
    idx = np.where(active, rel, 0)
    cos_t = jnp.where(active[None, :], cos[:, idx], 1.0)
    sin_t = jnp.where(active[None, :], sin[:, idx], 0.0)
    sin_next = jnp.where(even[None, :], -sin_t, 0.0)
    sin_prev = jnp.where(even[None, :], 0.0, sin_t)
    ctx = lambda a, v: jnp.concatenate([jnp.full((lc, LANES), v, F32), a], axis=0)
    return ctx(cos_t, 1.0), ctx(sin_next, 0.0), ctx(sin_prev, 0.0)


def _relayout_w_in(w):
    d = w.shape[0]
    seg = lambda lo, n: w[:, lo:lo + n]
    z = lambda n: jnp.zeros((d, n), w.dtype)
    a0, b0, g0, c0, d0 = 0, 1536, 3584, 3600, 4144
    parts = [seg(c0, 384), seg(c0 + 384, 128),
             seg(a0, 1536), seg(b0, 2048), seg(d0, 1536),
             seg(g0, 16), z(KR_LANE - 16), seg(c0 + 512, 32), z(LANES - KR_LANE - 32)]
    return jnp.concatenate(parts, axis=1).astype(BF16)


def _mla_weights(w_uq, w_ukv):
    hd = MLA_NOPE + MLA_ROPE
    wq = jnp.pad(w_uq.reshape(MLA_Q_RANK, MLA_HEADS, hd), ((0, 0), (0, 0), (0, LANES - hd)))
    wkv = w_ukv.reshape(MLA_KV_RANK, MLA_HEADS, MLA_NOPE + MLA_V)
    wk = jnp.pad(wkv[:, :, :MLA_NOPE], ((0, 0), (0, 0), (0, LANES - MLA_NOPE)))
    wv = wkv[:, :, MLA_NOPE:]
    return (wq.reshape(MLA_Q_RANK, -1).astype(BF16), wk.reshape(MLA_KV_RANK, -1).astype(BF16),
            wv.reshape(MLA_KV_RANK, -1).astype(BF16))


def kernel(x, c, ctx, c_ctx, w_ada, b_ada, w_in, w_out, da_q_gain, da_k_gain, da_lambda, da_out_gain, ml_conv_w, ml_conv_b, ml_gate_b, ml_out_gain, mla_cq_gain, mla_ckv_gain, mla_w_uq, mla_w_ukv, mla_q_gain, mla_k_gain, na_q_gain, na_k_gain, na_rpb, ffn_w1, ffn_w3, ffn_w2, moe_router, moe_w1, moe_w3, moe_w2):
    b, n_lat, d = x.shape
    lc = ctx.shape[1]
    t = lc + n_lat
    depth = w_in.shape[0]
    assert b <= 4 and lc == NA_QR * GRID_W and lc <= TQ_DIFF and n_lat % lc == 0 and t % 768 == 0

    xs = jnp.concatenate([ctx, x], axis=1).reshape(b * t, d)
    cond = jnp.zeros((8, d), F32).at[:b].set(c).at[4].set(c_ctx)
    mod = _mod_table(cond, w_ada, b_ada)
    rope_da = _rope_tables(n_lat, lc, DA_QK, 0, DA_QK)
    rope_mla = _rope_tables(n_lat, lc, MLA_ROPE, KR_LANE, LANES)

    for l in range(depth):
        lam_init = 0.8 - 0.6 * math.exp(-0.3 * l)
        p = _inproj(xs, mod[l], _relayout_w_in(w_in[l]), t, lc)
        mix_a = _mixer_diff(p, rope_da, da_q_gain[l], da_k_gain[l], da_lambda[l], da_out_gain[l], lam_init, b, t, lc)
        mix_b = _mixer_mlstm(p, ml_conv_w[l], ml_conv_b[l], ml_gate_b[l], ml_out_gain[l], b, t, lc)
        wq, wk, wv = _mla_weights(mla_w_uq[l], mla_w_ukv[l])
        qc, kc, vc = _mla_prep(p, rope_mla, mla_cq_gain[l], mla_ckv_gain[l], wq, wk, wv,
                               mla_q_gain[l], mla_k_gain[l], t)
        mix_c = _mla_attn(qc, kc, vc, b, t, lc)
        mix_d = _mixer_na(p, na_q_gain[l], na_k_gain[l], na_rpb[l], b, t, lc)
        xs = _outproj((mix_a, mix_b, mix_c, mix_d), w_out[l].astype(BF16), xs, mod[l], t, lc)
        if l % 2 == 0:
            xs = _ffn_dense(xs, mod[l], ffn_w1[l // 2], ffn_w3[l // 2], ffn_w2[l // 2], t, lc)
        else:
            xs = _ffn_moe(xs, mod[l], moe_router[l // 2], moe_w1, moe_w3, moe_w2, l // 2, t, lc)
    return xs.reshape(b, t, d)[:, lc:]
```

```python
import functools
import math

import numpy as np
import jax
import jax.numpy as jnp
from jax import lax
from jax.experimental import pallas as pl
from jax.experimental.pallas import tpu as pltpu

F32 = jnp.float32
BF16 = jnp.bfloat16

GRID_W = 64
DA_HEADS, DA_QK, DA_V = 4, 64, 128
ML_HEADS, ML_DIM, ML_CONV = 4, 128, 3
MLA_HEADS, MLA_NOPE, MLA_ROPE, MLA_V = 8, 64, 32, 64
MLA_Q_RANK, MLA_KV_RANK = 384, 128
NA_HEADS, NA_DIM, NA_WIN_R, NA_WIN_C = 8, 64, 8, 16
N_EXPERTS, TOP_K = 8, 2
ROPE_BASE = 10000.0
RMS_EPS = 1e-6
LOG2E = math.log2(math.e)

LANES = 128
VMEM_LIMIT = 56 * 1024 * 1024

P_CQ, P_CKV = 0, 384
P_AQ, P_AK, P_AV = 512, 1024, 1536
P_BQ, P_BK, P_BV, P_BO = 2048, 2560, 3072, 3584
P_DQ, P_DK, P_DV = 4096, 4608, 5120
P_LAST = 5632
P_WIDTH = 5760
KR_LANE = 64

TQ_DIFF, TQ_MLA = 384, 768
ML_L = 128
NA_QR, NA_KR = 4, 12
MOE_TM, MOE_TS = 1280, 256
STRIP = 16
STRIP_UNROLL = 8


def _params(sem):
    return pltpu.CompilerParams(dimension_semantics=sem, vmem_limit_bytes=VMEM_LIMIT)


def _silu(x):
    return x * (1.0 / (1.0 + jnp.exp(-x)))


def _sigmoid(x):
    return 1.0 / (1.0 + jnp.exp(-x))


def _rms(x, n=None):
    n = x.shape[-1] if n is None else n
    return x * lax.rsqrt(jnp.sum(x * x, axis=-1, keepdims=True) * (1.0 / n) + RMS_EPS)


def _rms_halves(x, gain):
    lo = lax.broadcasted_iota(jnp.int32, x.shape, 1) < 64
    x2 = x * x
    s_lo = jnp.sum(jnp.where(lo, x2, 0.0), axis=-1, keepdims=True)
    s_hi = jnp.sum(jnp.where(lo, 0.0, x2), axis=-1, keepdims=True)
    ms = jnp.where(lo, s_lo, s_hi) * (1.0 / 64)
    return x * lax.rsqrt(ms + RMS_EPS) * gain


def _rope(x, cos, sin_next, sin_prev, seg):
    return x * cos + pltpu.roll(x, LANES - seg, 1) * sin_next + pltpu.roll(x, seg, 1) * sin_prev


def _softmax2_strips(s_ref, e_ref, r_ref, r_lo, r_hi, nk, zero_tail=True):
    width = s_ref.shape[1]

    def body(i, carry):
        rows = pl.ds(pl.multiple_of(r_lo + i * STRIP, STRIP), STRIP)
        s = s_ref[rows, 0:nk]
        e = jnp.exp2(s - jnp.max(s, axis=-1, keepdims=True))
        r_ref[rows, :] = 1.0 / jnp.sum(e, axis=-1, keepdims=True)
        e_ref[rows, 0:nk] = e.astype(BF16)
        if zero_tail and nk < width:
            e_ref[rows, nk:width] = jnp.zeros((STRIP, width - nk), BF16)
        return carry

    n_strips = (r_hi - r_lo) // STRIP
    lax.fori_loop(0, n_strips, body, 0, unroll=math.gcd(n_strips, STRIP_UNROLL))


def _softmax2_blocks(s_ref, e_ref, r_ref, tq, lc, first):
    n, t = s_ref.shape
    if first:
        for base in range(0, n, tq):
            _softmax2_strips(s_ref, e_ref, r_ref, base, base + lc, lc)
            _softmax2_strips(s_ref, e_ref, r_ref, base + lc, base + tq, t)
    else:
        _softmax2_strips(s_ref, e_ref, r_ref, 0, n, t)


def _dot(a, b):
    return jnp.dot(a, b, preferred_element_type=F32)


def _dot_nt(a, b):
    return lax.dot_general(a, b, (((1,), (1,)), ((), ())), preferred_element_type=F32)


def _dot_tn(a, b):
    return lax.dot_general(a, b, (((0,), (0,)), ((), ())), preferred_element_type=F32)


def _split3(a):
    a1 = a.astype(BF16)
    r = a - a1.astype(F32)
    a2 = r.astype(BF16)
    a3 = (r - a2.astype(F32)).astype(BF16)
    return a1, a2, a3


def _dot_f32(a, b):
    a1, a2, a3 = _split3(a)
    b1, b2, b3 = _split3(b)
    return (_dot(a1, b1) + (_dot(a1, b2) + _dot(a2, b1))
            + (_dot(a1, b3) + _dot(a2, b2) + _dot(a3, b1)))


def _adaln_tile(x, mod_ref, i, tm, tpb, lc, c_shift, c_scale):
    d = x.shape[1]
    b = i // tpb
    row = (i % tpb) * tm + lax.broadcasted_iota(jnp.int32, (tm, 1), 0)
    is_ctx = row < lc
    shift = jnp.where(is_ctx, mod_ref[4:5, c_shift * d:(c_shift + 1) * d],
                      mod_ref[pl.ds(b, 1), c_shift * d:(c_shift + 1) * d])
    scale = jnp.where(is_ctx, mod_ref[4:5, c_scale * d:(c_scale + 1) * d],
                      mod_ref[pl.ds(b, 1), c_scale * d:(c_scale + 1) * d])
    return _rms(x) * (1.0 + scale) + shift


def _gate_tile(mod_ref, i, tm, tpb, lc, c_gate, col0, width, d):
    b = i // tpb
    row = (i % tpb) * tm + lax.broadcasted_iota(jnp.int32, (tm, 1), 0)
    lo = c_gate * d + col0
    return jnp.where(row < lc, mod_ref[4:5, lo:lo + width], mod_ref[pl.ds(b, 1), lo:lo + width])


def _mod_kernel(c_ref, w_ref, b_ref, o_ref):
    s = _silu(c_ref[...]).astype(BF16)
    o_ref[...] = _dot(s, w_ref[...].astype(BF16)) + b_ref[...]


def _mod_table(cond, w_ada, b_ada):
    depth, d, n = w_ada.shape
    tn = 1024
    return pl.pallas_call(
        _mod_kernel,
        grid=(depth, n // tn),
        in_specs=[pl.BlockSpec((8, d), lambda l, j: (0, 0)),
                  pl.BlockSpec((None, d, tn), lambda l, j: (l, 0, j)),
                  pl.BlockSpec((None, 1, tn), lambda l, j: (l, 0, j))],
        out_specs=pl.BlockSpec((None, 8, tn), lambda l, j: (l, 0, j)),
        out_shape=jax.ShapeDtypeStruct((depth, 8, n), F32),
        compiler_params=_params(("arbitrary", "arbitrary")),
        name="mod_table",
    )(cond, w_ada, b_ada.reshape(depth, 1, n))


def _inproj_kernel(x_ref, mod_ref, w_ref, o_ref, h_scr, *, tm, tpb, lc):
    i = pl.program_id(0)

    @pl.when(pl.program_id(1) == 0)
    def _():
        h_scr[...] = _adaln_tile(x_ref[...], mod_ref, i, tm, tpb, lc, 0, 1).astype(BF16)

    o_ref[...] = _dot(h_scr[...], w_ref[...])


def _inproj(x, mod, w, t, lc):
    m, d = x.shape
    n = w.shape[1]
    tm, tn = 768, 1152
    return pl.pallas_call(
        functools.partial(_inproj_kernel, tm=tm, tpb=t // tm, lc=lc),
        grid=(m // tm, n // tn),
        in_specs=[pl.BlockSpec((tm, d), lambda i, j: (i, 0)),
                  pl.BlockSpec(mod.shape, lambda i, j: (0, 0)),
                  pl.BlockSpec((d, tn), lambda i, j: (0, j))],
        out_specs=pl.BlockSpec((tm, tn), lambda i, j: (i, j)),
        out_shape=jax.ShapeDtypeStruct((m, n), F32),
        scratch_shapes=[pltpu.VMEM((tm, d), BF16)],
        compiler_params=_params(("arbitrary", "arbitrary")),
        name="inproj",
    )(x, mod, w)


def _diff_attn_kernel(q_ref, k_ref, v_ref, cos_ref, sn_ref, sp_ref, qg_ref, kg_ref, og_ref, lam_ref,
                      o_ref, kn_scr, vb_scr, s_scr, e_scr, r_scr, *, tq, lc, lam_init, scale):
    qi = pl.program_id(2)

    @pl.when(qi == 0)
    def _():
        k = _rms_halves(k_ref[...], kg_ref[...])
        kn_scr[...] = _rope(k, cos_ref[...], sn_ref[...], sp_ref[...], 16).astype(BF16)
        vb_scr[...] = v_ref[...].astype(BF16)

    r0 = pl.multiple_of(qi * tq, tq)
    q = _rms_halves(q_ref[...], qg_ref[...])
    q = _rope(q, cos_ref[pl.ds(r0, tq), :], sn_ref[pl.ds(r0, tq), :], sp_ref[pl.ds(r0, tq), :], 16) * scale
    lo = lax.broadcasted_iota(jnp.int32, q.shape, 1) < 64
    qs = jnp.concatenate([jnp.where(lo, q, 0.0), jnp.where(lo, 0.0, q)], axis=0).astype(BF16)
    lp = lam_ref[...]
    lam = (jnp.exp(jnp.sum(lp[0:1] * lp[1:2], axis=-1, keepdims=True))
           - jnp.exp(jnp.sum(lp[2:3] * lp[3:4], axis=-1, keepdims=True)) + lam_init)

    s_scr[...] = _dot_nt(qs, kn_scr[...])

    @pl.when(qi == 0)
    def _():
        _softmax2_blocks(s_scr, e_scr, r_scr, tq, lc, True)

    @pl.when(qi > 0)
    def _():
        _softmax2_blocks(s_scr, e_scr, r_scr, tq, lc, False)

    o2 = _dot(e_scr[...], vb_scr[...]) * r_scr[...]
    o = o2[0:tq] - lam * o2[tq:2 * tq]
    o_ref[...] = (_rms(o) * og_ref[...] * (1.0 - lam_init)).astype(BF16)


def _mixer_diff(p, tabs, q_gain, k_gain, lam_params, out_gain, lam_init, b, t, lc):
    m = p.shape[0]
    tq = TQ_DIFF
    nq = t // tq
    cos, sn, sp = tabs
    tile2 = lambda g: jnp.tile(g, 2).reshape(1, LANES)
    full = lambda a: pl.BlockSpec(a.shape, lambda bb, h, qi: (0,) * a.ndim)
    args = (cos, sn, sp, tile2(q_gain), tile2(k_gain), out_gain.reshape(1, LANES), lam_params)
    return pl.pallas_call(
        functools.partial(_diff_attn_kernel, tq=tq, lc=lc, lam_init=lam_init, scale=DA_QK ** -0.5 * LOG2E),
        grid=(b, DA_HEADS, nq),
        in_specs=[pl.BlockSpec((tq, LANES), lambda bb, h, qi: (bb * nq + qi, P_AQ // LANES + h)),
                  pl.BlockSpec((t, LANES), lambda bb, h, qi: (bb, P_AK // LANES + h)),
                  pl.BlockSpec((t, LANES), lambda bb, h, qi: (bb, P_AV // LANES + h))]
                 + [full(a) for a in args],
        out_specs=pl.BlockSpec((tq, LANES), lambda bb, h, qi: (bb * nq + qi, h)),
        out_shape=jax.ShapeDtypeStruct((m, DA_HEADS * DA_V), BF16),
        scratch_shapes=[pltpu.VMEM((t, LANES), BF16), pltpu.VMEM((t, LANES), BF16),
                        pltpu.VMEM((2 * tq, t), F32), pltpu.VMEM((2 * tq, t), BF16), pltpu.VMEM((2 * tq, 1), F32)],
        compiler_params=_params(("arbitrary", "arbitrary", "arbitrary")),
        name="diff_attn",
    )(p, p, p, *args)


def _mla_prep_kernel(cq_ref, ckv_ref, last_ref, cos_ref, sn_ref, sp_ref, cqg_ref, ckvg_ref, wq_ref, wk_ref,
                     wv_ref, qg_ref, kg_ref, q_out, k_out, v_out, *, scale):
    hd = MLA_NOPE + MLA_ROPE
    cos, sn, sp = cos_ref[...], sn_ref[...], sp_ref[...]
    cqn = (_rms(cq_ref[...]) * cqg_ref[...]).astype(BF16)
    ckvn = (_rms(ckv_ref[...]) * ckvg_ref[...]).astype(BF16)
    q = _dot(cqn, wq_ref[...])
    kk = _dot(ckvn, wk_ref[...])
    v_out[...] = _dot(ckvn, wv_ref[...]).astype(BF16)
    last = last_ref[...]
    lane = lax.broadcasted_iota(jnp.int32, last.shape, 1)
    kr = jnp.where((lane >= KR_LANE) & (lane < KR_LANE + MLA_ROPE), last, 0.0)
    for h in range(MLA_HEADS):
        sl = slice(h * LANES, (h + 1) * LANES)
        qh = _rms(q[:, sl], hd) * qg_ref[...]
        q_out[:, sl] = (_rope(qh, cos, sn, sp, 8) * scale).astype(BF16)
        kh = _rms(kk[:, sl] + kr, hd) * kg_ref[...]
        k_out[:, sl] = _rope(kh, cos, sn, sp, 8).astype(BF16)


def _mla_prep(p, tabs, cq_gain, ckv_gain, wq, wk, wv, q_gain, k_gain, t):
    m = p.shape[0]
    tm = 768
    tpb = t // tm
    hw = MLA_HEADS * LANES
    cos, sn, sp = tabs
    tab = pl.BlockSpec((tm, LANES), lambda i: (i % tpb, 0))
    full = lambda a: pl.BlockSpec(a.shape, lambda i: (0,) * a.ndim)
    pad = lambda g: jnp.pad(g, (0, LANES - g.shape[0])).reshape(1, LANES)
    consts = (cq_gain.reshape(1, -1), ckv_gain.reshape(1, -1), wq, wk, wv, pad(q_gain), pad(k_gain))
    return pl.pallas_call(
        functools.partial(_mla_prep_kernel, scale=(MLA_NOPE + MLA_ROPE) ** -0.5 * LOG2E),
        grid=(m // tm,),
        in_specs=[pl.BlockSpec((tm, MLA_Q_RANK), lambda i: (i, P_CQ // MLA_Q_RANK)),
                  pl.BlockSpec((tm, LANES), lambda i: (i, P_CKV // LANES)),
                  pl.BlockSpec((tm, LANES), lambda i: (i, P_LAST // LANES)),
                  tab, tab, tab] + [full(a) for a in consts],
        out_specs=[pl.BlockSpec((tm, hw), lambda i: (i, 0)),
                   pl.BlockSpec((tm, hw), lambda i: (i, 0)),
                   pl.BlockSpec((tm, MLA_HEADS * MLA_V), lambda i: (i, 0))],
        out_shape=[jax.ShapeDtypeStruct((m, hw), BF16), jax.ShapeDtypeStruct((m, hw), BF16),
                   jax.ShapeDtypeStruct((m, MLA_HEADS * MLA_V), BF16)],
        compiler_params=_params(("arbitrary",)),
        name="mla_prep",
    )(p, p, p, cos, sn, sp, *consts)


def _mla_attn_kernel(q_ref, k_ref, v_ref, o_ref, s_scr, e_scr, r_scr, *, tq, lc):
    qi = pl.program_id(2)
    outs = []
    for h in range(2):
        sl = slice(h * LANES, (h + 1) * LANES)
        s_scr[...] = _dot_nt(q_ref[:, sl], k_ref[:, sl])

        @pl.when(qi == 0)
        def _():
            _softmax2_blocks(s_scr, e_scr, r_scr, tq, lc, True)

        @pl.when(qi > 0)
        def _():
            _softmax2_blocks(s_scr, e_scr, r_scr, tq, lc, False)

        outs.append(_dot(e_scr[...], v_ref[...]) * r_scr[...])
    lo = lax.broadcasted_iota(jnp.int32, outs[0].shape, 1) < MLA_V
    o_ref[...] = jnp.where(lo, outs[0], outs[1]).astype(BF16)


def _mla_attn(q, k, v, b, t, lc):
    m = q.shape[0]
    tq = TQ_MLA
    nq = t // tq
    return pl.pallas_call(
        functools.partial(_mla_attn_kernel, tq=tq, lc=lc),
        grid=(b, MLA_HEADS // 2, nq),
        in_specs=[pl.BlockSpec((tq, 2 * LANES), lambda bb, hp, qi: (bb * nq + qi, hp)),
                  pl.BlockSpec((t, 2 * LANES), lambda bb, hp, qi: (bb, hp)),
                  pl.BlockSpec((t, LANES), lambda bb, hp, qi: (bb, hp))],
        out_specs=pl.BlockSpec((tq, LANES), lambda bb, hp, qi: (bb * nq + qi, hp)),
        out_shape=jax.ShapeDtypeStruct((m, MLA_HEADS * MLA_V), BF16),
        scratch_shapes=[pltpu.VMEM((tq, t), F32), pltpu.VMEM((tq, t), BF16), pltpu.VMEM((tq, 1), F32)],
        compiler_params=_params(("arbitrary", "arbitrary", "arbitrary")),
        name="mla_attn",
    )(q, k, v)


def _na_kernel(q_ref, k_ref, v_ref, bias_ref, qg_ref, kg_ref, o_ref, kn_scr, vb_scr, s_scr, e_scr, r_scr,
               *, lc, rows, scale):
    s = pl.program_id(1)
    tq = q_ref.shape[0]
    n_ctx_blk = lc // tq
    band = NA_KR * GRID_W
    npair = NA_HEADS // 2

    @pl.when(s == 0)
    def _():
        for pp in range(npair):
            sl = slice(pp * LANES, (pp + 1) * LANES)
            kn_scr[:, sl] = _rms_halves(k_ref[:, sl], kg_ref[:, sl]).astype(BF16)
        vb_scr[...] = v_ref[...].astype(BF16)

    def heads(pv_fn):
        for pp in range(npair):
            sl = slice(pp * LANES, (pp + 1) * LANES)
            q = _rms_halves(q_ref[:, sl], qg_ref[:, sl]) * scale
            lo = lax.broadcasted_iota(jnp.int32, q.shape, 1) < NA_DIM
            qs = jnp.concatenate([jnp.where(lo, q, 0.0), jnp.where(lo, 0.0, q)], axis=0).astype(BF16)
            o2 = pv_fn(qs, sl, pp)
            o_ref[:, sl] = jnp.where(lo, o2[0:tq], o2[tq:2 * tq]).astype(BF16)

    @pl.when(s < n_ctx_blk)
    def _():
        def f(qs, sl, pp):
            s_scr[:, 0:lc] = _dot_nt(qs, kn_scr[0:lc, sl])
            _softmax2_strips(s_scr, e_scr, r_scr, 0, 2 * tq, lc, zero_tail=False)
            return _dot(e_scr[:, 0:lc], vb_scr[0:lc, sl]) * r_scr[...]
        heads(f)

    @pl.when(s >= n_ctx_blk)
    def _():
        rb = s - n_ctx_blk
        krow = jnp.clip(rb * NA_QR - NA_WIN_R // 2, 0, rows - NA_KR)
        k0 = pl.multiple_of(lc + krow * GRID_W, GRID_W)

        def f(qs, sl, pp):
            s_scr[:, 0:band] = _dot_nt(qs, kn_scr[pl.ds(k0, band), sl]) + bias_ref[pp]
            s_scr[:, band:band + lc] = _dot_nt(qs, kn_scr[0:lc, sl])
            _softmax2_strips(s_scr, e_scr, r_scr, 0, 2 * tq, band + lc)
            num = (_dot(e_scr[:, 0:band], vb_scr[pl.ds(k0, band), sl])
                   + _dot(e_scr[:, band:band + lc], vb_scr[0:lc, sl]))
            return num * r_scr[...]
        heads(f)


def _na_variant(rb, n_rb):
    return jnp.where(rb == 0, 0, jnp.where(rb == n_rb - 1, 2, 1))


def _na_bias_table(rpb, rows):
    n_rb = rows // NA_QR
    rbs = np.array([0, 1, n_rb - 1])
    q_rows = rbs[:, None] * NA_QR + np.arange(NA_QR)
    key_rows = np.clip(rbs * NA_QR - NA_WIN_R // 2, 0, rows - NA_KR)[:, None] + np.arange(NA_KR)
    cols = np.arange(GRID_W)
    r0 = np.clip(q_rows - NA_WIN_R // 2, 0, rows - NA_WIN_R)[:, :, None]
    c0 = np.clip(cols - NA_WIN_C // 2, 0, GRID_W - NA_WIN_C)[:, None]
    kr = key_rows[:, None, :]
    ok_r = (kr >= r0) & (kr < r0 + NA_WIN_R)
    ok_c = (cols[None, :] >= c0) & (cols[None, :] < c0 + NA_WIN_C)
    rel_r = np.clip(kr - q_rows[:, :, None] + NA_WIN_R - 1, 0, 2 * NA_WIN_R - 2)
    rel_c = np.clip(cols[None, :] - cols[:, None] + NA_WIN_C - 1, 0, 2 * NA_WIN_C - 2)
    oh_c = jnp.asarray(np.eye(2 * NA_WIN_C - 1, dtype=np.float32)[rel_c])
    rpb2 = rpb.reshape(NA_HEADS // 2, 2, 2 * NA_WIN_R - 1, 2 * NA_WIN_C - 1)
    tiles = jnp.einsum('phde,qce->phdqc', rpb2, oh_c, precision=lax.Precision.HIGHEST)
    tiles = jnp.where(jnp.asarray(ok_c), tiles * LOG2E, -jnp.inf)
    npair, nv, tq, band = NA_HEADS // 2, len(rbs), NA_QR * GRID_W, NA_KR * GRID_W
    return pl.pallas_call(
        _na_bias_kernel,
        grid_spec=pltpu.PrefetchScalarGridSpec(
            num_scalar_prefetch=2,
            grid=(npair, nv),
            in_specs=[pl.BlockSpec((None,) + tiles.shape[1:], lambda p, v, rel, ok: (p, 0, 0, 0, 0))],
            out_specs=pl.BlockSpec((None, None, 2 * tq, band), lambda p, v, rel, ok: (p, v, 0, 0))),
        out_shape=jax.ShapeDtypeStruct((npair, nv, 2 * tq, band), F32),
        compiler_params=_params(("arbitrary", "arbitrary")),
        name="na_bias",
    )(jnp.asarray(rel_r.reshape(-1), jnp.int32), jnp.asarray(ok_r.reshape(-1), jnp.int32), tiles)


def _na_bias_kernel(rel_ref, ok_ref, t_ref, o_ref):
    v = pl.program_id(1)
    for hh in range(2):
        for i in range(NA_QR):
            rows = slice((hh * NA_QR + i) * GRID_W, (hh * NA_QR + i + 1) * GRID_W)
            for k in range(0, NA_KR, 2):
                pair = []
                for kk in (k, k + 1):
                    idx = (v * NA_QR + i) * NA_KR + kk
                    pair.append(jnp.where(ok_ref[idx] > 0, t_ref[hh, rel_ref[idx]], -jnp.inf))
                o_ref[rows, k * GRID_W:(k + 2) * GRID_W] = jnp.concatenate(pair, axis=1)


def _mixer_na(p, q_gain, k_gain, rpb, b, t, lc):
    m = p.shape[0]
    rows = (t - lc) // GRID_W
    tq = NA_QR * GRID_W
    nblk = t // tq
    n_ctx_blk = lc // tq
    bias = _na_bias_table(rpb, rows)
    tile8 = lambda g: jnp.tile(g, NA_HEADS).reshape(1, NA_HEADS * NA_DIM)
    w = NA_HEADS * NA_DIM
    return pl.pallas_call(
        functools.partial(_na_kernel, lc=lc, rows=rows, scale=NA_DIM ** -0.5 * LOG2E),
        grid=(b, nblk),
        in_specs=[pl.BlockSpec((tq, w), lambda bb, s: (bb * nblk + s, P_DQ // w)),
                  pl.BlockSpec((t, w), lambda bb, s: (bb, P_DK // w)),
                  pl.BlockSpec((t, w), lambda bb, s: (bb, P_DV // w)),
                  pl.BlockSpec((NA_HEADS // 2, None, 2 * tq, NA_KR * GRID_W),
                               lambda bb, s: (0, _na_variant(jnp.maximum(s - n_ctx_blk, 0), rows // NA_QR), 0, 0)),
                  pl.BlockSpec((1, w), lambda bb, s: (0, 0)),
                  pl.BlockSpec((1, w), lambda bb, s: (0, 0))],
        out_specs=pl.BlockSpec((tq, w), lambda bb, s: (bb * nblk + s, 0)),
        out_shape=jax.ShapeDtypeStruct((m, w), BF16),
        scratch_shapes=[pltpu.VMEM((t, w), BF16), pltpu.VMEM((t, w), BF16),
                        pltpu.VMEM((2 * tq, NA_KR * GRID_W + lc), F32),
                        pltpu.VMEM((2 * tq, NA_KR * GRID_W + lc), BF16), pltpu.VMEM((2 * tq, 1), F32)],
        compiler_params=_params(("arbitrary", "arbitrary")),
        name="na_attn",
    )(p, p, p, bias, tile8(q_gain), tile8(k_gain))


def _ml_prep_kernel(x_ref, w_ref, b_ref, o_ref, *, lc):
    t = x_ref.shape[0]
    x = x_ref[...]
    row = lax.broadcasted_iota(jnp.int32, (t, 1), 0)
    prev = jnp.where((row == 0) | (row == lc), 0.0, pltpu.roll(x, 1, 0))
    nxt = jnp.where((row == lc - 1) | (row == t - 1), 0.0, pltpu.roll(x, t - 1, 0))
    y = _silu(prev * w_ref[0:1, :] + x * w_ref[1:2, :] + nxt * w_ref[2:3, :] + b_ref[...])
    o_ref[...] = y * jnp.where(pl.program_id(1) == 1, ML_DIM ** -0.5, 1.0)


def _ml_prep(p, conv_w, conv_b, b, t, lc):
    m = p.shape[0]
    w = ML_HEADS * ML_DIM
    return pl.pallas_call(
        functools.partial(_ml_prep_kernel, lc=lc),
        grid=(b, 2),
        in_specs=[pl.BlockSpec((t, w), lambda bb, c: (bb, P_BQ // w + c)),
                  pl.BlockSpec((ML_CONV, w), lambda bb, c: (0, c)),
                  pl.BlockSpec((1, w), lambda bb, c: (0, c))],
        out_specs=pl.BlockSpec((t, w), lambda bb, c: (bb, c)),
        out_shape=jax.ShapeDtypeStruct((m, 2 * w), F32),
        compiler_params=_params(("arbitrary", "arbitrary")),
        name="mlstm_prep",
    )(p, conv_w, conv_b.reshape(1, -1))


def _ml_scan_kernel(qf_ref, kf_ref, vf_ref, gf_ref, qb_ref, kb_ref, vb_ref, gb_ref, of_ref, ob_ref,
                    c_scr, n_scr, m_scr):
    @pl.when(pl.program_id(1) == 0)
    def _():
        c_scr[...] = jnp.zeros_like(c_scr)
        n_scr[...] = jnp.zeros_like(n_scr)
        m_scr[...] = jnp.zeros_like(m_scr)

    L = qf_ref.shape[0]
    tt = lax.broadcasted_iota(jnp.int32, (L, L), 0)
    ss = lax.broadcasted_iota(jnp.int32, (L, L), 1)
    eye = tt == ss
    dirs = ((qf_ref, kf_ref, vf_ref, gf_ref, of_ref, ss <= tt, tt <= ss),
            (qb_ref, kb_ref, vb_ref, gb_ref, ob_ref, ss >= tt, tt >= ss))
    for d, (q_ref, k_ref, v_ref, g_ref, o_ref, cm, cm_t) in enumerate(dirs):
        for h in range(ML_HEADS):
            sl = slice(h * ML_DIM, (h + 1) * ML_DIM)
            _ml_chain(q_ref[:, sl], k_ref[:, sl], v_ref[:, sl], g_ref[h], o_ref.at[:, sl],
                      c_scr.at[d, h], n_scr.at[d, h], m_scr.at[d, h], eye, cm, cm_t)


def _ml_chain(q, k, v, g, o_ref, c_scr, n_scr, m_scr, eye, cm, cm_t):
    i_row = g[0:1, :]
    f_row = g[1:2, :]
    lf_row = jnp.minimum(f_row, 0.0) - jnp.log(1.0 + jnp.exp(-jnp.abs(f_row)))
    to_col = lambda r: jnp.sum(jnp.where(eye, r, 0.0), axis=-1, keepdims=True)
    lf_col = to_col(lf_row)
    i_col = to_col(i_row)
    b_col = jnp.sum(jnp.where(cm, lf_row, 0.0), axis=-1, keepdims=True)
    b_row = jnp.sum(jnp.where(cm_t, lf_col, 0.0), axis=0, keepdims=True)
    b_end = jnp.sum(lf_row, axis=-1, keepdims=True)
    m_old = m_scr[...]
    c_old = c_scr[...]
    n_old = n_scr[...]

    log_w = jnp.where(cm, b_col - b_row + i_row, -jnp.inf)
    inter = b_col + m_old
    m_t = jnp.maximum(inter, jnp.max(log_w, axis=-1, keepdims=True))
    qb = q.astype(BF16)
    kb = k.astype(BF16)
    vb = v.astype(BF16)
    s = _dot_nt(qb, kb) * jnp.exp(log_w - m_t)
    a = jnp.exp(inter - m_t)
    num = a * _dot(qb, c_old.astype(BF16)) + _dot(s.astype(BF16), vb)
    den = a * jnp.sum(q * n_old, axis=-1, keepdims=True) + jnp.sum(s, axis=-1, keepdims=True)
    o_ref[...] = num / jnp.maximum(jnp.abs(den), jnp.exp(-m_t))

    w_end = b_end - b_col + i_col
    m_new = jnp.maximum(b_end + m_old, jnp.max(w_end, axis=0, keepdims=True))
    decay = jnp.exp(b_end + m_old - m_new)
    we = jnp.exp(w_end - m_new)
    c_scr[...] = decay * c_old + _dot_tn(kb, (we * v).astype(BF16))
    n_scr[...] = decay * n_old + jnp.sum(we * k, axis=0, keepdims=True)
    m_scr[...] = m_new


def _ml_scan(qk, p, gates, b, t, lc):
    m = p.shape[0]
    L = ML_L
    nct, ncc = t // L, lc // L
    w = ML_HEADS * ML_DIM

    fwd = lambda j: j
    bwd = lambda j: jnp.where(j < ncc, ncc - 1 - j, nct - 1 - (j - ncc))

    def specs(chunk, d):
        blk = lambda col: pl.BlockSpec((L, w), lambda bb, j: (bb * nct + chunk(j), col))
        return [blk(0), blk(1), blk(P_BV // w),
                pl.BlockSpec((None, None, None, ML_HEADS, 2, L), lambda bb, j: (bb, d, chunk(j), 0, 0, 0))]

    out = lambda chunk: pl.BlockSpec((L, w), lambda bb, j: (bb * nct + chunk(j), 0))
    nh = ML_HEADS
    return pl.pallas_call(
        _ml_scan_kernel,
        grid=(b, nct),
        in_specs=specs(fwd, 0) + specs(bwd, 1),
        out_specs=[out(fwd), out(bwd)],
        out_shape=[jax.ShapeDtypeStruct((m, w), F32)] * 2,
        scratch_shapes=[pltpu.VMEM((2, nh, ML_DIM, ML_DIM), F32), pltpu.VMEM((2, nh, 1, ML_DIM), F32),
                        pltpu.VMEM((2, nh, 1, 1), F32)],
        compiler_params=_params(("arbitrary", "arbitrary")),
        name="mlstm_scan",
    )(qk, qk, p, gates, qk, qk, p, gates)


def _ml_finish_kernel(hf_ref, hb_ref, o_ref, g_ref, out_ref):
    hsum = hf_ref[...] + hb_ref[...]
    og = _sigmoid(o_ref[...])
    for h in range(ML_HEADS):
        sl = slice(h * ML_DIM, (h + 1) * ML_DIM)
        out_ref[:, sl] = (_rms(hsum[:, sl]) * g_ref[...] * og[:, sl]).astype(BF16)


def _ml_finish(hf, hb, p, out_gain):
    m = p.shape[0]
    w = ML_HEADS * ML_DIM
    tm = 768
    return pl.pallas_call(
        _ml_finish_kernel,
        grid=(m // tm,),
        in_specs=[pl.BlockSpec((tm, w), lambda i: (i, 0)),
                  pl.BlockSpec((tm, w), lambda i: (i, 0)),
                  pl.BlockSpec((tm, w), lambda i: (i, P_BO // w)),
                  pl.BlockSpec((1, ML_DIM), lambda i: (0, 0))],
        out_specs=pl.BlockSpec((tm, w), lambda i: (i, 0)),
        out_shape=jax.ShapeDtypeStruct((m, w), BF16),
        compiler_params=_params(("arbitrary",)),
        name="mlstm_finish",
    )(hf, hb, p, out_gain.reshape(1, ML_DIM))


def _mixer_mlstm(p, conv_w, conv_b, gate_b, out_gain, b, t, lc):
    qk = _ml_prep(p, conv_w, conv_b, b, t, lc)
    nct = t // ML_L
    g = p[:, P_LAST:P_LAST + 4 * ML_HEADS].reshape(b, nct, ML_L, 2, 2, ML_HEADS) + gate_b
    gates = g.transpose(0, 3, 1, 5, 4, 2)
    hf, hb = _ml_scan(qk, p, gates, b, t, lc)
    return _ml_finish(hf, hb, p, out_gain)


def _outproj_kernel(a_ref, b_ref, c_ref, d_ref, w_ref, x_ref, mod_ref, o_ref, *, tm, tpb, lc, tn, d):
    i = pl.program_id(0)
    j = pl.program_id(1)
    kw = a_ref.shape[1]
    y = _dot(a_ref[...], w_ref[0:kw, :])
    y += _dot(b_ref[...], w_ref[kw:2 * kw, :])
    y += _dot(c_ref[...], w_ref[2 * kw:3 * kw, :])
    y += _dot(d_ref[...], w_ref[3 * kw:4 * kw, :])
    row = (i % tpb) * tm + lax.broadcasted_iota(jnp.int32, (tm, 1), 0)
    b = i // tpb
    col = pl.multiple_of(2 * d + j * tn, LANES)
    gate = jnp.where(row < lc, mod_ref[4:5, pl.ds(col, tn)], mod_ref[pl.ds(b, 1), pl.ds(col, tn)])
    o_ref[...] = x_ref[...] + gate * y


def _outproj(mixes, w, x, mod, t, lc):
    m, d = x.shape
    tm, tn = 768, min(1024, d)
    kw = mixes[0].shape[1]
    mix_spec = pl.BlockSpec((tm, kw), lambda i, j: (i, 0))
    return pl.pallas_call(
        functools.partial(_outproj_kernel, tm=tm, tpb=t // tm, lc=lc, tn=tn, d=d),
        grid=(m // tm, d // tn),
        in_specs=[mix_spec] * 4 + [pl.BlockSpec((4 * kw, tn), lambda i, j: (0, j)),
                                   pl.BlockSpec((tm, tn), lambda i, j: (i, j)),
                                   pl.BlockSpec(mod.shape, lambda i, j: (0, 0))],
        out_specs=pl.BlockSpec((tm, tn), lambda i, j: (i, j)),
        out_shape=jax.ShapeDtypeStruct((m, d), F32),
        compiler_params=_params(("arbitrary", "arbitrary")),
        name="outproj",
    )(*mixes, w, x, mod)


def _ffn_kernel(x_ref, mod_ref, w1_ref, w3_ref, w2_ref, o_ref, h_scr, acc_scr, *, tm, tpb, lc):
    i = pl.program_id(0)
    j = pl.program_id(1)

    @pl.when(j == 0)
    def _():
        h_scr[...] = _adaln_tile(x_ref[...], mod_ref, i, tm, tpb, lc, 3, 4).astype(BF16)
        acc_scr[...] = jnp.zeros_like(acc_scr)

    h = h_scr[...]
    act = _silu(_dot(h, w1_ref[...])) * _dot(h, w3_ref[...])
    acc_scr[...] += _dot(act.astype(BF16), w2_ref[...])

    @pl.when(j == pl.num_programs(1) - 1)
    def _():
        d = x_ref.shape[1]
        gate = _gate_tile(mod_ref, i, tm, tpb, lc, 5, 0, d, d)
        o_ref[...] = x_ref[...] + gate * acc_scr[...]


def _ffn_dense(x, mod, w1, w3, w2, t, lc):
    m, d = x.shape
    f = w1.shape[1]
    tm, tf = 768, 512
    w1, w3, w2 = w1.astype(BF16), w3.astype(BF16), w2.astype(BF16)
    return pl.pallas_call(
        functools.partial(_ffn_kernel, tm=tm, tpb=t // tm, lc=lc),
        grid=(m // tm, f // tf),
        in_specs=[pl.BlockSpec((tm, d), lambda i, j: (i, 0), pipeline_mode=pl.Buffered(1)),
                  pl.BlockSpec(mod.shape, lambda i, j: (0, 0)),
                  pl.BlockSpec((d, tf), lambda i, j: (0, j)),
                  pl.BlockSpec((d, tf), lambda i, j: (0, j)),
                  pl.BlockSpec((tf, d), lambda i, j: (j, 0))],
        out_specs=pl.BlockSpec((tm, d), lambda i, j: (i, 0), pipeline_mode=pl.Buffered(1)),
        out_shape=jax.ShapeDtypeStruct((m, d), F32),
        scratch_shapes=[pltpu.VMEM((tm, d), BF16), pltpu.VMEM((tm, d), F32)],
        compiler_params=_params(("arbitrary", "arbitrary")),
        name="ffn_dense",
    )(x, mod, w1, w3, w2)


def _router_kernel(x_ref, mod_ref, wr_ref, h_ref, r_ref, *, tm, tpb, lc):
    i = pl.program_id(0)
    h = _adaln_tile(x_ref[...], mod_ref, i, tm, tpb, lc, 3, 4)
    h_ref[...] = h
    lane = lax.broadcasted_iota(jnp.int32, (tm, LANES), 1).astype(F32)
    logits = jnp.where(lane < N_EXPERTS, _dot_f32(h, wr_ref[...]), -jnp.inf)
    v1 = jnp.max(logits, axis=-1, keepdims=True)
    i1 = jnp.min(jnp.where(logits == v1, lane, float(LANES)), axis=-1, keepdims=True)
    rest = jnp.where(lane == i1, -jnp.inf, logits)
    v2 = jnp.max(rest, axis=-1, keepdims=True)
    i2 = jnp.min(jnp.where(rest == v2, lane, float(LANES)), axis=-1, keepdims=True)
    e2 = jnp.exp(v2 - v1)
    g1 = 1.0 / (1.0 + e2)
    g2 = e2 / (1.0 + e2)
    r_ref[...] = jnp.where(lane == 0, i1, jnp.where(lane == 1, i2, jnp.where(lane == 2, g1, jnp.where(lane == 3, g2, 0.0))))


def _router(x, mod, w_router, t, lc):
    m, d = x.shape
    tm = 768
    wr = jnp.pad(w_router, ((0, 0), (0, LANES - w_router.shape[1])))
    return pl.pallas_call(
        functools.partial(_router_kernel, tm=tm, tpb=t // tm, lc=lc),
        grid=(m // tm,),
        in_specs=[pl.BlockSpec((tm, d), lambda i: (i, 0)),
                  pl.BlockSpec(mod.shape, lambda i: (0, 0)),
                  pl.BlockSpec((d, LANES), lambda i: (0, 0))],
        out_specs=[pl.BlockSpec((tm, d), lambda i: (i, 0)),
                   pl.BlockSpec((tm, LANES), lambda i: (i, 0))],
        out_shape=[jax.ShapeDtypeStruct((m, d), F32), jax.ShapeDtypeStruct((m, LANES), F32)],
        compiler_params=_params(("arbitrary",)),
        name="moe_router",
    )(x, mod, wr)


def _gather_kernel(src_ref, h_ref, o_ref, buf, sem, *, tg):
    g = pl.program_id(0)
    last = pl.num_programs(0) - 1
    slot = g % 2

    def row_copy(idx, r, s):
        return pltpu.make_async_copy(h_ref.at[pl.ds(idx, 1)], buf.at[s, pl.ds(r, 1)], sem.at[s])

    def group_valid(gg):
        return src_ref[gg * tg] >= 0

    def issue(gg, s):
        def body(r, carry):
            row_copy(jnp.maximum(src_ref[gg * tg + r], 0), r, s).start()
            return carry
        lax.fori_loop(0, tg, body, 0, unroll=8)

    @pl.when((g == 0) & group_valid(0))
    def _():
        issue(0, 0)

    nxt = jnp.minimum(g + 1, last)

    @pl.when((g < last) & group_valid(nxt))
    def _():
        issue(nxt, 1 - slot)

    @pl.when(group_valid(g))
    def _():
        def body(r, carry):
            row_copy(0, r, slot).wait()
            return carry
        lax.fori_loop(0, tg, body, 0, unroll=8)
        o_ref[...] = buf[slot].astype(BF16)

    @pl.when(jnp.logical_not(group_valid(g)))
    def _():
        o_ref[...] = jnp.zeros_like(o_ref)


def _gather_rows(src, h, n_rows):
    d = h.shape[1]
    tg = MOE_TS
    return pl.pallas_call(
        functools.partial(_gather_kernel, tg=tg),
        grid_spec=pltpu.PrefetchScalarGridSpec(
            num_scalar_prefetch=1,
            grid=(n_rows // tg,),
            in_specs=[pl.BlockSpec(memory_space=pl.ANY)],
            out_specs=pl.BlockSpec((tg, d), lambda i, src: (i, 0)),
            scratch_shapes=[pltpu.VMEM((2, tg, d), h.dtype), pltpu.SemaphoreType.DMA((2,))]),
        out_shape=jax.ShapeDtypeStruct((n_rows, d), BF16),
        compiler_params=_params(("arbitrary",)),
        name="moe_gather",
    )(src, h)


def _rows_switch(n_rows, ts, tm, body):
    n_sub = (n_rows + ts - 1) // ts
    for c in range(tm // ts + 1):
        pl.when(n_sub == c)(functools.partial(body, c * ts))


def _moe_up_kernel(te_ref, tr_ref, nr_ref, hs_ref, w1_ref, w3_ref, a_ref, *, ts):
    tm, tf = a_ref.shape

    def body(n):
        if n > 0:
            h = hs_ref[0:n, :]
            act = _silu(_dot(h, w1_ref[...].astype(BF16))) * _dot(h, w3_ref[...].astype(BF16))
            a_ref[0:n, :] = act.astype(BF16)
        if n < tm:
            a_ref[n:tm, :] = jnp.zeros((tm - n, tf), BF16)

    _rows_switch(nr_ref[pl.program_id(0)], ts, tm, body)


def _moe_down_kernel(te_ref, tr_ref, nr_ref, a_ref, w2_ref, y_ref, *, ts):
    tm, tn = y_ref.shape

    def body(n):
        if n > 0:
            y_ref[0:n, :] = _dot(a_ref[0:n, :], w2_ref[...].astype(BF16))
        if n < tm:
            y_ref[n:tm, :] = jnp.zeros((tm - n, tn), F32)

    _rows_switch(nr_ref[pl.program_id(0)], ts, tm, body)


def _moe_ffn(tile_expert, tile_row, tile_rows, hs, w1, w3, w2, layer):
    r, d = hs.shape
    f = w1.shape[3]
    tm, tf, tn = MOE_TM, 512, 256
    nf, nn = f // tf, d // tn
    prefetch = (tile_expert, tile_row, tile_rows)
    frozen = lambda last: (lambda i, j, nr: jnp.where(nr[i] > 0, j, last))
    jf, jn = frozen(nf - 1), frozen(nn - 1)
    act = pl.pallas_call(
        functools.partial(_moe_up_kernel, ts=MOE_TS),
        grid_spec=pltpu.PrefetchScalarGridSpec(
            num_scalar_prefetch=3,
            grid=(r // tm, nf),
            in_specs=[pl.BlockSpec((tm, d), lambda i, j, te, tr, nr: (tr[i], 0)),
                      pl.BlockSpec((None, None, d, tf), lambda i, j, te, tr, nr: (layer, te[i], 0, jf(i, j, nr))),
                      pl.BlockSpec((None, None, d, tf), lambda i, j, te, tr, nr: (layer, te[i], 0, jf(i, j, nr)))],
            out_specs=pl.BlockSpec((tm, tf), lambda i, j, te, tr, nr: (i, j))),
        out_shape=jax.ShapeDtypeStruct((r, f), BF16),
        compiler_params=_params(("arbitrary", "arbitrary")),
        name="moe_up",
    )(*prefetch, hs, w1, w3)
    return pl.pallas_call(
        functools.partial(_moe_down_kernel, ts=MOE_TS),
        grid_spec=pltpu.PrefetchScalarGridSpec(
            num_scalar_prefetch=3,
            grid=(r // tm, nn),
            in_specs=[pl.BlockSpec((tm, f), lambda i, j, te, tr, nr: (tr[i], 0)),
                      pl.BlockSpec((None, None, f, tn), lambda i, j, te, tr, nr: (layer, te[i], 0, jn(i, j, nr)))],
            out_specs=pl.BlockSpec((tm, tn), lambda i, j, te, tr, nr: (i, j))),
        out_shape=jax.ShapeDtypeStruct((r, d), F32),
        compiler_params=_params(("arbitrary", "arbitrary")),
        name="moe_down",
    )(*prefetch, act, w2)


def _combine_kernel(p1_ref, p2_ref, y_ref, x_ref, r_ref, mod_ref, o_ref, y1_scr, y2_scr, sem, *, tc, tpb, lc):
    i = pl.program_id(0)
    last = pl.num_programs(0) - 1
    slot = i % 2

    def copies(idx1, idx2, r, s):
        return (pltpu.make_async_copy(y_ref.at[pl.ds(idx1, 1)], y1_scr.at[s, pl.ds(r, 1)], sem.at[s]),
                pltpu.make_async_copy(y_ref.at[pl.ds(idx2, 1)], y2_scr.at[s, pl.ds(r, 1)], sem.at[s]))

    def issue(ii, s):
        def body(r, carry):
            c1, c2 = copies(p1_ref[ii * tc + r], p2_ref[ii * tc + r], r, s)
            c1.start()
            c2.start()
            return carry
        lax.fori_loop(0, tc, body, 0, unroll=8)

    @pl.when(i == 0)
    def _():
        issue(0, 0)

    @pl.when(i < last)
    def _():
        issue(jnp.minimum(i + 1, last), 1 - slot)

    def drain(r, carry):
        c1, c2 = copies(0, 0, r, slot)
        c1.wait()
        c2.wait()
        return carry

    lax.fori_loop(0, tc, drain, 0, unroll=8)
    d = x_ref.shape[1]
    gate = _gate_tile(mod_ref, i, tc, tpb, lc, 5, 0, d, d)
    route = r_ref[...]
    f = route[:, 2:3] * y1_scr[slot] + route[:, 3:4] * y2_scr[slot]
    o_ref[...] = x_ref[...] + gate * f


def _combine(pos1, pos2, y, x, route, mod, t, lc):
    m, d = x.shape
    tc = 256
    return pl.pallas_call(
        functools.partial(_combine_kernel, tc=tc, tpb=t // tc, lc=lc),
        grid_spec=pltpu.PrefetchScalarGridSpec(
            num_scalar_prefetch=2,
            grid=(m // tc,),
            in_specs=[pl.BlockSpec(memory_space=pl.ANY),
                      pl.BlockSpec((tc, d), lambda i, p1, p2: (i, 0)),
                      pl.BlockSpec((tc, LANES), lambda i, p1, p2: (i, 0)),
                      pl.BlockSpec(mod.shape, lambda i, p1, p2: (0, 0))],
            out_specs=pl.BlockSpec((tc, d), lambda i, p1, p2: (i, 0)),
            scratch_shapes=[pltpu.VMEM((2, tc, d), F32), pltpu.VMEM((2, tc, d), F32),
                            pltpu.SemaphoreType.DMA((2,))]),
        out_shape=jax.ShapeDtypeStruct((m, d), F32),
        compiler_params=_params(("arbitrary",)),
        name="moe_combine",
    )(pos1, pos2, y, x, route, mod)


def _ffn_moe(x, mod, w_router, w1, w3, w2, layer, t, lc):
    m, d = x.shape
    tm = MOE_TM
    h, route = _router(x, mod, w_router, t, lc)
    experts = route[:, 0:TOP_K].astype(jnp.int32).reshape(-1)
    onehot = (experts[:, None] == jnp.arange(N_EXPERTS)[None, :]).astype(jnp.int32)
    rank = jnp.sum((jnp.cumsum(onehot, axis=0) - onehot) * onehot, axis=1)
    count = jnp.sum(onehot, axis=0)
    tiles = (count + tm - 1) // tm
    tile_end = jnp.cumsum(tiles)
    tile_start = tile_end - tiles
    pos = (tile_start * tm)[experts] + rank
    n_tiles = (TOP_K * m + N_EXPERTS * (tm - 1)) // tm
    n_rows = n_tiles * tm
    src = jnp.full((n_rows,), -1, jnp.int32).at[pos].set(jnp.arange(TOP_K * m, dtype=jnp.int32) // TOP_K)
    tile_ids = jnp.arange(n_tiles)
    tile_row = jnp.minimum(tile_ids, tile_end[-1] - 1)
    tile_expert = jnp.sum((tile_row[:, None] >= tile_end[None, :]).astype(jnp.int32), axis=1)
    tile_rows = jnp.clip(count[tile_expert] - (tile_ids - tile_start[tile_expert]) * tm, 0, tm)
    tile_rows = jnp.where(tile_ids < tile_end[-1], tile_rows, 0)
    hs = _gather_rows(src, h, n_rows)
    y = _moe_ffn(tile_expert.astype(jnp.int32), tile_row.astype(jnp.int32), tile_rows.astype(jnp.int32),
                 hs, w1, w3, w2, layer)
    pos = pos.reshape(m, TOP_K)
    return _combine(pos[:, 0], pos[:, 1], y, x, route, mod, t, lc)


def _rope_tables(n_lat, lc, rot_dim, lane0, period):
    tok = jnp.arange(n_lat)
    axis_dim = rot_dim // 2
    inv_freq = ROPE_BASE ** (-jnp.arange(0, axis_dim, 2, dtype=F32) / axis_dim)
    ang_r = (tok // GRID_W).astype(F32)[:, None] * inv_freq
    ang_c = (tok % GRID_W).astype(F32)[:, None] * inv_freq
    ang = jnp.concatenate([ang_r, ang_r, ang_c, ang_c], axis=-1)
    cos, sin = jnp.cos(ang), jnp.sin(ang)
    seg = rot_dim // 4
    lane = np.arange(LANES)
    rel = (lane - lane0) % period
    active = (lane >= lane0) & (rel < rot_dim)
    even = ((rel // seg) % 2 == 0)
    idx = np.where(active, rel, 0)
    cos_t = jnp.where(active[None, :], cos[:, idx], 1.0)
    sin_t = jnp.where(active[None, :], sin[:, idx], 0.0)
    sin_next = jnp.where(even[None, :], -sin_t, 0.0)
    sin_prev = jnp.where(even[None, :], 0.0, sin_t)
    ctx = lambda a, v: jnp.concatenate([jnp.full((lc, LANES), v, F32), a], axis=0)
    return ctx(cos_t, 1.0), ctx(sin_next, 0.0), ctx(sin_prev, 0.0)


def _relayout_w_in(w):
    d = w.shape[0]
    seg = lambda lo, n: w[:, lo:lo + n].astype(BF16)
    z = lambda n: jnp.zeros((d, n), BF16)
    a0, b0, g0, c0, d0 = 0, 1536, 3584, 3600, 4144
    parts = [seg(c0, 384), seg(c0 + 384, 128),
             seg(a0, 1536), seg(b0, 2048), seg(d0, 1536),
             seg(g0, 16), z(KR_LANE - 16), seg(c0 + 512, 32), z(LANES - KR_LANE - 32)]
    return jnp.concatenate(parts, axis=1)


def _mla_weights(w_uq, w_ukv):
    hd = MLA_NOPE + MLA_ROPE
    wq = jnp.pad(w_uq.reshape(MLA_Q_RANK, MLA_HEADS, hd), ((0, 0), (0, 0), (0, LANES - hd)))
    wkv = w_ukv.reshape(MLA_KV_RANK, MLA_HEADS, MLA_NOPE + MLA_V)
    wk = jnp.pad(wkv[:, :, :MLA_NOPE], ((0, 0), (0, 0), (0, LANES - MLA_NOPE)))
    wv = wkv[:, :, MLA_NOPE:]
    return (wq.reshape(MLA_Q_RANK, -1).astype(BF16), wk.reshape(MLA_KV_RANK, -1).astype(BF16),
            wv.reshape(MLA_KV_RANK, -1).astype(BF16))


def kernel(x, c, ctx, c_ctx, w_ada, b_ada, w_in, w_out, da_q_gain, da_k_gain, da_lambda, da_out_gain, ml_conv_w, ml_conv_b, ml_gate_b, ml_out_gain, mla_cq_gain, mla_ckv_gain, mla_w_uq, mla_w_ukv, mla_q_gain, mla_k_gain, na_q_gain, na_k_gain, na_rpb, ffn_w1, ffn_w3, ffn_w2, moe_router, moe_w1, moe_w3, moe_w2):
    b, n_lat, d = x.shape
    lc = ctx.shape[1]
    t = lc + n_lat
    depth = w_in.shape[0]
    assert b <= 4 and lc == NA_QR * GRID_W and lc <= TQ_DIFF and n_lat % lc == 0 and t % 768 == 0

    xs = jnp.concatenate([ctx, x], axis=1).reshape(b * t, d)
    cond = jnp.zeros((8, d), F32).at[:b].set(c).at[4].set(c_ctx)
    mod = _mod_table(cond, w_ada, b_ada)
    rope_da = _rope_tables(n_lat, lc, DA_QK, 0, DA_QK)
    rope_mla = _rope_tables(n_lat, lc, MLA_ROPE, KR_LANE, LANES)

    for l in range(depth):
        lam_init = 0.8 - 0.6 * math.exp(-0.3 * l)
        p = _inproj(xs, mod[l], _relayout_w_in(w_in[l]), t, lc)
        mix_a = _mixer_diff(p, rope_da, da_q_gain[l], da_k_gain[l], da_lambda[l], da_out_gain[l], lam_init, b, t, lc)
        mix_b = _mixer_mlstm(p, ml_conv_w[l], ml_conv_b[l], ml_gate_b[l], ml_out_gain[l], b, t, lc)
        wq, wk, wv = _mla_weights(mla_w_uq[l], mla_w_ukv[l])
        qc, kc, vc = _mla_prep(p, rope_mla, mla_cq_gain[l], mla_ckv_gain[l], wq, wk, wv,
                               mla_q_gain[l], mla_k_gain[l], t)
        mix_c = _mla_attn(qc, kc, vc, b, t, lc)
        mix_d = _mixer_na(p, na_q_gain[l], na_k_gain[l], na_rpb[l], b, t, lc)
        xs = _outproj((mix_a, mix_b, mix_c, mix_d), w_out[l].astype(BF16), xs, mod[l], t, lc)
        if l % 2 == 0:
            xs = _ffn_dense(xs, mod[l], ffn_w1[l // 2], ffn_w3[l // 2], ffn_w2[l // 2], t, lc)
        else:
            xs = _ffn_moe(xs, mod[l], moe_router[l // 2], moe_w1, moe_w3, moe_w2, l // 2, t, lc)
    return xs.reshape(b, t, d)[:, lc:]
```

```python
import functools
import math

import numpy as np
import jax
import jax.numpy as jnp
from jax import lax
from jax.experimental import pallas as pl
from jax.experimental.pallas import tpu as pltpu

F32 = jnp.float32
BF16 = jnp.bfloat16

GRID_W = 64
DA_HEADS, DA_QK, DA_V = 4, 64, 128
ML_HEADS, ML_DIM, ML_CONV = 4, 128, 3
MLA_HEADS, MLA_NOPE, MLA_ROPE, MLA_V = 8, 64, 32, 64
MLA_Q_RANK, MLA_KV_RANK = 384, 128
NA_HEADS, NA_DIM, NA_WIN_R, NA_WIN_C = 8, 64, 8, 16
N_EXPERTS, TOP_K = 8, 2
ROPE_BASE = 10000.0
RMS_EPS = 1e-6
LOG2E = math.log2(math.e)

LANES = 128
VMEM_LIMIT = 56 * 1024 * 1024

P_CQ, P_CKV = 0, 384
P_AQ, P_AK, P_AV = 512, 1024, 1536
P_BQ, P_BK, P_BV, P_BO = 2048, 2560, 3072, 3584
P_DQ, P_DK, P_DV = 4096, 4608, 5120
P_LAST = 5632
P_WIDTH = 5760
KR_LANE = 64

TQ_DIFF, TQ_MLA = 256, 256
ML_L = 256
NA_QR, NA_KR = 4, 12
MOE_TM, MOE_TS = 1280, 256


def _params(sem):
    return pltpu.CompilerParams(dimension_semantics=sem, vmem_limit_bytes=VMEM_LIMIT)


def _silu(x):
    return x * (1.0 / (1.0 + jnp.exp(-x)))


def _sigmoid(x):
    return 1.0 / (1.0 + jnp.exp(-x))


def _rms(x, n=None):
    n = x.shape[-1] if n is None else n
    return x * lax.rsqrt(jnp.sum(x * x, axis=-1, keepdims=True) * (1.0 / n) + RMS_EPS)


def _rms_halves(x, gain):
    lo = lax.broadcasted_iota(jnp.int32, x.shape, 1) < 64
    x2 = x * x
    s_lo = jnp.sum(jnp.where(lo, x2, 0.0), axis=-1, keepdims=True)
    s_hi = jnp.sum(jnp.where(lo, 0.0, x2), axis=-1, keepdims=True)
    ms = jnp.where(lo, s_lo, s_hi) * (1.0 / 64)
    return x * lax.rsqrt(ms + RMS_EPS) * gain


def _rope(x, cos, sin_next, sin_prev, seg):
    return x * cos + pltpu.roll(x, LANES - seg, 1) * sin_next + pltpu.roll(x, seg, 1) * sin_prev


def _softmax2_pv(s2, v):
    e = jnp.exp2(s2 - jnp.max(s2, axis=-1, keepdims=True))
    return _dot(e.astype(BF16), v) * (1.0 / jnp.sum(e, axis=-1, keepdims=True))


def _dot(a, b):
    return jnp.dot(a, b, preferred_element_type=F32)


def _dot_nt(a, b):
    return lax.dot_general(a, b, (((1,), (1,)), ((), ())), preferred_element_type=F32)


def _dot_tn(a, b):
    return lax.dot_general(a, b, (((0,), (0,)), ((), ())), preferred_element_type=F32)


def _split3(a):
    a1 = a.astype(BF16)
    r = a - a1.astype(F32)
    a2 = r.astype(BF16)
    a3 = (r - a2.astype(F32)).astype(BF16)
    return a1, a2, a3


def _dot_f32(a, b):
    a1, a2, a3 = _split3(a)
    b1, b2, b3 = _split3(b)
    return (_dot(a1, b1) + (_dot(a1, b2) + _dot(a2, b1))
            + (_dot(a1, b3) + _dot(a2, b2) + _dot(a3, b1)))


def _adaln_tile(x, mod_ref, i, tm, tpb, lc, c_shift, c_scale):
    d = x.shape[1]
    b = i // tpb
    row = (i % tpb) * tm + lax.broadcasted_iota(jnp.int32, (tm, 1), 0)
    is_ctx = row < lc
    shift = jnp.where(is_ctx, mod_ref[4:5, c_shift * d:(c_shift + 1) * d],
                      mod_ref[pl.ds(b, 1), c_shift * d:(c_shift + 1) * d])
    scale = jnp.where(is_ctx, mod_ref[4:5, c_scale * d:(c_scale + 1) * d],
                      mod_ref[pl.ds(b, 1), c_scale * d:(c_scale + 1) * d])
    return _rms(x) * (1.0 + scale) + shift


def _gate_tile(mod_ref, i, tm, tpb, lc, c_gate, col0, width, d):
    b = i // tpb
    row = (i % tpb) * tm + lax.broadcasted_iota(jnp.int32, (tm, 1), 0)
    lo = c_gate * d + col0
    return jnp.where(row < lc, mod_ref[4:5, lo:lo + width], mod_ref[pl.ds(b, 1), lo:lo + width])


def _mod_kernel(c_ref, w_ref, b_ref, o_ref):
    s = _silu(c_ref[...]).astype(BF16)
    o_ref[...] = _dot(s, w_ref[...].astype(BF16)) + b_ref[...]


def _mod_table(cond, w_ada, b_ada):
    depth, d, n = w_ada.shape
    tn = 1024
    return pl.pallas_call(
        _mod_kernel,
        grid=(depth, n // tn),
        in_specs=[pl.BlockSpec((8, d), lambda l, j: (0, 0)),
                  pl.BlockSpec((None, d, tn), lambda l, j: (l, 0, j)),
                  pl.BlockSpec((None, 1, tn), lambda l, j: (l, 0, j))],
        out_specs=pl.BlockSpec((None, 8, tn), lambda l, j: (l, 0, j)),
        out_shape=jax.ShapeDtypeStruct((depth, 8, n), F32),
        compiler_params=_params(("arbitrary", "arbitrary")),
        name="mod_table",
    )(cond, w_ada, b_ada.reshape(depth, 1, n))


def _inproj_kernel(x_ref, mod_ref, w_ref, o_ref, h_scr, *, tm, tpb, lc):
    i = pl.program_id(0)

    @pl.when(pl.program_id(1) == 0)
    def _():
        h_scr[...] = _adaln_tile(x_ref[...], mod_ref, i, tm, tpb, lc, 0, 1).astype(BF16)

    o_ref[...] = _dot(h_scr[...], w_ref[...])


def _inproj(x, mod, w, t, lc):
    m, d = x.shape
    n = w.shape[1]
    tm, tn = 768, 1152
    return pl.pallas_call(
        functools.partial(_inproj_kernel, tm=tm, tpb=t // tm, lc=lc),
        grid=(m // tm, n // tn),
        in_specs=[pl.BlockSpec((tm, d), lambda i, j: (i, 0)),
                  pl.BlockSpec(mod.shape, lambda i, j: (0, 0)),
                  pl.BlockSpec((d, tn), lambda i, j: (0, j))],
        out_specs=pl.BlockSpec((tm, tn), lambda i, j: (i, j)),
        out_shape=jax.ShapeDtypeStruct((m, n), F32),
        scratch_shapes=[pltpu.VMEM((tm, d), BF16)],
        compiler_params=_params(("arbitrary", "arbitrary")),
        name="inproj",
    )(x, mod, w)


def _diff_attn_kernel(q_ref, k_ref, v_ref, cos_ref, sn_ref, sp_ref, qg_ref, kg_ref, og_ref, lam_ref,
                      o_ref, kn_scr, vb_scr, *, tq, lc, lam_init, scale):
    qi = pl.program_id(2)
    t = k_ref.shape[0]

    @pl.when(qi == 0)
    def _():
        k = _rms_halves(k_ref[...], kg_ref[...])
        kn_scr[...] = _rope(k, cos_ref[...], sn_ref[...], sp_ref[...], 16).astype(BF16)
        vb_scr[...] = v_ref[...].astype(BF16)

    r0 = pl.multiple_of(qi * tq, tq)
    q = _rms_halves(q_ref[...], qg_ref[...])
    q = _rope(q, cos_ref[pl.ds(r0, tq), :], sn_ref[pl.ds(r0, tq), :], sp_ref[pl.ds(r0, tq), :], 16) * scale
    lo = lax.broadcasted_iota(jnp.int32, q.shape, 1) < 64
    q1 = jnp.where(lo, q, 0.0).astype(BF16)
    q2 = jnp.where(lo, 0.0, q).astype(BF16)
    lp = lam_ref[...]
    lam = (jnp.exp(jnp.sum(lp[0:1] * lp[1:2], axis=-1, keepdims=True))
           - jnp.exp(jnp.sum(lp[2:3] * lp[3:4], axis=-1, keepdims=True)) + lam_init)

    def attend(nk):
        kk = kn_scr[0:nk, :]
        vv = vb_scr[0:nk, :]
        o = _softmax2_pv(_dot_nt(q1, kk), vv) - lam * _softmax2_pv(_dot_nt(q2, kk), vv)
        o_ref[...] = (_rms(o) * og_ref[...] * (1.0 - lam_init)).astype(BF16)

    @pl.when(qi * tq < lc)
    def _():
        attend(lc)

    @pl.when(qi * tq >= lc)
    def _():
        attend(t)


def _mixer_diff(p, tabs, q_gain, k_gain, lam_params, out_gain, lam_init, b, t, lc):
    m = p.shape[0]
    tq = TQ_DIFF
    nq = t // tq
    cos, sn, sp = tabs
    tile2 = lambda g: jnp.tile(g, 2).reshape(1, LANES)
    full = lambda a: pl.BlockSpec(a.shape, lambda bb, h, qi: (0,) * a.ndim)
    args = (cos, sn, sp, tile2(q_gain), tile2(k_gain), out_gain.reshape(1, LANES), lam_params)
    return pl.pallas_call(
        functools.partial(_diff_attn_kernel, tq=tq, lc=lc, lam_init=lam_init, scale=DA_QK ** -0.5 * LOG2E),
        grid=(b, DA_HEADS, nq),
        in_specs=[pl.BlockSpec((tq, LANES), lambda bb, h, qi: (bb * nq + qi, P_AQ // LANES + h)),
                  pl.BlockSpec((t, LANES), lambda bb, h, qi: (bb, P_AK // LANES + h)),
                  pl.BlockSpec((t, LANES), lambda bb, h, qi: (bb, P_AV // LANES + h))]
                 + [full(a) for a in args],
        out_specs=pl.BlockSpec((tq, LANES), lambda bb, h, qi: (bb * nq + qi, h)),
        out_shape=jax.ShapeDtypeStruct((m, DA_HEADS * DA_V), BF16),
        scratch_shapes=[pltpu.VMEM((t, LANES), BF16), pltpu.VMEM((t, LANES), BF16)],
        compiler_params=_params(("arbitrary", "arbitrary", "arbitrary")),
        name="diff_attn",
    )(p, p, p, *args)


def _mla_prep_kernel(cq_ref, ckv_ref, last_ref, cos_ref, sn_ref, sp_ref, cqg_ref, ckvg_ref, wq_ref, wk_ref,
                     wv_ref, qg_ref, kg_ref, q_out, k_out, v_out, *, scale):
    hd = MLA_NOPE + MLA_ROPE
    cos, sn, sp = cos_ref[...], sn_ref[...], sp_ref[...]
    cqn = (_rms(cq_ref[...]) * cqg_ref[...]).astype(BF16)
    ckvn = (_rms(ckv_ref[...]) * ckvg_ref[...]).astype(BF16)
    q = _dot(cqn, wq_ref[...])
    kk = _dot(ckvn, wk_ref[...])
    v_out[...] = _dot(ckvn, wv_ref[...]).astype(BF16)
    last = last_ref[...]
    lane = lax.broadcasted_iota(jnp.int32, last.shape, 1)
    kr = jnp.where((lane >= KR_LANE) & (lane < KR_LANE + MLA_ROPE), last, 0.0)
    for h in range(MLA_HEADS):
        sl = slice(h * LANES, (h + 1) * LANES)
        qh = _rms(q[:, sl], hd) * qg_ref[...]
        q_out[:, sl] = (_rope(qh, cos, sn, sp, 8) * scale).astype(BF16)
        kh = _rms(kk[:, sl] + kr, hd) * kg_ref[...]
        k_out[:, sl] = _rope(kh, cos, sn, sp, 8).astype(BF16)


def _mla_prep(p, tabs, cq_gain, ckv_gain, wq, wk, wv, q_gain, k_gain, t):
    m = p.shape[0]
    tm = 768
    tpb = t // tm
    hw = MLA_HEADS * LANES
    cos, sn, sp = tabs
    tab = pl.BlockSpec((tm, LANES), lambda i: (i % tpb, 0))
    full = lambda a: pl.BlockSpec(a.shape, lambda i: (0,) * a.ndim)
    pad = lambda g: jnp.pad(g, (0, LANES - g.shape[0])).reshape(1, LANES)
    consts = (cq_gain.reshape(1, -1), ckv_gain.reshape(1, -1), wq, wk, wv, pad(q_gain), pad(k_gain))
    return pl.pallas_call(
        functools.partial(_mla_prep_kernel, scale=(MLA_NOPE + MLA_ROPE) ** -0.5 * LOG2E),
        grid=(m // tm,),
        in_specs=[pl.BlockSpec((tm, MLA_Q_RANK), lambda i: (i, P_CQ // MLA_Q_RANK)),
                  pl.BlockSpec((tm, LANES), lambda i: (i, P_CKV // LANES)),
                  pl.BlockSpec((tm, LANES), lambda i: (i, P_LAST // LANES)),
                  tab, tab, tab] + [full(a) for a in consts],
        out_specs=[pl.BlockSpec((tm, hw), lambda i: (i, 0)),
                   pl.BlockSpec((tm, hw), lambda i: (i, 0)),
                   pl.BlockSpec((tm, MLA_HEADS * MLA_V), lambda i: (i, 0))],
        out_shape=[jax.ShapeDtypeStruct((m, hw), BF16), jax.ShapeDtypeStruct((m, hw), BF16),
                   jax.ShapeDtypeStruct((m, MLA_HEADS * MLA_V), BF16)],
        compiler_params=_params(("arbitrary",)),
        name="mla_prep",
    )(p, p, p, cos, sn, sp, *consts)


def _mla_attn_kernel(q_ref, k_ref, v_ref, o_ref, *, tq, lc):
    qi = pl.program_id(2)
    t = k_ref.shape[0]

    def attend(nk):
        vv = v_ref[0:nk, :]
        outs = []
        for h in range(2):
            sl = slice(h * LANES, (h + 1) * LANES)
            outs.append(_softmax2_pv(_dot_nt(q_ref[:, sl], k_ref[0:nk, sl]), vv))
        lo = lax.broadcasted_iota(jnp.int32, outs[0].shape, 1) < MLA_V
        o_ref[...] = jnp.where(lo, outs[0], outs[1]).astype(BF16)

    @pl.when(qi * tq < lc)
    def _():
        attend(lc)

    @pl.when(qi * tq >= lc)
    def _():
        attend(t)


def _mla_attn(q, k, v, b, t, lc):
    m = q.shape[0]
    tq = TQ_MLA
    nq = t // tq
    return pl.pallas_call(
        functools.partial(_mla_attn_kernel, tq=tq, lc=lc),
        grid=(b, MLA_HEADS // 2, nq),
        in_specs=[pl.BlockSpec((tq, 2 * LANES), lambda bb, hp, qi: (bb * nq + qi, hp)),
                  pl.BlockSpec((t, 2 * LANES), lambda bb, hp, qi: (bb, hp)),
                  pl.BlockSpec((t, LANES), lambda bb, hp, qi: (bb, hp))],
        out_specs=pl.BlockSpec((tq, LANES), lambda bb, hp, qi: (bb * nq + qi, hp)),
        out_shape=jax.ShapeDtypeStruct((m, MLA_HEADS * MLA_V), BF16),
        compiler_params=_params(("arbitrary", "arbitrary", "arbitrary")),
        name="mla_attn",
    )(q, k, v)


def _na_kernel(q_ref, k_ref, v_ref, bias_ref, qg_ref, kg_ref, o_ref, kn_scr, vb_scr, *, lc, rows, scale):
    s = pl.program_id(1)
    tq = q_ref.shape[0]
    n_ctx_blk = lc // tq
    band = NA_KR * GRID_W
    npair = NA_HEADS // 2

    @pl.when(s == 0)
    def _():
        for pp in range(npair):
            sl = slice(pp * LANES, (pp + 1) * LANES)
            kn_scr[:, sl] = _rms_halves(k_ref[:, sl], kg_ref[:, sl]).astype(BF16)
        vb_scr[...] = v_ref[...].astype(BF16)

    def heads(pv_fn):
        for pp in range(npair):
            sl = slice(pp * LANES, (pp + 1) * LANES)
            q = _rms_halves(q_ref[:, sl], qg_ref[:, sl]) * scale
            lo = lax.broadcasted_iota(jnp.int32, q.shape, 1) < NA_DIM
            qs = jnp.concatenate([jnp.where(lo, q, 0.0), jnp.where(lo, 0.0, q)], axis=0).astype(BF16)
            o2 = pv_fn(qs, sl, pp)
            o_ref[:, sl] = jnp.where(lo, o2[0:tq], o2[tq:2 * tq]).astype(BF16)

    @pl.when(s < n_ctx_blk)
    def _():
        def f(qs, sl, pp):
            return _softmax2_pv(_dot_nt(qs, kn_scr[0:lc, sl]), vb_scr[0:lc, sl])
        heads(f)

    @pl.when(s >= n_ctx_blk)
    def _():
        rb = s - n_ctx_blk
        krow = jnp.clip(rb * NA_QR - NA_WIN_R // 2, 0, rows - NA_KR)
        k0 = pl.multiple_of(lc + krow * GRID_W, GRID_W)

        def f(qs, sl, pp):
            s_loc = _dot_nt(qs, kn_scr[pl.ds(k0, band), sl]) + bias_ref[pp]
            s_ctx = _dot_nt(qs, kn_scr[0:lc, sl])
            mx = jnp.maximum(jnp.max(s_loc, axis=-1, keepdims=True), jnp.max(s_ctx, axis=-1, keepdims=True))
            e_loc = jnp.exp2(s_loc - mx)
            e_ctx = jnp.exp2(s_ctx - mx)
            den = jnp.sum(e_loc, axis=-1, keepdims=True) + jnp.sum(e_ctx, axis=-1, keepdims=True)
            num = _dot(e_loc.astype(BF16), vb_scr[pl.ds(k0, band), sl]) + _dot(e_ctx.astype(BF16), vb_scr[0:lc, sl])
            return num * (1.0 / den)
        heads(f)


def _na_variant(rb, n_rb):
    return jnp.where(rb == 0, 0, jnp.where(rb == n_rb - 1, 2, 1))


def _na_bias_table(rpb, rows):
    n_rb = rows // NA_QR
    rbs = np.array([0, 1, n_rb - 1])
    q_rows = rbs[:, None] * NA_QR + np.arange(NA_QR)
    key_rows = np.clip(rbs * NA_QR - NA_WIN_R // 2, 0, rows - NA_KR)[:, None] + np.arange(NA_KR)
    cols = np.arange(GRID_W)
    r0 = np.clip(q_rows - NA_WIN_R // 2, 0, rows - NA_WIN_R)[:, :, None]
    c0 = np.clip(cols - NA_WIN_C // 2, 0, GRID_W - NA_WIN_C)[:, None]
    kr = key_rows[:, None, :]
    ok_r = (kr >= r0) & (kr < r0 + NA_WIN_R)
    ok_c = (cols[None, :] >= c0) & (cols[None, :] < c0 + NA_WIN_C)
    rel_r = np.clip(kr - q_rows[:, :, None] + NA_WIN_R - 1, 0, 2 * NA_WIN_R - 2)
    rel_c = np.clip(cols[None, :] - cols[:, None] + NA_WIN_C - 1, 0, 2 * NA_WIN_C - 2)
    oh_c = jnp.asarray(np.eye(2 * NA_WIN_C - 1, dtype=np.float32)[rel_c])
    rpb2 = rpb.reshape(NA_HEADS // 2, 2, 2 * NA_WIN_R - 1, 2 * NA_WIN_C - 1)
    tiles = jnp.einsum('phde,qce->phdqc', rpb2, oh_c, precision=lax.Precision.HIGHEST)
    tiles = jnp.where(jnp.asarray(ok_c), tiles * LOG2E, -jnp.inf)
    npair, nv, tq, band = NA_HEADS // 2, len(rbs), NA_QR * GRID_W, NA_KR * GRID_W
    return pl.pallas_call(
        _na_bias_kernel,
        grid_spec=pltpu.PrefetchScalarGridSpec(
            num_scalar_prefetch=2,
            grid=(npair, nv),
            in_specs=[pl.BlockSpec((None,) + tiles.shape[1:], lambda p, v, rel, ok: (p, 0, 0, 0, 0))],
            out_specs=pl.BlockSpec((None, None, 2 * tq, band), lambda p, v, rel, ok: (p, v, 0, 0))),
        out_shape=jax.ShapeDtypeStruct((npair, nv, 2 * tq, band), F32),
        compiler_params=_params(("arbitrary", "arbitrary")),
        name="na_bias",
    )(jnp.asarray(rel_r.reshape(-1), jnp.int32), jnp.asarray(ok_r.reshape(-1), jnp.int32), tiles)


def _na_bias_kernel(rel_ref, ok_ref, t_ref, o_ref):
    v = pl.program_id(1)
    for hh in range(2):
        for i in range(NA_QR):
            rows = slice((hh * NA_QR + i) * GRID_W, (hh * NA_QR + i + 1) * GRID_W)
            for k in range(0, NA_KR, 2):
                pair = []
                for kk in (k, k + 1):
                    idx = (v * NA_QR + i) * NA_KR + kk
                    pair.append(jnp.where(ok_ref[idx] > 0, t_ref[hh, rel_ref[idx]], -jnp.inf))
                o_ref[rows, k * GRID_W:(k + 2) * GRID_W] = jnp.concatenate(pair, axis=1)


def _mixer_na(p, q_gain, k_gain, rpb, b, t, lc):
    m = p.shape[0]
    rows = (t - lc) // GRID_W
    tq = NA_QR * GRID_W
    nblk = t // tq
    n_ctx_blk = lc // tq
    bias = _na_bias_table(rpb, rows)
    tile8 = lambda g: jnp.tile(g, NA_HEADS).reshape(1, NA_HEADS * NA_DIM)
    w = NA_HEADS * NA_DIM
    return pl.pallas_call(
        functools.partial(_na_kernel, lc=lc, rows=rows, scale=NA_DIM ** -0.5 * LOG2E),
        grid=(b, nblk),
        in_specs=[pl.BlockSpec((tq, w), lambda bb, s: (bb * nblk + s, P_DQ // w)),
                  pl.BlockSpec((t, w), lambda bb, s: (bb, P_DK // w)),
                  pl.BlockSpec((t, w), lambda bb, s: (bb, P_DV // w)),
                  pl.BlockSpec((NA_HEADS // 2, None, 2 * tq, NA_KR * GRID_W),
                               lambda bb, s: (0, _na_variant(jnp.maximum(s - n_ctx_blk, 0), rows // NA_QR), 0, 0)),
                  pl.BlockSpec((1, w), lambda bb, s: (0, 0)),
                  pl.BlockSpec((1, w), lambda bb, s: (0, 0))],
        out_specs=pl.BlockSpec((tq, w), lambda bb, s: (bb * nblk + s, 0)),
        out_shape=jax.ShapeDtypeStruct((m, w), BF16),
        scratch_shapes=[pltpu.VMEM((t, w), BF16), pltpu.VMEM((t, w), BF16)],
        compiler_params=_params(("arbitrary", "arbitrary")),
        name="na_attn",
    )(p, p, p, bias, tile8(q_gain), tile8(k_gain))


def _ml_prep_kernel(x_ref, w_ref, b_ref, o_ref, *, lc):
    t = x_ref.shape[0]
    x = x_ref[...]
    row = lax.broadcasted_iota(jnp.int32, (t, 1), 0)
    prev = jnp.where((row == 0) | (row == lc), 0.0, pltpu.roll(x, 1, 0))
    nxt = jnp.where((row == lc - 1) | (row == t - 1), 0.0, pltpu.roll(x, t - 1, 0))
    y = _silu(prev * w_ref[0:1, :] + x * w_ref[1:2, :] + nxt * w_ref[2:3, :] + b_ref[...])
    o_ref[...] = y * jnp.where(pl.program_id(1) == 1, ML_DIM ** -0.5, 1.0)


def _ml_prep(p, conv_w, conv_b, b, t, lc):
    m = p.shape[0]
    w = ML_HEADS * ML_DIM
    return pl.pallas_call(
        functools.partial(_ml_prep_kernel, lc=lc),
        grid=(b, 2),
        in_specs=[pl.BlockSpec((t, w), lambda bb, c: (bb, P_BQ // w + c)),
                  pl.BlockSpec((ML_CONV, w), lambda bb, c: (0, c)),
                  pl.BlockSpec((1, w), lambda bb, c: (0, c))],
        out_specs=pl.BlockSpec((t, w), lambda bb, c: (bb, c)),
        out_shape=jax.ShapeDtypeStruct((m, 2 * w), F32),
        compiler_params=_params(("arbitrary", "arbitrary")),
        name="mlstm_prep",
    )(p, conv_w, conv_b.reshape(1, -1))


def _ml_scan_kernel(qf_ref, kf_ref, vf_ref, gf_ref, qb_ref, kb_ref, vb_ref, gb_ref, of_ref, ob_ref,
                    c_scr, n_scr, m_scr):
    @pl.when(pl.program_id(1) == 0)
    def _():
        c_scr[...] = jnp.zeros_like(c_scr)
        n_scr[...] = jnp.zeros_like(n_scr)
        m_scr[...] = jnp.zeros_like(m_scr)

    L = qf_ref.shape[0]
    tt = lax.broadcasted_iota(jnp.int32, (L, L), 0)
    ss = lax.broadcasted_iota(jnp.int32, (L, L), 1)
    eye = tt == ss
    dirs = ((qf_ref, kf_ref, vf_ref, gf_ref, of_ref, ss <= tt, tt <= ss),
            (qb_ref, kb_ref, vb_ref, gb_ref, ob_ref, ss >= tt, tt >= ss))
    for d, (q_ref, k_ref, v_ref, g_ref, o_ref, cm, cm_t) in enumerate(dirs):
        for h in range(ML_HEADS):
            sl = slice(h * ML_DIM, (h + 1) * ML_DIM)
            _ml_chain(q_ref[:, sl], k_ref[:, sl], v_ref[:, sl], g_ref[h], o_ref.at[:, sl],
                      c_scr.at[d, h], n_scr.at[d, h], m_scr.at[d, h], eye, cm, cm_t)


def _ml_chain(q, k, v, g, o_ref, c_scr, n_scr, m_scr, eye, cm, cm_t):
    i_row = g[0:1, :]
    f_row = g[1:2, :]
    lf_row = jnp.minimum(f_row, 0.0) - jnp.log(1.0 + jnp.exp(-jnp.abs(f_row)))
    to_col = lambda r: jnp.sum(jnp.where(eye, r, 0.0), axis=-1, keepdims=True)
    lf_col = to_col(lf_row)
    i_col = to_col(i_row)
    b_col = jnp.sum(jnp.where(cm, lf_row, 0.0), axis=-1, keepdims=True)
    b_row = jnp.sum(jnp.where(cm_t, lf_col, 0.0), axis=0, keepdims=True)
    b_end = jnp.sum(lf_row, axis=-1, keepdims=True)
    m_old = m_scr[...]
    c_old = c_scr[...]
    n_old = n_scr[...]

    log_w = jnp.where(cm, b_col - b_row + i_row, -jnp.inf)
    inter = b_col + m_old
    m_t = jnp.maximum(inter, jnp.max(log_w, axis=-1, keepdims=True))
    qb = q.astype(BF16)
    kb = k.astype(BF16)
    vb = v.astype(BF16)
    s = _dot_nt(qb, kb) * jnp.exp(log_w - m_t)
    a = jnp.exp(inter - m_t)
    num = a * _dot(qb, c_old.astype(BF16)) + _dot(s.astype(BF16), vb)
    den = a * jnp.sum(q * n_old, axis=-1, keepdims=True) + jnp.sum(s, axis=-1, keepdims=True)
    o_ref[...] = num / jnp.maximum(jnp.abs(den), jnp.exp(-m_t))

    w_end = b_end - b_col + i_col
    m_new = jnp.maximum(b_end + m_old, jnp.max(w_end, axis=0, keepdims=True))
    decay = jnp.exp(b_end + m_old - m_new)
    we = jnp.exp(w_end - m_new)
    c_scr[...] = decay * c_old + _dot_tn(kb, (we * v).astype(BF16))
    n_scr[...] = decay * n_old + jnp.sum(we * k, axis=0, keepdims=True)
    m_scr[...] = m_new


def _ml_scan(qk, p, gates, b, t, lc):
    m = p.shape[0]
    L = ML_L
    nct, ncc = t // L, lc // L
    w = ML_HEADS * ML_DIM

    fwd = lambda j: j
    bwd = lambda j: jnp.where(j < ncc, ncc - 1 - j, nct - 1 - (j - ncc))

    def specs(chunk, d):
        blk = lambda col: pl.BlockSpec((L, w), lambda bb, j: (bb * nct + chunk(j), col))
        return [blk(0), blk(1), blk(P_BV // w),
                pl.BlockSpec((None, None, None, ML_HEADS, 2, L), lambda bb, j: (bb, d, chunk(j), 0, 0, 0))]

    out = lambda chunk: pl.BlockSpec((L, w), lambda bb, j: (bb * nct + chunk(j), 0))
    nh = ML_HEADS
    return pl.pallas_call(
        _ml_scan_kernel,
        grid=(b, nct),
        in_specs=specs(fwd, 0) + specs(bwd, 1),
        out_specs=[out(fwd), out(bwd)],
        out_shape=[jax.ShapeDtypeStruct((m, w), F32)] * 2,
        scratch_shapes=[pltpu.VMEM((2, nh, ML_DIM, ML_DIM), F32), pltpu.VMEM((2, nh, 1, ML_DIM), F32),
                        pltpu.VMEM((2, nh, 1, 1), F32)],
        compiler_params=_params(("arbitrary", "arbitrary")),
        name="mlstm_scan",
    )(qk, qk, p, gates, qk, qk, p, gates)


def _ml_finish_kernel(hf_ref, hb_ref, o_ref, g_ref, out_ref):
    hsum = hf_ref[...] + hb_ref[...]
    og = _sigmoid(o_ref[...])
    for h in range(ML_HEADS):
        sl = slice(h * ML_DIM, (h + 1) * ML_DIM)
        out_ref[:, sl] = (_rms(hsum[:, sl]) * g_ref[...] * og[:, sl]).astype(BF16)


def _ml_finish(hf, hb, p, out_gain):
    m = p.shape[0]
    w = ML_HEADS * ML_DIM
    tm = 768
    return pl.pallas_call(
        _ml_finish_kernel,
        grid=(m // tm,),
        in_specs=[pl.BlockSpec((tm, w), lambda i: (i, 0)),
                  pl.BlockSpec((tm, w), lambda i: (i, 0)),
                  pl.BlockSpec((tm, w), lambda i: (i, P_BO // w)),
                  pl.BlockSpec((1, ML_DIM), lambda i: (0, 0))],
        out_specs=pl.BlockSpec((tm, w), lambda i: (i, 0)),
        out_shape=jax.ShapeDtypeStruct((m, w), BF16),
        compiler_params=_params(("arbitrary",)),
        name="mlstm_finish",
    )(hf, hb, p, out_gain.reshape(1, ML_DIM))


def _mixer_mlstm(p, conv_w, conv_b, gate_b, out_gain, b, t, lc):
    qk = _ml_prep(p, conv_w, conv_b, b, t, lc)
    nct = t // ML_L
    g = p[:, P_LAST:P_LAST + 4 * ML_HEADS].reshape(b, nct, ML_L, 2, 2, ML_HEADS) + gate_b
    gates = g.transpose(0, 3, 1, 5, 4, 2)
    hf, hb = _ml_scan(qk, p, gates, b, t, lc)
    return _ml_finish(hf, hb, p, out_gain)


def _outproj_kernel(a_ref, b_ref, c_ref, d_ref, w_ref, x_ref, mod_ref, o_ref, *, tm, tpb, lc, tn, d):
    i = pl.program_id(0)
    j = pl.program_id(1)
    kw = a_ref.shape[1]
    y = _dot(a_ref[...], w_ref[0:kw, :])
    y += _dot(b_ref[...], w_ref[kw:2 * kw, :])
    y += _dot(c_ref[...], w_ref[2 * kw:3 * kw, :])
    y += _dot(d_ref[...], w_ref[3 * kw:4 * kw, :])
    row = (i % tpb) * tm + lax.broadcasted_iota(jnp.int32, (tm, 1), 0)
    b = i // tpb
    col = pl.multiple_of(2 * d + j * tn, LANES)
    gate = jnp.where(row < lc, mod_ref[4:5, pl.ds(col, tn)], mod_ref[pl.ds(b, 1), pl.ds(col, tn)])
    o_ref[...] = x_ref[...] + gate * y


def _outproj(mixes, w, x, mod, t, lc):
    m, d = x.shape
    tm, tn = 768, min(1024, d)
    kw = mixes[0].shape[1]
    mix_spec = pl.BlockSpec((tm, kw), lambda i, j: (i, 0))
    return pl.pallas_call(
        functools.partial(_outproj_kernel, tm=tm, tpb=t // tm, lc=lc, tn=tn, d=d),
        grid=(m // tm, d // tn),
        in_specs=[mix_spec] * 4 + [pl.BlockSpec((4 * kw, tn), lambda i, j: (0, j)),
                                   pl.BlockSpec((tm, tn), lambda i, j: (i, j)),
                                   pl.BlockSpec(mod.shape, lambda i, j: (0, 0))],
        out_specs=pl.BlockSpec((tm, tn), lambda i, j: (i, j)),
        out_shape=jax.ShapeDtypeStruct((m, d), F32),
        compiler_params=_params(("arbitrary", "arbitrary")),
        name="outproj",
    )(*mixes, w, x, mod)


def _ffn_kernel(x_ref, mod_ref, w1_ref, w3_ref, w2_ref, o_ref, h_scr, acc_scr, *, tm, tpb, lc):
    i = pl.program_id(0)
    j = pl.program_id(1)

    @pl.when(j == 0)
    def _():
        h_scr[...] = _adaln_tile(x_ref[...], mod_ref, i, tm, tpb, lc, 3, 4).astype(BF16)
        acc_scr[...] = jnp.zeros_like(acc_scr)

    h = h_scr[...]
    act = _silu(_dot(h, w1_ref[...])) * _dot(h, w3_ref[...])
    acc_scr[...] += _dot(act.astype(BF16), w2_ref[...])

    @pl.when(j == pl.num_programs(1) - 1)
    def _():
        d = x_ref.shape[1]
        gate = _gate_tile(mod_ref, i, tm, tpb, lc, 5, 0, d, d)
        o_ref[...] = x_ref[...] + gate * acc_scr[...]


def _ffn_dense(x, mod, w1, w3, w2, t, lc):
    m, d = x.shape
    f = w1.shape[1]
    tm, tf = 768, 512
    w1, w3, w2 = w1.astype(BF16), w3.astype(BF16), w2.astype(BF16)
    return pl.pallas_call(
        functools.partial(_ffn_kernel, tm=tm, tpb=t // tm, lc=lc),
        grid=(m // tm, f // tf),
        in_specs=[pl.BlockSpec((tm, d), lambda i, j: (i, 0), pipeline_mode=pl.Buffered(1)),
                  pl.BlockSpec(mod.shape, lambda i, j: (0, 0)),
                  pl.BlockSpec((d, tf), lambda i, j: (0, j)),
                  pl.BlockSpec((d, tf), lambda i, j: (0, j)),
                  pl.BlockSpec((tf, d), lambda i, j: (j, 0))],
        out_specs=pl.BlockSpec((tm, d), lambda i, j: (i, 0), pipeline_mode=pl.Buffered(1)),
        out_shape=jax.ShapeDtypeStruct((m, d), F32),
        scratch_shapes=[pltpu.VMEM((tm, d), BF16), pltpu.VMEM((tm, d), F32)],
        compiler_params=_params(("arbitrary", "arbitrary")),
        name="ffn_dense",
    )(x, mod, w1, w3, w2)


def _router_kernel(x_ref, mod_ref, wr_ref, h_ref, r_ref, *, tm, tpb, lc):
    i = pl.program_id(0)
    h = _adaln_tile(x_ref[...], mod_ref, i, tm, tpb, lc, 3, 4)
    h_ref[...] = h
    lane = lax.broadcasted_iota(jnp.int32, (tm, LANES), 1).astype(F32)
    logits = jnp.where(lane < N_EXPERTS, _dot_f32(h, wr_ref[...]), -jnp.inf)
    v1 = jnp.max(logits, axis=-1, keepdims=True)
    i1 = jnp.min(jnp.where(logits == v1, lane, float(LANES)), axis=-1, keepdims=True)
    rest = jnp.where(lane == i1, -jnp.inf, logits)
    v2 = jnp.max(rest, axis=-1, keepdims=True)
    i2 = jnp.min(jnp.where(rest == v2, lane, float(LANES)), axis=-1, keepdims=True)
    e2 = jnp.exp(v2 - v1)
    g1 = 1.0 / (1.0 + e2)
    g2 = e2 / (1.0 + e2)
    r_ref[...] = jnp.where(lane == 0, i1, jnp.where(lane == 1, i2, jnp.where(lane == 2, g1, jnp.where(lane == 3, g2, 0.0))))


def _router(x, mod, w_router, t, lc):
    m, d = x.shape
    tm = 768
    wr = jnp.pad(w_router, ((0, 0), (0, LANES - w_router.shape[1])))
    return pl.pallas_call(
        functools.partial(_router_kernel, tm=tm, tpb=t // tm, lc=lc),
        grid=(m // tm,),
        in_specs=[pl.BlockSpec((tm, d), lambda i: (i, 0)),
                  pl.BlockSpec(mod.shape, lambda i: (0, 0)),
                  pl.BlockSpec((d, LANES), lambda i: (0, 0))],
        out_specs=[pl.BlockSpec((tm, d), lambda i: (i, 0)),
                   pl.BlockSpec((tm, LANES), lambda i: (i, 0))],
        out_shape=[jax.ShapeDtypeStruct((m, d), F32), jax.ShapeDtypeStruct((m, LANES), F32)],
        compiler_params=_params(("arbitrary",)),
        name="moe_router",
    )(x, mod, wr)


def _gather_kernel(src_ref, h_ref, o_ref, buf, sem, *, tg):
    g = pl.program_id(0)
    last = pl.num_programs(0) - 1
    slot = g % 2

    def row_copy(idx, r, s):
        return pltpu.make_async_copy(h_ref.at[pl.ds(idx, 1)], buf.at[s, pl.ds(r, 1)], sem.at[s])

    def group_valid(gg):
        return src_ref[gg * tg] >= 0

    def issue(gg, s):
        def body(r, carry):
            row_copy(jnp.maximum(src_ref[gg * tg + r], 0), r, s).start()
            return carry
        lax.fori_loop(0, tg, body, 0, unroll=8)

    @pl.when((g == 0) & group_valid(0))
    def _():
        issue(0, 0)

    nxt = jnp.minimum(g + 1, last)

    @pl.when((g < last) & group_valid(nxt))
    def _():
        issue(nxt, 1 - slot)

    @pl.when(group_valid(g))
    def _():
        def body(r, carry):
            row_copy(0, r, slot).wait()
            return carry
        lax.fori_loop(0, tg, body, 0, unroll=8)
        o_ref[...] = buf[slot].astype(BF16)

    @pl.when(jnp.logical_not(group_valid(g)))
    def _():
        o_ref[...] = jnp.zeros_like(o_ref)


def _gather_rows(src, h, n_rows):
    d = h.shape[1]
    tg = MOE_TS
    return pl.pallas_call(
        functools.partial(_gather_kernel, tg=tg),
        grid_spec=pltpu.PrefetchScalarGridSpec(
            num_scalar_prefetch=1,
            grid=(n_rows // tg,),
            in_specs=[pl.BlockSpec(memory_space=pl.ANY)],
            out_specs=pl.BlockSpec((tg, d), lambda i, src: (i, 0)),
            scratch_shapes=[pltpu.VMEM((2, tg, d), h.dtype), pltpu.SemaphoreType.DMA((2,))]),
        out_shape=jax.ShapeDtypeStruct((n_rows, d), BF16),
        compiler_params=_params(("arbitrary",)),
        name="moe_gather",
    )(src, h)


def _rows_switch(n_rows, ts, tm, body):
    n_sub = (n_rows + ts - 1) // ts
    for c in range(tm // ts + 1):
        pl.when(n_sub == c)(functools.partial(body, c * ts))


def _moe_up_kernel(te_ref, tr_ref, nr_ref, hs_ref, w1_ref, w3_ref, a_ref, *, ts):
    tm, tf = a_ref.shape

    def body(n):
        if n > 0:
            h = hs_ref[0:n, :]
            act = _silu(_dot(h, w1_ref[...].astype(BF16))) * _dot(h, w3_ref[...].astype(BF16))
            a_ref[0:n, :] = act.astype(BF16)
        if n < tm:
            a_ref[n:tm, :] = jnp.zeros((tm - n, tf), BF16)

    _rows_switch(nr_ref[pl.program_id(0)], ts, tm, body)


def _moe_down_kernel(te_ref, tr_ref, nr_ref, a_ref, w2_ref, y_ref, *, ts):
    tm, tn = y_ref.shape

    def body(n):
        if n > 0:
            y_ref[0:n, :] = _dot(a_ref[0:n, :], w2_ref[...].astype(BF16))
        if n < tm:
            y_ref[n:tm, :] = jnp.zeros((tm - n, tn), F32)

    _rows_switch(nr_ref[pl.program_id(0)], ts, tm, body)


def _moe_ffn(tile_expert, tile_row, tile_rows, hs, w1, w3, w2, layer):
    r, d = hs.shape
    f = w1.shape[3]
    tm, tf, tn = MOE_TM, 512, 256
    nf, nn = f // tf, d // tn
    prefetch = (tile_expert, tile_row, tile_rows)
    frozen = lambda last: (lambda i, j, nr: jnp.where(nr[i] > 0, j, last))
    jf, jn = frozen(nf - 1), frozen(nn - 1)
    act = pl.pallas_call(
        functools.partial(_moe_up_kernel, ts=MOE_TS),
        grid_spec=pltpu.PrefetchScalarGridSpec(
            num_scalar_prefetch=3,
            grid=(r // tm, nf),
            in_specs=[pl.BlockSpec((tm, d), lambda i, j, te, tr, nr: (tr[i], 0)),
                      pl.BlockSpec((None, None, d, tf), lambda i, j, te, tr, nr: (layer, te[i], 0, jf(i, j, nr))),
                      pl.BlockSpec((None, None, d, tf), lambda i, j, te, tr, nr: (layer, te[i], 0, jf(i, j, nr)))],
            out_specs=pl.BlockSpec((tm, tf), lambda i, j, te, tr, nr: (i, j))),
        out_shape=jax.ShapeDtypeStruct((r, f), BF16),
        compiler_params=_params(("arbitrary", "arbitrary")),
        name="moe_up",
    )(*prefetch, hs, w1, w3)
    return pl.pallas_call(
        functools.partial(_moe_down_kernel, ts=MOE_TS),
        grid_spec=pltpu.PrefetchScalarGridSpec(
            num_scalar_prefetch=3,
            grid=(r // tm, nn),
            in_specs=[pl.BlockSpec((tm, f), lambda i, j, te, tr, nr: (tr[i], 0)),
                      pl.BlockSpec((None, None, f, tn), lambda i, j, te, tr, nr: (layer, te[i], 0, jn(i, j, nr)))],
            out_specs=pl.BlockSpec((tm, tn), lambda i, j, te, tr, nr: (i, j))),
        out_shape=jax.ShapeDtypeStruct((r, d), F32),
        compiler_params=_params(("arbitrary", "arbitrary")),
        name="moe_down",
    )(*prefetch, act, w2)


def _combine_kernel(p1_ref, p2_ref, y_ref, x_ref, r_ref, mod_ref, o_ref, y1_scr, y2_scr, sem, *, tc, tpb, lc):
    i = pl.program_id(0)
    last = pl.num_programs(0) - 1
    slot = i % 2

    def copies(idx1, idx2, r, s):
        return (pltpu.make_async_copy(y_ref.at[pl.ds(idx1, 1)], y1_scr.at[s, pl.ds(r, 1)], sem.at[s]),
                pltpu.make_async_copy(y_ref.at[pl.ds(idx2, 1)], y2_scr.at[s, pl.ds(r, 1)], sem.at[s]))

    def issue(ii, s):
        def body(r, carry):
            c1, c2 = copies(p1_ref[ii * tc + r], p2_ref[ii * tc + r], r, s)
            c1.start()
            c2.start()
            return carry
        lax.fori_loop(0, tc, body, 0, unroll=8)

    @pl.when(i == 0)
    def _():
        issue(0, 0)

    @pl.when(i < last)
    def _():
        issue(jnp.minimum(i + 1, last), 1 - slot)

    def drain(r, carry):
        c1, c2 = copies(0, 0, r, slot)
        c1.wait()
        c2.wait()
        return carry

    lax.fori_loop(0, tc, drain, 0, unroll=8)
    d = x_ref.shape[1]
    gate = _gate_tile(mod_ref, i, tc, tpb, lc, 5, 0, d, d)
    route = r_ref[...]
    f = route[:, 2:3] * y1_scr[slot] + route[:, 3:4] * y2_scr[slot]
    o_ref[...] = x_ref[...] + gate * f


def _combine(pos1, pos2, y, x, route, mod, t, lc):
    m, d = x.shape
    tc = 256
    return pl.pallas_call(
        functools.partial(_combine_kernel, tc=tc, tpb=t // tc, lc=lc),
        grid_spec=pltpu.PrefetchScalarGridSpec(
            num_scalar_prefetch=2,
            grid=(m // tc,),
            in_specs=[pl.BlockSpec(memory_space=pl.ANY),
                      pl.BlockSpec((tc, d), lambda i, p1, p2: (i, 0)),
                      pl.BlockSpec((tc, LANES), lambda i, p1, p2: (i, 0)),
                      pl.BlockSpec(mod.shape, lambda i, p1, p2: (0, 0))],
            out_specs=pl.BlockSpec((tc, d), lambda i, p1, p2: (i, 0)),
            scratch_shapes=[pltpu.VMEM((2, tc, d), F32), pltpu.VMEM((2, tc, d), F32),
                            pltpu.SemaphoreType.DMA((2,))]),
        out_shape=jax.ShapeDtypeStruct((m, d), F32),
        compiler_params=_params(("arbitrary",)),
        name="moe_combine",
    )(pos1, pos2, y, x, route, mod)


def _ffn_moe(x, mod, w_router, w1, w3, w2, layer, t, lc, ctx_out):
    m, d = x.shape
    tm = MOE_TM
    h, route = _router(x, mod, w_router, t, lc)
    experts = route[:, 0:TOP_K].astype(jnp.int32).reshape(-1)
    token = jnp.arange(TOP_K * m, dtype=jnp.int32) // TOP_K
    used = jnp.logical_or(ctx_out, token % t >= lc)
    onehot = ((experts[:, None] == jnp.arange(N_EXPERTS)[None, :]) & used[:, None]).astype(jnp.int32)
    rank = jnp.sum((jnp.cumsum(onehot, axis=0) - onehot) * onehot, axis=1)
    count = jnp.sum(onehot, axis=0)
    tiles = (count + tm - 1) // tm
    tile_end = jnp.cumsum(tiles)
    tile_start = tile_end - tiles
    n_tiles = (TOP_K * m + N_EXPERTS * (tm - 1)) // tm
    n_rows = n_tiles * tm
    pos = (tile_start * tm)[experts] + rank
    src = jnp.full((n_rows,), -1, jnp.int32).at[jnp.where(used, pos, n_rows)].set(token, mode='drop')
    pos = jnp.where(used, pos, 0)
    tile_ids = jnp.arange(n_tiles)
    tile_row = jnp.minimum(tile_ids, tile_end[-1] - 1)
    tile_expert = jnp.sum((tile_row[:, None] >= tile_end[None, :]).astype(jnp.int32), axis=1)
    tile_rows = jnp.clip(count[tile_expert] - (tile_ids - tile_start[tile_expert]) * tm, 0, tm)
    tile_rows = jnp.where(tile_ids < tile_end[-1], tile_rows, 0)
    hs = _gather_rows(src, h, n_rows)
    y = _moe_ffn(tile_expert.astype(jnp.int32), tile_row.astype(jnp.int32), tile_rows.astype(jnp.int32),
                 hs, w1, w3, w2, layer)
    pos = pos.reshape(m, TOP_K)
    return _combine(pos[:, 0], pos[:, 1], y, x, route, mod, t, lc)


def _rope_tables(n_lat, lc, rot_dim, lane0, period):
    tok = jnp.arange(n_lat)
    axis_dim = rot_dim // 2
    inv_freq = ROPE_BASE ** (-jnp.arange(0, axis_dim, 2, dtype=F32) / axis_dim)
    ang_r = (tok // GRID_W).astype(F32)[:, None] * inv_freq
    ang_c = (tok % GRID_W).astype(F32)[:, None] * inv_freq
    ang = jnp.concatenate([ang_r, ang_r, ang_c, ang_c], axis=-1)
    cos, sin = jnp.cos(ang), jnp.sin(ang)
    seg = rot_dim // 4
    lane = np.arange(LANES)
    rel = (lane - lane0) % period
    active = (lane >= lane0) & (rel < rot_dim)
    even = ((rel // seg) % 2 == 0)
    idx = np.where(active, rel, 0)
    cos_t = jnp.where(active[None, :], cos[:, idx], 1.0)
    sin_t = jnp.where(active[None, :], sin[:, idx], 0.0)
    sin_next = jnp.where(even[None, :], -sin_t, 0.0)
    sin_prev = jnp.where(even[None, :], 0.0, sin_t)
    ctx = lambda a, v: jnp.concatenate([jnp.full((lc, LANES), v, F32), a], axis=0)
    return ctx(cos_t, 1.0), ctx(sin_next, 0.0), ctx(sin_prev, 0.0)


def _relayout_w_in(w):
    d = w.shape[0]
    seg = lambda lo, n: w[:, lo:lo + n].astype(BF16)
    z = lambda n: jnp.zeros((d, n), BF16)
    a0, b0, g0, c0, d0 = 0, 1536, 3584, 3600, 4144
    parts = [seg(c0, 384), seg(c0 + 384, 128),
             seg(a0, 1536), seg(b0, 2048), seg(d0, 1536),
             seg(g0, 16), z(KR_LANE - 16), seg(c0 + 512, 32), z(LANES - KR_LANE - 32)]
    return jnp.concatenate(parts, axis=1)


def _mla_weights(w_uq, w_ukv):
    hd = MLA_NOPE + MLA_ROPE
    wq = jnp.pad(w_uq.reshape(MLA_Q_RANK, MLA_HEADS, hd), ((0, 0), (0, 0), (0, LANES - hd)))
    wkv = w_ukv.reshape(MLA_KV_RANK, MLA_HEADS, MLA_NOPE + MLA_V)
    wk = jnp.pad(wkv[:, :, :MLA_NOPE], ((0, 0), (0, 0), (0, LANES - MLA_NOPE)))
    wv = wkv[:, :, MLA_NOPE:]
    return (wq.reshape(MLA_Q_RANK, -1).astype(BF16), wk.reshape(MLA_KV_RANK, -1).astype(BF16),
            wv.reshape(MLA_KV_RANK, -1).astype(BF16))


def kernel(x, c, ctx, c_ctx, w_ada, b_ada, w_in, w_out, da_q_gain, da_k_gain, da_lambda, da_out_gain, ml_conv_w, ml_conv_b, ml_gate_b, ml_out_gain, mla_cq_gain, mla_ckv_gain, mla_w_uq, mla_w_ukv, mla_q_gain, mla_k_gain, na_q_gain, na_k_gain, na_rpb, ffn_w1, ffn_w3, ffn_w2, moe_router, moe_w1, moe_w3, moe_w2):
    b, n_lat, d = x.shape
    lc = ctx.shape[1]
    t = lc + n_lat
    depth = w_in.shape[0]
    assert b <= 4 and lc == NA_QR * GRID_W and lc == TQ_DIFF == TQ_MLA and n_lat % lc == 0 and t % 768 == 0

    xs = jnp.concatenate([ctx, x], axis=1).reshape(b * t, d)
    cond = jnp.zeros((8, d), F32).at[:b].set(c).at[4].set(c_ctx)
    mod = _mod_table(cond, w_ada, b_ada)
    rope_da = _rope_tables(n_lat, lc, DA_QK, 0, DA_QK)
    rope_mla = _rope_tables(n_lat, lc, MLA_ROPE, KR_LANE, LANES)

    for l in range(depth):
        lam_init = 0.8 - 0.6 * math.exp(-0.3 * l)
        p = _inproj(xs, mod[l], _relayout_w_in(w_in[l]), t, lc)
        mix_a = _mixer_diff(p, rope_da, da_q_gain[l], da_k_gain[l], da_lambda[l], da_out_gain[l], lam_init, b, t, lc)
        mix_b = _mixer_mlstm(p, ml_conv_w[l], ml_conv_b[l], ml_gate_b[l], ml_out_gain[l], b, t, lc)
        wq, wk, wv = _mla_weights(mla_w_uq[l], mla_w_ukv[l])
        qc, kc, vc = _mla_prep(p, rope_mla, mla_cq_gain[l], mla_ckv_gain[l], wq, wk, wv,
                               mla_q_gain[l], mla_k_gain[l], t)
        mix_c = _mla_attn(qc, kc, vc, b, t, lc)
        mix_d = _mixer_na(p, na_q_gain[l], na_k_gain[l], na_rpb[l], b, t, lc)
        xs = _outproj((mix_a, mix_b, mix_c, mix_d), w_out[l].astype(BF16), xs, mod[l], t, lc)
        if l % 2 == 0:
            xs = _ffn_dense(xs, mod[l], ffn_w1[l // 2], ffn_w3[l // 2], ffn_w2[l // 2], t, lc)
        else:
            xs = _ffn_moe(xs, mod[l], moe_router[l // 2], moe_w1, moe_w3, moe_w2, l // 2, t, lc, l < depth - 1)
    return xs.reshape(b, t, d)[:, lc:]
```

```python
import functools
import math

import numpy as np
import jax
import jax.numpy as jnp
from jax import lax
from jax.experimental import pallas as pl
from jax.experimental.pallas import tpu as pltpu

F32 = jnp.float32
BF16 = jnp.bfloat16

GRID_W = 64
DA_HEADS, DA_QK, DA_V = 4, 64, 128
ML_HEADS, ML_DIM, ML_CONV = 4, 128, 3
MLA_HEADS, MLA_NOPE, MLA_ROPE, MLA_V = 8, 64, 32, 64
MLA_Q_RANK, MLA_KV_RANK = 384, 128
NA_HEADS, NA_DIM, NA_WIN_R, NA_WIN_C = 8, 64, 8, 16
N_EXPERTS, TOP_K = 8, 2
ROPE_BASE = 10000.0
RMS_EPS = 1e-6
LOG2E = math.log2(math.e)

LANES = 128
VMEM_LIMIT = 56 * 1024 * 1024

P_CQ, P_CKV = 0, 384
P_AQ, P_AK, P_AV = 512, 1024, 1536
P_BQ, P_BK, P_BV, P_BO = 2048, 2560, 3072, 3584
P_DQ, P_DK, P_DV = 4096, 4608, 5120
P_LAST = 5632
P_WIDTH = 5760
KR_LANE = 64

TQ_DIFF, TQ_MLA = 256, 256
ML_L = 256
NA_QR, NA_KR = 4, 12
MOE_TM, MOE_TS = 1280, 256


def _params(sem):
    return pltpu.CompilerParams(dimension_semantics=sem, vmem_limit_bytes=VMEM_LIMIT)


def _silu(x):
    return x * (1.0 / (1.0 + jnp.exp(-x)))


def _sigmoid(x):
    return 1.0 / (1.0 + jnp.exp(-x))


def _rms(x, n=None):
    n = x.shape[-1] if n is None else n
    return x * lax.rsqrt(jnp.sum(x * x, axis=-1, keepdims=True) * (1.0 / n) + RMS_EPS)


def _rms_halves(x, gain):
    lo = lax.broadcasted_iota(jnp.int32, x.shape, 1) < 64
    x2 = x * x
    s_lo = jnp.sum(jnp.where(lo, x2, 0.0), axis=-1, keepdims=True)
    s_hi = jnp.sum(jnp.where(lo, 0.0, x2), axis=-1, keepdims=True)
    ms = jnp.where(lo, s_lo, s_hi) * (1.0 / 64)
    return x * lax.rsqrt(ms + RMS_EPS) * gain


def _rope(x, cos, sin_next, sin_prev, seg):
    return x * cos + pltpu.roll(x, LANES - seg, 1) * sin_next + pltpu.roll(x, seg, 1) * sin_prev


def _softmax2_pv(s2, v):
    e = jnp.exp2(s2 - jnp.max(s2, axis=-1, keepdims=True))
    return _dot(e.astype(BF16), v) * (1.0 / jnp.sum(e, axis=-1, keepdims=True))


def _dot(a, b):
    return jnp.dot(a, b, preferred_element_type=F32)


def _dot_nt(a, b):
    return lax.dot_general(a, b, (((1,), (1,)), ((), ())), preferred_element_type=F32)


def _dot_tn(a, b):
    return lax.dot_general(a, b, (((0,), (0,)), ((), ())), preferred_element_type=F32)


def _split3(a):
    a1 = a.astype(BF16)
    r = a - a1.astype(F32)
    a2 = r.astype(BF16)
    a3 = (r - a2.astype(F32)).astype(BF16)
    return a1, a2, a3


def _dot_f32(a, b):
    a1, a2, a3 = _split3(a)
    b1, b2, b3 = _split3(b)
    return (_dot(a1, b1) + (_dot(a1, b2) + _dot(a2, b1))
            + (_dot(a1, b3) + _dot(a2, b2) + _dot(a3, b1)))


def _adaln_tile(x, mod_ref, i, tm, tpb, lc, c_shift, c_scale):
    d = x.shape[1]
    b = i // tpb
    row = (i % tpb) * tm + lax.broadcasted_iota(jnp.int32, (tm, 1), 0)
    is_ctx = row < lc
    shift = jnp.where(is_ctx, mod_ref[4:5, c_shift * d:(c_shift + 1) * d],
                      mod_ref[pl.ds(b, 1), c_shift * d:(c_shift + 1) * d])
    scale = jnp.where(is_ctx, mod_ref[4:5, c_scale * d:(c_scale + 1) * d],
                      mod_ref[pl.ds(b, 1), c_scale * d:(c_scale + 1) * d])
    return _rms(x) * (1.0 + scale) + shift


def _gate_tile(mod_ref, i, tm, tpb, lc, c_gate, col0, width, d):
    b = i // tpb
    row = (i % tpb) * tm + lax.broadcasted_iota(jnp.int32, (tm, 1), 0)
    lo = c_gate * d + col0
    return jnp.where(row < lc, mod_ref[4:5, lo:lo + width], mod_ref[pl.ds(b, 1), lo:lo + width])


def _mod_kernel(c_ref, w_ref, b_ref, o_ref):
    s = _silu(c_ref[...]).astype(BF16)
    o_ref[...] = _dot(s, w_ref[...].astype(BF16)) + b_ref[...]


def _mod_table(cond, w_ada, b_ada):
    depth, d, n = w_ada.shape
    tn = 1024
    return pl.pallas_call(
        _mod_kernel,
        grid=(depth, n // tn),
        in_specs=[pl.BlockSpec((8, d), lambda l, j: (0, 0)),
                  pl.BlockSpec((None, d, tn), lambda l, j: (l, 0, j)),
                  pl.BlockSpec((None, 1, tn), lambda l, j: (l, 0, j))],
        out_specs=pl.BlockSpec((None, 8, tn), lambda l, j: (l, 0, j)),
        out_shape=jax.ShapeDtypeStruct((depth, 8, n), F32),
        compiler_params=_params(("arbitrary", "arbitrary")),
        name="mod_table",
    )(cond, w_ada, b_ada.reshape(depth, 1, n))


def _inproj_kernel(x_ref, mod_ref, w_ref, o_ref, h_scr, *, tm, tpb, lc):
    i = pl.program_id(0)

    @pl.when(pl.program_id(1) == 0)
    def _():
        h_scr[...] = _adaln_tile(x_ref[...], mod_ref, i, tm, tpb, lc, 0, 1).astype(BF16)

    o_ref[...] = _dot(h_scr[...], w_ref[...])


def _inproj(x, mod, w, t, lc):
    m, d = x.shape
    n = w.shape[1]
    tm, tn = 768, 1152
    return pl.pallas_call(
        functools.partial(_inproj_kernel, tm=tm, tpb=t // tm, lc=lc),
        grid=(m // tm, n // tn),
        in_specs=[pl.BlockSpec((tm, d), lambda i, j: (i, 0)),
                  pl.BlockSpec(mod.shape, lambda i, j: (0, 0)),
                  pl.BlockSpec((d, tn), lambda i, j: (0, j))],
        out_specs=pl.BlockSpec((tm, tn), lambda i, j: (i, j)),
        out_shape=jax.ShapeDtypeStruct((m, n), F32),
        scratch_shapes=[pltpu.VMEM((tm, d), BF16)],
        compiler_params=_params(("arbitrary", "arbitrary")),
        name="inproj",
    )(x, mod, w)


def _diff_attn_kernel(q_ref, k_ref, v_ref, cos_ref, sn_ref, sp_ref, qg_ref, kg_ref, og_ref, lam_ref,
                      o_ref, kn_scr, vb_scr, *, tq, lc, lam_init, scale):
    qi = pl.program_id(2)
    t = k_ref.shape[0]

    @pl.when(qi == 0)
    def _():
        k = _rms_halves(k_ref[...], kg_ref[...])
        kn_scr[...] = _rope(k, cos_ref[...], sn_ref[...], sp_ref[...], 16).astype(BF16)
        vb_scr[...] = v_ref[...].astype(BF16)

    r0 = pl.multiple_of(qi * tq, tq)
    q = _rms_halves(q_ref[...], qg_ref[...])
    q = _rope(q, cos_ref[pl.ds(r0, tq), :], sn_ref[pl.ds(r0, tq), :], sp_ref[pl.ds(r0, tq), :], 16) * scale
    lo = lax.broadcasted_iota(jnp.int32, q.shape, 1) < 64
    q1 = jnp.where(lo, q, 0.0).astype(BF16)
    q2 = jnp.where(lo, 0.0, q).astype(BF16)
    lp = lam_ref[...]
    lam = (jnp.exp(jnp.sum(lp[0:1] * lp[1:2], axis=-1, keepdims=True))
           - jnp.exp(jnp.sum(lp[2:3] * lp[3:4], axis=-1, keepdims=True)) + lam_init)

    def attend(nk):
        kk = kn_scr[0:nk, :]
        vv = vb_scr[0:nk, :]
        o = _softmax2_pv(_dot_nt(q1, kk), vv) - lam * _softmax2_pv(_dot_nt(q2, kk), vv)
        o_ref[...] = (_rms(o) * og_ref[...] * (1.0 - lam_init)).astype(BF16)

    @pl.when(qi * tq < lc)
    def _():
        attend(lc)

    @pl.when(qi * tq >= lc)
    def _():
        attend(t)


def _mixer_diff(p, tabs, q_gain, k_gain, lam_params, out_gain, lam_init, b, t, lc):
    m = p.shape[0]
    tq = TQ_DIFF
    nq = t // tq
    cos, sn, sp = tabs
    tile2 = lambda g: jnp.tile(g, 2).reshape(1, LANES)
    full = lambda a: pl.BlockSpec(a.shape, lambda bb, h, qi: (0,) * a.ndim)
    args = (cos, sn, sp, tile2(q_gain), tile2(k_gain), out_gain.reshape(1, LANES), lam_params)
    return pl.pallas_call(
        functools.partial(_diff_attn_kernel, tq=tq, lc=lc, lam_init=lam_init, scale=DA_QK ** -0.5 * LOG2E),
        grid=(b, DA_HEADS, nq),
        in_specs=[pl.BlockSpec((tq, LANES), lambda bb, h, qi: (bb * nq + qi, P_AQ // LANES + h)),
                  pl.BlockSpec((t, LANES), lambda bb, h, qi: (bb, P_AK // LANES + h)),
                  pl.BlockSpec((t, LANES), lambda bb, h, qi: (bb, P_AV // LANES + h))]
                 + [full(a) for a in args],
        out_specs=pl.BlockSpec((tq, LANES), lambda bb, h, qi: (bb * nq + qi, h)),
        out_shape=jax.ShapeDtypeStruct((m, DA_HEADS * DA_V), BF16),
        scratch_shapes=[pltpu.VMEM((t, LANES), BF16), pltpu.VMEM((t, LANES), BF16)],
        compiler_params=_params(("arbitrary", "arbitrary", "arbitrary")),
        name="diff_attn",
    )(p, p, p, *args)


def _mla_prep_kernel(cq_ref, ckv_ref, last_ref, cos_ref, sn_ref, sp_ref, cqg_ref, ckvg_ref, wq_ref, wk_ref,
                     wv_ref, qg_ref, kg_ref, q_out, k_out, v_out, *, scale):
    hd = MLA_NOPE + MLA_ROPE
    cos, sn, sp = cos_ref[...], sn_ref[...], sp_ref[...]
    cqn = (_rms(cq_ref[...]) * cqg_ref[...]).astype(BF16)
    ckvn = (_rms(ckv_ref[...]) * ckvg_ref[...]).astype(BF16)
    q = _dot(cqn, wq_ref[...])
    kk = _dot(ckvn, wk_ref[...])
    v_out[...] = _dot(ckvn, wv_ref[...]).astype(BF16)
    last = last_ref[...]
    lane = lax.broadcasted_iota(jnp.int32, last.shape, 1)
    kr = jnp.where((lane >= KR_LANE) & (lane < KR_LANE + MLA_ROPE), last, 0.0)
    for h in range(MLA_HEADS):
        sl = slice(h * LANES, (h + 1) * LANES)
        qh = _rms(q[:, sl], hd) * qg_ref[...]
        q_out[:, sl] = (_rope(qh, cos, sn, sp, 8) * scale).astype(BF16)
        kh = _rms(kk[:, sl] + kr, hd) * kg_ref[...]
        k_out[:, sl] = _rope(kh, cos, sn, sp, 8).astype(BF16)


def _mla_prep(p, tabs, cq_gain, ckv_gain, wq, wk, wv, q_gain, k_gain, t):
    m = p.shape[0]
    tm = 768
    tpb = t // tm
    hw = MLA_HEADS * LANES
    cos, sn, sp = tabs
    tab = pl.BlockSpec((tm, LANES), lambda i: (i % tpb, 0))
    full = lambda a: pl.BlockSpec(a.shape, lambda i: (0,) * a.ndim)
    pad = lambda g: jnp.pad(g, (0, LANES - g.shape[0])).reshape(1, LANES)
    consts = (cq_gain.reshape(1, -1), ckv_gain.reshape(1, -1), wq, wk, wv, pad(q_gain), pad(k_gain))
    return pl.pallas_call(
        functools.partial(_mla_prep_kernel, scale=(MLA_NOPE + MLA_ROPE) ** -0.5 * LOG2E),
        grid=(m // tm,),
        in_specs=[pl.BlockSpec((tm, MLA_Q_RANK), lambda i: (i, P_CQ // MLA_Q_RANK)),
                  pl.BlockSpec((tm, LANES), lambda i: (i, P_CKV // LANES)),
                  pl.BlockSpec((tm, LANES), lambda i: (i, P_LAST // LANES)),
                  tab, tab, tab] + [full(a) for a in consts],
        out_specs=[pl.BlockSpec((tm, hw), lambda i: (i, 0)),
                   pl.BlockSpec((tm, hw), lambda i: (i, 0)),
                   pl.BlockSpec((tm, MLA_HEADS * MLA_V), lambda i: (i, 0))],
        out_shape=[jax.ShapeDtypeStruct((m, hw), BF16), jax.ShapeDtypeStruct((m, hw), BF16),
                   jax.ShapeDtypeStruct((m, MLA_HEADS * MLA_V), BF16)],
        compiler_params=_params(("arbitrary",)),
        name="mla_prep",
    )(p, p, p, cos, sn, sp, *consts)


def _mla_attn_kernel(q_ref, k_ref, v_ref, o_ref, *, tq, lc):
    qi = pl.program_id(2)
    t = k_ref.shape[0]

    def attend(nk):
        vv = v_ref[0:nk, :]
        outs = []
        for h in range(2):
            sl = slice(h * LANES, (h + 1) * LANES)
            outs.append(_softmax2_pv(_dot_nt(q_ref[:, sl], k_ref[0:nk, sl]), vv))
        lo = lax.broadcasted_iota(jnp.int32, outs[0].shape, 1) < MLA_V
        o_ref[...] = jnp.where(lo, outs[0], outs[1]).astype(BF16)

    @pl.when(qi * tq < lc)
    def _():
        attend(lc)

    @pl.when(qi * tq >= lc)
    def _():
        attend(t)


def _mla_attn(q, k, v, b, t, lc):
    m = q.shape[0]
    tq = TQ_MLA
    nq = t // tq
    return pl.pallas_call(
        functools.partial(_mla_attn_kernel, tq=tq, lc=lc),
        grid=(b, MLA_HEADS // 2, nq),
        in_specs=[pl.BlockSpec((tq, 2 * LANES), lambda bb, hp, qi: (bb * nq + qi, hp)),
                  pl.BlockSpec((t, 2 * LANES), lambda bb, hp, qi: (bb, hp)),
                  pl.BlockSpec((t, LANES), lambda bb, hp, qi: (bb, hp))],
        out_specs=pl.BlockSpec((tq, LANES), lambda bb, hp, qi: (bb * nq + qi, hp)),
        out_shape=jax.ShapeDtypeStruct((m, MLA_HEADS * MLA_V), BF16),
        compiler_params=_params(("arbitrary", "arbitrary", "arbitrary")),
        name="mla_attn",
    )(q, k, v)


def _na_kernel(q_ref, k_ref, v_ref, bias_ref, qg_ref, kg_ref, o_ref, kn_scr, vb_scr, *, lc, rows, scale):
    s = pl.program_id(1)
    tq = q_ref.shape[0]
    n_ctx_blk = lc // tq
    band = NA_KR * GRID_W
    npair = NA_HEADS // 2

    @pl.when(s == 0)
    def _():
        for pp in range(npair):
            sl = slice(pp * LANES, (pp + 1) * LANES)
            kn_scr[:, sl] = _rms_halves(k_ref[:, sl], kg_ref[:, sl]).astype(BF16)
        vb_scr[...] = v_ref[...].astype(BF16)

    def heads(pv_fn):
        for pp in range(npair):
            sl = slice(pp * LANES, (pp + 1) * LANES)
            q = _rms_halves(q_ref[:, sl], qg_ref[:, sl]) * scale
            lo = lax.broadcasted_iota(jnp.int32, q.shape, 1) < NA_DIM
            qs = jnp.concatenate([jnp.where(lo, q, 0.0), jnp.where(lo, 0.0, q)], axis=0).astype(BF16)
            o2 = pv_fn(qs, sl, pp)
            o_ref[:, sl] = jnp.where(lo, o2[0:tq], o2[tq:2 * tq]).astype(BF16)

    @pl.when(s < n_ctx_blk)
    def _():
        def f(qs, sl, pp):
            return _softmax2_pv(_dot_nt(qs, kn_scr[0:lc, sl]), vb_scr[0:lc, sl])
        heads(f)

    @pl.when(s >= n_ctx_blk)
    def _():
        rb = s - n_ctx_blk
        krow = jnp.clip(rb * NA_QR - NA_WIN_R // 2, 0, rows - NA_KR)
        k0 = pl.multiple_of(lc + krow * GRID_W, GRID_W)

        def f(qs, sl, pp):
            s_loc = _dot_nt(qs, kn_scr[pl.ds(k0, band), sl]) + bias_ref[pp]
            s_ctx = _dot_nt(qs, kn_scr[0:lc, sl])
            mx = jnp.maximum(jnp.max(s_loc, axis=-1, keepdims=True), jnp.max(s_ctx, axis=-1, keepdims=True))
            e_loc = jnp.exp2(s_loc - mx)
            e_ctx = jnp.exp2(s_ctx - mx)
            den = jnp.sum(e_loc, axis=-1, keepdims=True) + jnp.sum(e_ctx, axis=-1, keepdims=True)
            num = _dot(e_loc.astype(BF16), vb_scr[pl.ds(k0, band), sl]) + _dot(e_ctx.astype(BF16), vb_scr[0:lc, sl])
            return num * (1.0 / den)
        heads(f)


def _na_variant(rb, n_rb):
    return jnp.where(rb == 0, 0, jnp.where(rb == n_rb - 1, 2, 1))


def _na_bias_table(rpb, rows):
    n_rb = rows // NA_QR
    rbs = np.array([0, 1, n_rb - 1])
    q_rows = rbs[:, None] * NA_QR + np.arange(NA_QR)
    key_rows = np.clip(rbs * NA_QR - NA_WIN_R // 2, 0, rows - NA_KR)[:, None] + np.arange(NA_KR)
    cols = np.arange(GRID_W)
    r0 = np.clip(q_rows - NA_WIN_R // 2, 0, rows - NA_WIN_R)[:, :, None]
    c0 = np.clip(cols - NA_WIN_C // 2, 0, GRID_W - NA_WIN_C)[:, None]
    kr = key_rows[:, None, :]
    ok_r = (kr >= r0) & (kr < r0 + NA_WIN_R)
    ok_c = (cols[None, :] >= c0) & (cols[None, :] < c0 + NA_WIN_C)
    rel_r = np.clip(kr - q_rows[:, :, None] + NA_WIN_R - 1, 0, 2 * NA_WIN_R - 2)
    rel_c = np.clip(cols[None, :] - cols[:, None] + NA_WIN_C - 1, 0, 2 * NA_WIN_C - 2)
    oh_c = jnp.asarray(np.eye(2 * NA_WIN_C - 1, dtype=np.float32)[rel_c])
    rpb2 = rpb.reshape(NA_HEADS // 2, 2, 2 * NA_WIN_R - 1, 2 * NA_WIN_C - 1)
    tiles = jnp.einsum('phde,qce->phdqc', rpb2, oh_c, precision=lax.Precision.HIGHEST)
    tiles = jnp.where(jnp.asarray(ok_c), tiles * LOG2E, -jnp.inf)
    npair, nv, tq, band = NA_HEADS // 2, len(rbs), NA_QR * GRID_W, NA_KR * GRID_W
    return pl.pallas_call(
        _na_bias_kernel,
        grid_spec=pltpu.PrefetchScalarGridSpec(
            num_scalar_prefetch=2,
            grid=(npair, nv),
            in_specs=[pl.BlockSpec((None,) + tiles.shape[1:], lambda p, v, rel, ok: (p, 0, 0, 0, 0))],
            out_specs=pl.BlockSpec((None, None, 2 * tq, band), lambda p, v, rel, ok: (p, v, 0, 0))),
        out_shape=jax.ShapeDtypeStruct((npair, nv, 2 * tq, band), F32),
        compiler_params=_params(("arbitrary", "arbitrary")),
        name="na_bias",
    )(jnp.asarray(rel_r.reshape(-1), jnp.int32), jnp.asarray(ok_r.reshape(-1), jnp.int32), tiles)


def _na_bias_kernel(rel_ref, ok_ref, t_ref, o_ref):
    v = pl.program_id(1)
    for hh in range(2):
        for i in range(NA_QR):
            rows = slice((hh * NA_QR + i) * GRID_W, (hh * NA_QR + i + 1) * GRID_W)
            for k in range(0, NA_KR, 2):
                pair = []
                for kk in (k, k + 1):
                    idx = (v * NA_QR + i) * NA_KR + kk
                    pair.append(jnp.where(ok_ref[idx] > 0, t_ref[hh, rel_ref[idx]], -jnp.inf))
                o_ref[rows, k * GRID_W:(k + 2) * GRID_W] = jnp.concatenate(pair, axis=1)


def _mixer_na(p, q_gain, k_gain, rpb, b, t, lc):
    m = p.shape[0]
    rows = (t - lc) // GRID_W
    tq = NA_QR * GRID_W
    nblk = t // tq
    n_ctx_blk = lc // tq
    bias = _na_bias_table(rpb, rows)
    tile8 = lambda g: jnp.tile(g, NA_HEADS).reshape(1, NA_HEADS * NA_DIM)
    w = NA_HEADS * NA_DIM
    return pl.pallas_call(
        functools.partial(_na_kernel, lc=lc, rows=rows, scale=NA_DIM ** -0.5 * LOG2E),
        grid=(b, nblk),
        in_specs=[pl.BlockSpec((tq, w), lambda bb, s: (bb * nblk + s, P_DQ // w)),
                  pl.BlockSpec((t, w), lambda bb, s: (bb, P_DK // w)),
                  pl.BlockSpec((t, w), lambda bb, s: (bb, P_DV // w)),
                  pl.BlockSpec((NA_HEADS // 2, None, 2 * tq, NA_KR * GRID_W),
                               lambda bb, s: (0, _na_variant(jnp.maximum(s - n_ctx_blk, 0), rows // NA_QR), 0, 0)),
                  pl.BlockSpec((1, w), lambda bb, s: (0, 0)),
                  pl.BlockSpec((1, w), lambda bb, s: (0, 0))],
        out_specs=pl.BlockSpec((tq, w), lambda bb, s: (bb * nblk + s, 0)),
        out_shape=jax.ShapeDtypeStruct((m, w), BF16),
        scratch_shapes=[pltpu.VMEM((t, w), BF16), pltpu.VMEM((t, w), BF16)],
        compiler_params=_params(("arbitrary", "arbitrary")),
        name="na_attn",
    )(p, p, p, bias, tile8(q_gain), tile8(k_gain))


def _ml_prep_kernel(x_ref, w_ref, b_ref, o_ref, *, lc):
    t = x_ref.shape[0]
    x = x_ref[...]
    row = lax.broadcasted_iota(jnp.int32, (t, 1), 0)
    prev = jnp.where((row == 0) | (row == lc), 0.0, pltpu.roll(x, 1, 0))
    nxt = jnp.where((row == lc - 1) | (row == t - 1), 0.0, pltpu.roll(x, t - 1, 0))
    y = _silu(prev * w_ref[0:1, :] + x * w_ref[1:2, :] + nxt * w_ref[2:3, :] + b_ref[...])
    o_ref[...] = y * jnp.where(pl.program_id(1) == 1, ML_DIM ** -0.5, 1.0)


def _ml_prep(p, conv_w, conv_b, b, t, lc):
    m = p.shape[0]
    w = ML_HEADS * ML_DIM
    return pl.pallas_call(
        functools.partial(_ml_prep_kernel, lc=lc),
        grid=(b, 2),
        in_specs=[pl.BlockSpec((t, w), lambda bb, c: (bb, P_BQ // w + c)),
                  pl.BlockSpec((ML_CONV, w), lambda bb, c: (0, c)),
                  pl.BlockSpec((1, w), lambda bb, c: (0, c))],
        out_specs=pl.BlockSpec((t, w), lambda bb, c: (bb, c)),
        out_shape=jax.ShapeDtypeStruct((m, 2 * w), F32),
        compiler_params=_params(("arbitrary", "arbitrary")),
        name="mlstm_prep",
    )(p, conv_w, conv_b.reshape(1, -1))


def _ml_scan_kernel(qf_ref, kf_ref, vf_ref, gf_ref, qb_ref, kb_ref, vb_ref, gb_ref, of_ref, ob_ref,
                    c_scr, n_scr, m_scr):
    @pl.when(pl.program_id(1) == 0)
    def _():
        c_scr[...] = jnp.zeros_like(c_scr)
        n_scr[...] = jnp.zeros_like(n_scr)
        m_scr[...] = jnp.zeros_like(m_scr)

    L = qf_ref.shape[0]
    tt = lax.broadcasted_iota(jnp.int32, (L, L), 0)
    ss = lax.broadcasted_iota(jnp.int32, (L, L), 1)
    eye = tt == ss
    dirs = ((qf_ref, kf_ref, vf_ref, gf_ref, of_ref, ss <= tt, tt <= ss),
            (qb_ref, kb_ref, vb_ref, gb_ref, ob_ref, ss >= tt, tt >= ss))
    for d, (q_ref, k_ref, v_ref, g_ref, o_ref, cm, cm_t) in enumerate(dirs):
        for h in range(ML_HEADS):
            sl = slice(h * ML_DIM, (h + 1) * ML_DIM)
            _ml_chain(q_ref[:, sl], k_ref[:, sl], v_ref[:, sl], g_ref[h], o_ref.at[:, sl],
                      c_scr.at[d, h], n_scr.at[d, h], m_scr.at[d, h], eye, cm, cm_t)


def _ml_chain(q, k, v, g, o_ref, c_scr, n_scr, m_scr, eye, cm, cm_t):
    i_row = g[0:1, :]
    f_row = g[1:2, :]
    lf_row = jnp.minimum(f_row, 0.0) - jnp.log(1.0 + jnp.exp(-jnp.abs(f_row)))
    to_col = lambda r: jnp.sum(jnp.where(eye, r, 0.0), axis=-1, keepdims=True)
    lf_col = to_col(lf_row)
    i_col = to_col(i_row)
    b_col = jnp.sum(jnp.where(cm, lf_row, 0.0), axis=-1, keepdims=True)
    b_row = jnp.sum(jnp.where(cm_t, lf_col, 0.0), axis=0, keepdims=True)
    b_end = jnp.sum(lf_row, axis=-1, keepdims=True)
    m_old = m_scr[...]
    c_old = c_scr[...]
    n_old = n_scr[...]

    log_w = jnp.where(cm, b_col - b_row + i_row, -jnp.inf)
    inter = b_col + m_old
    m_t = jnp.maximum(inter, jnp.max(log_w, axis=-1, keepdims=True))
    qb = q.astype(BF16)
    kb = k.astype(BF16)
    vb = v.astype(BF16)
    s = _dot_nt(qb, kb) * jnp.exp(log_w - m_t)
    a = jnp.exp(inter - m_t)
    num = a * _dot(qb, c_old.astype(BF16)) + _dot(s.astype(BF16), vb)
    den = a * jnp.sum(q * n_old, axis=-1, keepdims=True) + jnp.sum(s, axis=-1, keepdims=True)
    o_ref[...] = num / jnp.maximum(jnp.abs(den), jnp.exp(-m_t))

    w_end = b_end - b_col + i_col
    m_new = jnp.maximum(b_end + m_old, jnp.max(w_end, axis=0, keepdims=True))
    decay = jnp.exp(b_end + m_old - m_new)
    we = jnp.exp(w_end - m_new)
    c_scr[...] = decay * c_old + _dot_tn(kb, (we * v).astype(BF16))
    n_scr[...] = decay * n_old + jnp.sum(we * k, axis=0, keepdims=True)
    m_scr[...] = m_new


def _ml_scan(qk, p, gates, b, t, lc):
    m = p.shape[0]
    L = ML_L
    nct, ncc = t // L, lc // L
    w = ML_HEADS * ML_DIM

    fwd = lambda j: j
    bwd = lambda j: jnp.where(j < ncc, ncc - 1 - j, nct - 1 - (j - ncc))

    def specs(chunk, d):
        blk = lambda col: pl.BlockSpec((L, w), lambda bb, j: (bb * nct + chunk(j), col))
        return [blk(0), blk(1), blk(P_BV // w),
                pl.BlockSpec((None, None, None, ML_HEADS, 2, L), lambda bb, j: (bb, d, chunk(j), 0, 0, 0))]

    out = lambda chunk: pl.BlockSpec((L, w), lambda bb, j: (bb * nct + chunk(j), 0))
    nh = ML_HEADS
    return pl.pallas_call(
        _ml_scan_kernel,
        grid=(b, nct),
        in_specs=specs(fwd, 0) + specs(bwd, 1),
        out_specs=[out(fwd), out(bwd)],
        out_shape=[jax.ShapeDtypeStruct((m, w), F32)] * 2,
        scratch_shapes=[pltpu.VMEM((2, nh, ML_DIM, ML_DIM), F32), pltpu.VMEM((2, nh, 1, ML_DIM), F32),
                        pltpu.VMEM((2, nh, 1, 1), F32)],
        compiler_params=_params(("arbitrary", "arbitrary")),
        name="mlstm_scan",
    )(qk, qk, p, gates, qk, qk, p, gates)


def _ml_finish_kernel(hf_ref, hb_ref, o_ref, g_ref, out_ref):
    hsum = hf_ref[...] + hb_ref[...]
    og = _sigmoid(o_ref[...])
    for h in range(ML_HEADS):
        sl = slice(h * ML_DIM, (h + 1) * ML_DIM)
        out_ref[:, sl] = (_rms(hsum[:, sl]) * g_ref[...] * og[:, sl]).astype(BF16)


def _ml_finish(hf, hb, p, out_gain):
    m = p.shape[0]
    w = ML_HEADS * ML_DIM
    tm = 768
    return pl.pallas_call(
        _ml_finish_kernel,
        grid=(m // tm,),
        in_specs=[pl.BlockSpec((tm, w), lambda i: (i, 0)),
                  pl.BlockSpec((tm, w), lambda i: (i, 0)),
                  pl.BlockSpec((tm, w), lambda i: (i, P_BO // w)),
                  pl.BlockSpec((1, ML_DIM), lambda i: (0, 0))],
        out_specs=pl.BlockSpec((tm, w), lambda i: (i, 0)),
        out_shape=jax.ShapeDtypeStruct((m, w), BF16),
        compiler_params=_params(("arbitrary",)),
        name="mlstm_finish",
    )(hf, hb, p, out_gain.reshape(1, ML_DIM))


def _mixer_mlstm(p, conv_w, conv_b, gate_b, out_gain, b, t, lc):
    qk = _ml_prep(p, conv_w, conv_b, b, t, lc)
    nct = t // ML_L
    g = p[:, P_LAST:P_LAST + 4 * ML_HEADS].reshape(b, nct, ML_L, 2, 2, ML_HEADS) + gate_b
    gates = g.transpose(0, 3, 1, 5, 4, 2)
    hf, hb = _ml_scan(qk, p, gates, b, t, lc)
    return _ml_finish(hf, hb, p, out_gain)


def _outproj_kernel(a_ref, b_ref, c_ref, d_ref, w_ref, x_ref, mod_ref, o_ref, *, tm, tpb, lc, tn, d):
    i = pl.program_id(0)
    j = pl.program_id(1)
    kw = a_ref.shape[1]
    y = _dot(a_ref[...], w_ref[0:kw, :])
    y += _dot(b_ref[...], w_ref[kw:2 * kw, :])
    y += _dot(c_ref[...], w_ref[2 * kw:3 * kw, :])
    y += _dot(d_ref[...], w_ref[3 * kw:4 * kw, :])
    row = (i % tpb) * tm + lax.broadcasted_iota(jnp.int32, (tm, 1), 0)
    b = i // tpb
    col = pl.multiple_of(2 * d + j * tn, LANES)
    gate = jnp.where(row < lc, mod_ref[4:5, pl.ds(col, tn)], mod_ref[pl.ds(b, 1), pl.ds(col, tn)])
    o_ref[...] = x_ref[...] + gate * y


def _outproj(mixes, w, x, mod, t, lc):
    m, d = x.shape
    tm, tn = 768, min(1024, d)
    kw = mixes[0].shape[1]
    mix_spec = pl.BlockSpec((tm, kw), lambda i, j: (i, 0))
    return pl.pallas_call(
        functools.partial(_outproj_kernel, tm=tm, tpb=t // tm, lc=lc, tn=tn, d=d),
        grid=(m // tm, d // tn),
        in_specs=[mix_spec] * 4 + [pl.BlockSpec((4 * kw, tn), lambda i, j: (0, j)),
                                   pl.BlockSpec((tm, tn), lambda i, j: (i, j)),
                                   pl.BlockSpec(mod.shape, lambda i, j: (0, 0))],
        out_specs=pl.BlockSpec((tm, tn), lambda i, j: (i, j)),
        out_shape=jax.ShapeDtypeStruct((m, d), F32),
        compiler_params=_params(("arbitrary", "arbitrary")),
        name="outproj",
    )(*mixes, w, x, mod)


def _ffn_kernel(x_ref, mod_ref, w1_ref, w3_ref, w2_ref, o_ref, h_scr, acc_scr, *, tm, tpb, lc):
    i = pl.program_id(0)
    j = pl.program_id(1)

    @pl.when(j == 0)
    def _():
        h_scr[...] = _adaln_tile(x_ref[...], mod_ref, i, tm, tpb, lc, 3, 4).astype(BF16)
        acc_scr[...] = jnp.zeros_like(acc_scr)

    h = h_scr[...]
    act = _silu(_dot(h, w1_ref[...])) * _dot(h, w3_ref[...])
    acc_scr[...] += _dot(act.astype(BF16), w2_ref[...])

    @pl.when(j == pl.num_programs(1) - 1)
    def _():
        d = x_ref.shape[1]
        gate = _gate_tile(mod_ref, i, tm, tpb, lc, 5, 0, d, d)
        o_ref[...] = x_ref[...] + gate * acc_scr[...]


def _ffn_dense(x, mod, w1, w3, w2, t, lc):
    m, d = x.shape
    f = w1.shape[1]
    tm, tf = 768, 512
    w1, w3, w2 = w1.astype(BF16), w3.astype(BF16), w2.astype(BF16)
    return pl.pallas_call(
        functools.partial(_ffn_kernel, tm=tm, tpb=t // tm, lc=lc),
        grid=(m // tm, f // tf),
        in_specs=[pl.BlockSpec((tm, d), lambda i, j: (i, 0), pipeline_mode=pl.Buffered(1)),
                  pl.BlockSpec(mod.shape, lambda i, j: (0, 0)),
                  pl.BlockSpec((d, tf), lambda i, j: (0, j)),
                  pl.BlockSpec((d, tf), lambda i, j: (0, j)),
                  pl.BlockSpec((tf, d), lambda i, j: (j, 0))],
        out_specs=pl.BlockSpec((tm, d), lambda i, j: (i, 0), pipeline_mode=pl.Buffered(1)),
        out_shape=jax.ShapeDtypeStruct((m, d), F32),
        scratch_shapes=[pltpu.VMEM((tm, d), BF16), pltpu.VMEM((tm, d), F32)],
        compiler_params=_params(("arbitrary", "arbitrary")),
        name="ffn_dense",
    )(x, mod, w1, w3, w2)


def _router_kernel(x_ref, mod_ref, wr_ref, h_ref, r_ref, *, tm, tpb, lc):
    i = pl.program_id(0)
    h = _adaln_tile(x_ref[...], mod_ref, i, tm, tpb, lc, 3, 4)
    h_ref[...] = h
    lane = lax.broadcasted_iota(jnp.int32, (tm, LANES), 1).astype(F32)
    logits = jnp.where(lane < N_EXPERTS, _dot_f32(h, wr_ref[...]), -jnp.inf)
    v1 = jnp.max(logits, axis=-1, keepdims=True)
    i1 = jnp.min(jnp.where(logits == v1, lane, float(LANES)), axis=-1, keepdims=True)
    rest = jnp.where(lane == i1, -jnp.inf, logits)
    v2 = jnp.max(rest, axis=-1, keepdims=True)
    i2 = jnp.min(jnp.where(rest == v2, lane, float(LANES)), axis=-1, keepdims=True)
    e2 = jnp.exp(v2 - v1)
    g1 = 1.0 / (1.0 + e2)
    g2 = e2 / (1.0 + e2)
    r_ref[...] = jnp.where(lane == 0, i1, jnp.where(lane == 1, i2, jnp.where(lane == 2, g1, jnp.where(lane == 3, g2, 0.0))))


def _router(x, mod, w_router, t, lc):
    m, d = x.shape
    tm = 768
    wr = jnp.pad(w_router, ((0, 0), (0, LANES - w_router.shape[1])))
    return pl.pallas_call(
        functools.partial(_router_kernel, tm=tm, tpb=t // tm, lc=lc),
        grid=(m // tm,),
        in_specs=[pl.BlockSpec((tm, d), lambda i: (i, 0)),
                  pl.BlockSpec(mod.shape, lambda i: (0, 0)),
                  pl.BlockSpec((d, LANES), lambda i: (0, 0))],
        out_specs=[pl.BlockSpec((tm, d), lambda i: (i, 0)),
                   pl.BlockSpec((tm, LANES), lambda i: (i, 0))],
        out_shape=[jax.ShapeDtypeStruct((m, d), F32), jax.ShapeDtypeStruct((m, LANES), F32)],
        compiler_params=_params(("arbitrary",)),
        name="moe_router",
    )(x, mod, wr)


def _gather_kernel(src_ref, h_ref, o_ref, buf, sem, *, tg):
    g = pl.program_id(0)
    last = pl.num_programs(0) - 1
    slot = g % 2

    def row_copy(idx, r, s):
        return pltpu.make_async_copy(h_ref.at[pl.ds(idx, 1)], buf.at[s, pl.ds(r, 1)], sem.at[s])

    def group_valid(gg):
        return src_ref[gg * tg] >= 0

    def issue(gg, s):
        def body(r, carry):
            row_copy(jnp.maximum(src_ref[gg * tg + r], 0), r, s).start()
            return carry
        lax.fori_loop(0, tg, body, 0, unroll=8)

    @pl.when((g == 0) & group_valid(0))
    def _():
        issue(0, 0)

    nxt = jnp.minimum(g + 1, last)

    @pl.when((g < last) & group_valid(nxt))
    def _():
        issue(nxt, 1 - slot)

    @pl.when(group_valid(g))
    def _():
        def body(r, carry):
            row_copy(0, r, slot).wait()
            return carry
        lax.fori_loop(0, tg, body, 0, unroll=8)
        o_ref[...] = buf[slot].astype(BF16)

    @pl.when(jnp.logical_not(group_valid(g)))
    def _():
        o_ref[...] = jnp.zeros_like(o_ref)


def _gather_rows(src, h, n_rows):
    d = h.shape[1]
    tg = MOE_TS
    return pl.pallas_call(
        functools.partial(_gather_kernel, tg=tg),
        grid_spec=pltpu.PrefetchScalarGridSpec(
            num_scalar_prefetch=1,
            grid=(n_rows // tg,),
            in_specs=[pl.BlockSpec(memory_space=pl.ANY)],
            out_specs=pl.BlockSpec((tg, d), lambda i, src: (i, 0)),
            scratch_shapes=[pltpu.VMEM((2, tg, d), h.dtype), pltpu.SemaphoreType.DMA((2,))]),
        out_shape=jax.ShapeDtypeStruct((n_rows, d), BF16),
        compiler_params=_params(("arbitrary",)),
        name="moe_gather",
    )(src, h)


def _rows_switch(n_rows, ts, tm, body):
    n_sub = (n_rows + ts - 1) // ts
    for c in range(tm // ts + 1):
        pl.when(n_sub == c)(functools.partial(body, c * ts))


def _moe_up_kernel(te_ref, tr_ref, nr_ref, hs_ref, w1_ref, w3_ref, a_ref, *, ts):
    tm, tf = a_ref.shape

    def body(n):
        if n > 0:
            h = hs_ref[0:n, :]
            act = _silu(_dot(h, w1_ref[...].astype(BF16))) * _dot(h, w3_ref[...].astype(BF16))
            a_ref[0:n, :] = act.astype(BF16)
        if n < tm:
            a_ref[n:tm, :] = jnp.zeros((tm - n, tf), BF16)

    _rows_switch(nr_ref[pl.program_id(0)], ts, tm, body)


def _moe_down_kernel(te_ref, tr_ref, nr_ref, a_ref, w2_ref, y_ref, *, ts):
    tm, tn = y_ref.shape

    def body(n):
        if n > 0:
            y_ref[0:n, :] = _dot(a_ref[0:n, :], w2_ref[...].astype(BF16))
        if n < tm:
            y_ref[n:tm, :] = jnp.zeros((tm - n, tn), F32)

    _rows_switch(nr_ref[pl.program_id(0)], ts, tm, body)


def _moe_ffn(tile_expert, tile_row, tile_rows, hs, w1, w3, w2, layer):
    r, d = hs.shape
    f = w1.shape[3]
    tm, tf, tn = MOE_TM, 512, 256
    nf, nn = f // tf, d // tn
    prefetch = (tile_expert, tile_row, tile_rows)
    frozen = lambda last: (lambda i, j, nr: jnp.where(nr[i] > 0, j, last))
    jf, jn = frozen(nf - 1), frozen(nn - 1)
    act = pl.pallas_call(
        functools.partial(_moe_up_kernel, ts=MOE_TS),
        grid_spec=pltpu.PrefetchScalarGridSpec(
            num_scalar_prefetch=3,
            grid=(r // tm, nf),
            in_specs=[pl.BlockSpec((tm, d), lambda i, j, te, tr, nr: (tr[i], 0)),
                      pl.BlockSpec((None, None, d, tf), lambda i, j, te, tr, nr: (layer, te[i], 0, jf(i, j, nr))),
                      pl.BlockSpec((None, None, d, tf), lambda i, j, te, tr, nr: (layer, te[i], 0, jf(i, j, nr)))],
            out_specs=pl.BlockSpec((tm, tf), lambda i, j, te, tr, nr: (i, j))),
        out_shape=jax.ShapeDtypeStruct((r, f), BF16),
        compiler_params=_params(("arbitrary", "arbitrary")),
        name="moe_up",
    )(*prefetch, hs, w1, w3)
    return pl.pallas_call(
        functools.partial(_moe_down_kernel, ts=MOE_TS),
        grid_spec=pltpu.PrefetchScalarGridSpec(
            num_scalar_prefetch=3,
            grid=(r // tm, nn),
            in_specs=[pl.BlockSpec((tm, f), lambda i, j, te, tr, nr: (tr[i], 0)),
                      pl.BlockSpec((None, None, f, tn), lambda i, j, te, tr, nr: (layer, te[i], 0, jn(i, j, nr)))],
            out_specs=pl.BlockSpec((tm, tn), lambda i, j, te, tr, nr: (i, j))),
        out_shape=jax.ShapeDtypeStruct((r, d), F32),
        compiler_params=_params(("arbitrary", "arbitrary")),
        name="moe_down",
    )(*prefetch, act, w2)


def _combine_kernel(p1_ref, p2_ref, y_ref, x_ref, r_ref, mod_ref, o_ref, y1_scr, y2_scr, sem, *, tc, tpb, lc):
    i = pl.program_id(0)
    last = pl.num_programs(0) - 1
    slot = i % 2

    def copies(idx1, idx2, r, s):
        return (pltpu.make_async_copy(y_ref.at[pl.ds(idx1, 1)], y1_scr.at[s, pl.ds(r, 1)], sem.at[s]),
                pltpu.make_async_copy(y_ref.at[pl.ds(idx2, 1)], y2_scr.at[s, pl.ds(r, 1)], sem.at[s]))

    def issue(ii, s):
        def body(r, carry):
            c1, c2 = copies(p1_ref[ii * tc + r], p2_ref[ii * tc + r], r, s)
            c1.start()
            c2.start()
            return carry
        lax.fori_loop(0, tc, body, 0, unroll=8)

    @pl.when(i == 0)
    def _():
        issue(0, 0)

    @pl.when(i < last)
    def _():
        issue(jnp.minimum(i + 1, last), 1 - slot)

    def drain(r, carry):
        c1, c2 = copies(0, 0, r, slot)
        c1.wait()
        c2.wait()
        return carry

    lax.fori_loop(0, tc, drain, 0, unroll=8)
    d = x_ref.shape[1]
    gate = _gate_tile(mod_ref, i, tc, tpb, lc, 5, 0, d, d)
    route = r_ref[...]
    f = route[:, 2:3] * y1_scr[slot] + route[:, 3:4] * y2_scr[slot]
    o_ref[...] = x_ref[...] + gate * f


def _combine(pos1, pos2, y, x, route, mod, t, lc):
    m, d = x.shape
    tc = 256
    return pl.pallas_call(
        functools.partial(_combine_kernel, tc=tc, tpb=t // tc, lc=lc),
        grid_spec=pltpu.PrefetchScalarGridSpec(
            num_scalar_prefetch=2,
            grid=(m // tc,),
            in_specs=[pl.BlockSpec(memory_space=pl.ANY),
                      pl.BlockSpec((tc, d), lambda i, p1, p2: (i, 0)),
                      pl.BlockSpec((tc, LANES), lambda i, p1, p2: (i, 0)),
                      pl.BlockSpec(mod.shape, lambda i, p1, p2: (0, 0))],
            out_specs=pl.BlockSpec((tc, d), lambda i, p1, p2: (i, 0)),
            scratch_shapes=[pltpu.VMEM((2, tc, d), F32), pltpu.VMEM((2, tc, d), F32),
                            pltpu.SemaphoreType.DMA((2,))]),
        out_shape=jax.ShapeDtypeStruct((m, d), F32),
        compiler_params=_params(("arbitrary",)),
        name="moe_combine",
    )(pos1, pos2, y, x, route, mod)


def _ffn_moe(x, mod, w_router, w1, w3, w2, layer, t, lc, ctx_out):
    m, d = x.shape
    tm = MOE_TM
    h, route = _router(x, mod, w_router, t, lc)
    experts = route[:, 0:TOP_K].astype(jnp.int32).reshape(-1)
    token = jnp.arange(TOP_K * m, dtype=jnp.int32) // TOP_K
    used = jnp.logical_or(ctx_out, token % t >= lc)
    onehot = ((experts[:, None] == jnp.arange(N_EXPERTS)[None, :]) & used[:, None]).astype(jnp.int32)
    rank = jnp.sum((jnp.cumsum(onehot, axis=0) - onehot) * onehot, axis=1)
    count = jnp.sum(onehot, axis=0)
    tiles = (count + tm - 1) // tm
    tile_end = jnp.cumsum(tiles)
    tile_start = tile_end - tiles
    n_tiles = (TOP_K * m + N_EXPERTS * (tm - 1)) // tm
    n_rows = n_tiles * tm
    pos = (tile_start * tm)[experts] + rank
    src = jnp.full((n_rows,), -1, jnp.int32).at[jnp.where(used, pos, n_rows)].set(token, mode='drop')
    pos = jnp.where(used, pos, token % tm)
    tile_ids = jnp.arange(n_tiles)
    tile_row = jnp.minimum(tile_ids, tile_end[-1] - 1)
    tile_expert = jnp.sum((tile_row[:, None] >= tile_end[None, :]).astype(jnp.int32), axis=1)
    tile_rows = jnp.clip(count[tile_expert] - (tile_ids - tile_start[tile_expert]) * tm, 0, tm)
    tile_rows = jnp.where(tile_ids < tile_end[-1], tile_rows, 0)
    hs = _gather_rows(src, h, n_rows)
    y = _moe_ffn(tile_expert.astype(jnp.int32), tile_row.astype(jnp.int32), tile_rows.astype(jnp.int32),
                 hs, w1, w3, w2, layer)
    pos = pos.reshape(m, TOP_K)
    return _combine(pos[:, 0], pos[:, 1], y, x, route, mod, t, lc)


def _rope_tables(n_lat, lc, rot_dim, lane0, period):
    tok = jnp.arange(n_lat)
    axis_dim = rot_dim // 2
    inv_freq = ROPE_BASE ** (-jnp.arange(0, axis_dim, 2, dtype=F32) / axis_dim)
    ang_r = (tok // GRID_W).astype(F32)[:, None] * inv_freq
    ang_c = (tok % GRID_W).astype(F32)[:, None] * inv_freq
    ang = jnp.concatenate([ang_r, ang_r, ang_c, ang_c], axis=-1)
    cos, sin = jnp.cos(ang), jnp.sin(ang)
    seg = rot_dim // 4
    lane = np.arange(LANES)
    rel = (lane - lane0) % period
    active = (lane >= lane0) & (rel < rot_dim)
    even = ((rel // seg) % 2 == 0)
    idx = np.where(active, rel, 0)
    cos_t = jnp.where(active[None, :], cos[:, idx], 1.0)
    sin_t = jnp.where(active[None, :], sin[:, idx], 0.0)
    sin_next = jnp.where(even[None, :], -sin_t, 0.0)
    sin_prev = jnp.where(even[None, :], 0.0, sin_t)
    ctx = lambda a, v: jnp.concatenate([jnp.full((lc, LANES), v, F32), a], axis=0)
    return ctx(cos_t, 1.0), ctx(sin_next, 0.0), ctx(sin_prev, 0.0)


def _relayout_kernel(w_ref, o_ref):
    rows = w_ref.shape[0]
    seg = lambda lo, n: w_ref[:, lo:lo + n].astype(BF16)
    z = lambda n: jnp.zeros((rows, n), BF16)
    a0, b0, g0, c0, d0 = 0, 1536, 3584, 3600, 4144
    parts = [seg(c0, 384), seg(c0 + 384, 128),
             seg(a0, 1536), seg(b0, 2048), seg(d0, 1536),
             seg(g0, 16), z(KR_LANE - 16), seg(c0 + 512, 32), z(LANES - KR_LANE - 32)]
    o_ref[...] = jnp.concatenate(parts, axis=1)


def _relayout_w_in(w_in):
    depth, d, n = w_in.shape
    tr = 256
    return pl.pallas_call(
        _relayout_kernel,
        grid=(depth, d // tr),
        in_specs=[pl.BlockSpec((None, tr, n), lambda l, i: (l, i, 0))],
        out_specs=pl.BlockSpec((None, tr, P_WIDTH), lambda l, i: (l, i, 0)),
        out_shape=jax.ShapeDtypeStruct((depth, d, P_WIDTH), BF16),
        compiler_params=_params(("arbitrary", "arbitrary")),
        name="w_in_relayout",
    )(w_in)


def _mla_weights(w_uq, w_ukv):
    hd = MLA_NOPE + MLA_ROPE
    wq = jnp.pad(w_uq.reshape(MLA_Q_RANK, MLA_HEADS, hd), ((0, 0), (0, 0), (0, LANES - hd)))
    wkv = w_ukv.reshape(MLA_KV_RANK, MLA_HEADS, MLA_NOPE + MLA_V)
    wk = jnp.pad(wkv[:, :, :MLA_NOPE], ((0, 0), (0, 0), (0, LANES - MLA_NOPE)))
    wv = wkv[:, :, MLA_NOPE:]
    return (wq.reshape(MLA_Q_RANK, -1).astype(BF16), wk.reshape(MLA_KV_RANK, -1).astype(BF16),
            wv.reshape(MLA_KV_RANK, -1).astype(BF16))


def kernel(x, c, ctx, c_ctx, w_ada, b_ada, w_in, w_out, da_q_gain, da_k_gain, da_lambda, da_out_gain, ml_conv_w, ml_conv_b, ml_gate_b, ml_out_gain, mla_cq_gain, mla_ckv_gain, mla_w_uq, mla_w_ukv, mla_q_gain, mla_k_gain, na_q_gain, na_k_gain, na_rpb, ffn_w1, ffn_w3, ffn_w2, moe_router, moe_w1, moe_w3, moe_w2):
    b, n_lat, d = x.shape
    lc = ctx.shape[1]
    t = lc + n_lat
    depth = w_in.shape[0]
    assert b <= 4 and lc == NA_QR * GRID_W and lc == TQ_DIFF == TQ_MLA and n_lat % lc == 0 and t % 768 == 0

    xs = jnp.concatenate([ctx, x], axis=1).reshape(b * t, d)
    cond = jnp.zeros((8, d), F32).at[:b].set(c).at[4].set(c_ctx)
    mod = _mod_table(cond, w_ada, b_ada)
    rope_da = _rope_tables(n_lat, lc, DA_QK, 0, DA_QK)
    rope_mla = _rope_tables(n_lat, lc, MLA_ROPE, KR_LANE, LANES)
    w_in_r = _relayout_w_in(w_in)

    for l in range(depth):
        lam_init = 0.8 - 0.6 * math.exp(-0.3 * l)
        p = _inproj(xs, mod[l], w_in_r[l], t, lc)
        mix_a = _mixer_diff(p, rope_da, da_q_gain[l], da_k_gain[l], da_lambda[l], da_out_gain[l], lam_init, b, t, lc)
        mix_b = _mixer_mlstm(p, ml_conv_w[l], ml_conv_b[l], ml_gate_b[l], ml_out_gain[l], b, t, lc)
        wq, wk, wv = _mla_weights(mla_w_uq[l], mla_w_ukv[l])
        qc, kc, vc = _mla_prep(p, rope_mla, mla_cq_gain[l], mla_ckv_gain[l], wq, wk, wv,
                               mla_q_gain[l], mla_k_gain[l], t)
        mix_c = _mla_attn(qc, kc, vc, b, t, lc)
        mix_d = _mixer_na(p, na_q_gain[l], na_k_gain[l], na_rpb[l], b, t, lc)
        xs = _outproj((mix_a, mix_b, mix_c, mix_d), w_out[l].astype(BF16), xs, mod[l], t, lc)
        if l % 2 == 0:
            xs = _ffn_dense(xs, mod[l], ffn_w1[l // 2], ffn_w3[l // 2], ffn_w2[l // 2], t, lc)
        else:
            xs = _ffn_moe(xs, mod[l], moe_router[l // 2], moe_w1, moe_w3, moe_w2, l // 2, t, lc, l < depth - 1)
    return xs.reshape(b, t, d)[:, lc:]
```

```python
import functools
import math

import numpy as np
import jax
import jax.numpy as jnp
from jax import lax
from jax.experimental import pallas as pl
from jax.experimental.pallas import tpu as pltpu

F32 = jnp.float32
BF16 = jnp.bfloat16

GRID_W = 64
DA_HEADS, DA_QK, DA_V = 4, 64, 128
ML_HEADS, ML_DIM, ML_CONV = 4, 128, 3
MLA_HEADS, MLA_NOPE, MLA_ROPE, MLA_V = 8, 64, 32, 64
MLA_Q_RANK, MLA_KV_RANK = 384, 128
NA_HEADS, NA_DIM, NA_WIN_R, NA_WIN_C = 8, 64, 8, 16
N_EXPERTS, TOP_K = 8, 2
ROPE_BASE = 10000.0
RMS_EPS = 1e-6
LOG2E = math.log2(math.e)

LANES = 128
VMEM_LIMIT = 56 * 1024 * 1024

P_CQ, P_CKV = 0, 384
P_AQ, P_AK, P_AV = 512, 1024, 1536
P_BQ, P_BK, P_BV, P_BO = 2048, 2560, 3072, 3584
P_DQ, P_DK, P_DV = 4096, 4608, 5120
P_LAST = 5632
P_WIDTH = 5760
KR_LANE = 64

TQ_DIFF, TQ_MLA = 256, 256
ML_L = 256
NA_QR, NA_KR = 4, 12
MOE_TM, MOE_TS = 1280, 256


def _params(sem):
    return pltpu.CompilerParams(dimension_semantics=sem, vmem_limit_bytes=VMEM_LIMIT)


def _silu(x):
    return x * (1.0 / (1.0 + jnp.exp(-x)))


def _sigmoid(x):
    return 1.0 / (1.0 + jnp.exp(-x))


def _rms(x, n=None):
    n = x.shape[-1] if n is None else n
    return x * lax.rsqrt(jnp.sum(x * x, axis=-1, keepdims=True) * (1.0 / n) + RMS_EPS)


def _rms_halves(x, gain):
    lo = lax.broadcasted_iota(jnp.int32, x.shape, 1) < 64
    x2 = x * x
    s_lo = jnp.sum(jnp.where(lo, x2, 0.0), axis=-1, keepdims=True)
    s_hi = jnp.sum(jnp.where(lo, 0.0, x2), axis=-1, keepdims=True)
    ms = jnp.where(lo, s_lo, s_hi) * (1.0 / 64)
    return x * lax.rsqrt(ms + RMS_EPS) * gain


def _rope(x, cos, sin_next, sin_prev, seg):
    return x * cos + pltpu.roll(x, LANES - seg, 1) * sin_next + pltpu.roll(x, seg, 1) * sin_prev


def _softmax2_pv(s2, v):
    e = jnp.exp2(s2 - jnp.max(s2, axis=-1, keepdims=True))
    return _dot(e.astype(BF16), v) * (1.0 / jnp.sum(e, axis=-1, keepdims=True))


def _dot(a, b):
    return jnp.dot(a, b, preferred_element_type=F32)


def _dot_nt(a, b):
    return lax.dot_general(a, b, (((1,), (1,)), ((), ())), preferred_element_type=F32)


def _dot_tn(a, b):
    return lax.dot_general(a, b, (((0,), (0,)), ((), ())), preferred_element_type=F32)


def _split3(a):
    a1 = a.astype(BF16)
    r = a - a1.astype(F32)
    a2 = r.astype(BF16)
    a3 = (r - a2.astype(F32)).astype(BF16)
    return a1, a2, a3


def _dot_f32(a, b):
    a1, a2, a3 = _split3(a)
    b1, b2, b3 = _split3(b)
    return (_dot(a1, b1) + (_dot(a1, b2) + _dot(a2, b1))
            + (_dot(a1, b3) + _dot(a2, b2) + _dot(a3, b1)))


def _adaln_tile(x, mod_ref, i, tm, tpb, lc, c_shift, c_scale):
    d = x.shape[1]
    b = i // tpb
    row = (i % tpb) * tm + lax.broadcasted_iota(jnp.int32, (tm, 1), 0)
    is_ctx = row < lc
    shift = jnp.where(is_ctx, mod_ref[4:5, c_shift * d:(c_shift + 1) * d],
                      mod_ref[pl.ds(b, 1), c_shift * d:(c_shift + 1) * d])
    scale = jnp.where(is_ctx, mod_ref[4:5, c_scale * d:(c_scale + 1) * d],
                      mod_ref[pl.ds(b, 1), c_scale * d:(c_scale + 1) * d])
    return _rms(x) * (1.0 + scale) + shift


def _gate_tile(mod_ref, i, tm, tpb, lc, c_gate, col0, width, d):
    b = i // tpb
    row = (i % tpb) * tm + lax.broadcasted_iota(jnp.int32, (tm, 1), 0)
    lo = c_gate * d + col0
    return jnp.where(row < lc, mod_ref[4:5, lo:lo + width], mod_ref[pl.ds(b, 1), lo:lo + width])


def _mod_kernel(c_ref, w_ref, b_ref, o_ref):
    s = _silu(c_ref[...]).astype(BF16)
    o_ref[...] = _dot(s, w_ref[...].astype(BF16)) + b_ref[...]


def _mod_table(cond, w_ada, b_ada):
    depth, d, n = w_ada.shape
    tn = 1024
    return pl.pallas_call(
        _mod_kernel,
        grid=(depth, n // tn),
        in_specs=[pl.BlockSpec((8, d), lambda l, j: (0, 0)),
                  pl.BlockSpec((None, d, tn), lambda l, j: (l, 0, j)),
                  pl.BlockSpec((None, 1, tn), lambda l, j: (l, 0, j))],
        out_specs=pl.BlockSpec((None, 8, tn), lambda l, j: (l, 0, j)),
        out_shape=jax.ShapeDtypeStruct((depth, 8, n), F32),
        compiler_params=_params(("arbitrary", "arbitrary")),
        name="mod_table",
    )(cond, w_ada, b_ada.reshape(depth, 1, n))


def _inproj_kernel(x_ref, mod_ref, w_ref, o_ref, h_scr, *, tm, tpb, lc):
    i = pl.program_id(0)

    @pl.when(pl.program_id(1) == 0)
    def _():
        h_scr[...] = _adaln_tile(x_ref[...], mod_ref, i, tm, tpb, lc, 0, 1).astype(BF16)

    o_ref[...] = _dot(h_scr[...], w_ref[...])


def _inproj(x, mod, w, layer, t, lc):
    m, d = x.shape
    n = w.shape[2]
    tm, tn = 768, 1152
    return pl.pallas_call(
        functools.partial(_inproj_kernel, tm=tm, tpb=t // tm, lc=lc),
        grid=(m // tm, n // tn),
        in_specs=[pl.BlockSpec((tm, d), lambda i, j: (i, 0)),
                  pl.BlockSpec(mod.shape, lambda i, j: (0, 0)),
                  pl.BlockSpec((None, d, tn), lambda i, j: (layer, 0, j))],
        out_specs=pl.BlockSpec((tm, tn), lambda i, j: (i, j)),
        out_shape=jax.ShapeDtypeStruct((m, n), F32),
        scratch_shapes=[pltpu.VMEM((tm, d), BF16)],
        compiler_params=_params(("arbitrary", "arbitrary")),
        name="inproj",
    )(x, mod, w)


def _diff_attn_kernel(q_ref, k_ref, v_ref, cos_ref, sn_ref, sp_ref, qg_ref, kg_ref, og_ref, lam_ref,
                      o_ref, kn_scr, vb_scr, *, tq, lc, lam_init, scale):
    qi = pl.program_id(2)
    t = k_ref.shape[0]

    @pl.when(qi == 0)
    def _():
        k = _rms_halves(k_ref[...], kg_ref[...])
        kn_scr[...] = _rope(k, cos_ref[...], sn_ref[...], sp_ref[...], 16).astype(BF16)
        vb_scr[...] = v_ref[...].astype(BF16)

    r0 = pl.multiple_of(qi * tq, tq)
    q = _rms_halves(q_ref[...], qg_ref[...])
    q = _rope(q, cos_ref[pl.ds(r0, tq), :], sn_ref[pl.ds(r0, tq), :], sp_ref[pl.ds(r0, tq), :], 16) * scale
    lo = lax.broadcasted_iota(jnp.int32, q.shape, 1) < 64
    q1 = jnp.where(lo, q, 0.0).astype(BF16)
    q2 = jnp.where(lo, 0.0, q).astype(BF16)
    lp = lam_ref[...]
    lam = (jnp.exp(jnp.sum(lp[0:1] * lp[1:2], axis=-1, keepdims=True))
           - jnp.exp(jnp.sum(lp[2:3] * lp[3:4], axis=-1, keepdims=True)) + lam_init)

    def attend(nk):
        kk = kn_scr[0:nk, :]
        vv = vb_scr[0:nk, :]
        o = _softmax2_pv(_dot_nt(q1, kk), vv) - lam * _softmax2_pv(_dot_nt(q2, kk), vv)
        o_ref[...] = (_rms(o) * og_ref[...] * (1.0 - lam_init)).astype(BF16)

    @pl.when(qi * tq < lc)
    def _():
        attend(lc)

    @pl.when(qi * tq >= lc)
    def _():
        attend(t)


def _mixer_diff(p, tabs, q_gain, k_gain, lam_params, out_gain, lam_init, b, t, lc):
    m = p.shape[0]
    tq = TQ_DIFF
    nq = t // tq
    cos, sn, sp = tabs
    tile2 = lambda g: jnp.tile(g, 2).reshape(1, LANES)
    full = lambda a: pl.BlockSpec(a.shape, lambda bb, h, qi: (0,) * a.ndim)
    args = (cos, sn, sp, tile2(q_gain), tile2(k_gain), out_gain.reshape(1, LANES), lam_params)
    return pl.pallas_call(
        functools.partial(_diff_attn_kernel, tq=tq, lc=lc, lam_init=lam_init, scale=DA_QK ** -0.5 * LOG2E),
        grid=(b, DA_HEADS, nq),
        in_specs=[pl.BlockSpec((tq, LANES), lambda bb, h, qi: (bb * nq + qi, P_AQ // LANES + h)),
                  pl.BlockSpec((t, LANES), lambda bb, h, qi: (bb, P_AK // LANES + h)),
                  pl.BlockSpec((t, LANES), lambda bb, h, qi: (bb, P_AV // LANES + h))]
                 + [full(a) for a in args],
        out_specs=pl.BlockSpec((tq, LANES), lambda bb, h, qi: (bb * nq + qi, h)),
        out_shape=jax.ShapeDtypeStruct((m, DA_HEADS * DA_V), BF16),
        scratch_shapes=[pltpu.VMEM((t, LANES), BF16), pltpu.VMEM((t, LANES), BF16)],
        compiler_params=_params(("arbitrary", "arbitrary", "arbitrary")),
        name="diff_attn",
    )(p, p, p, *args)


def _mla_prep_kernel(cq_ref, ckv_ref, last_ref, cos_ref, sn_ref, sp_ref, cqg_ref, ckvg_ref, wq_ref, wk_ref,
                     wv_ref, qg_ref, kg_ref, q_out, k_out, v_out, *, scale):
    hd = MLA_NOPE + MLA_ROPE
    cos, sn, sp = cos_ref[...], sn_ref[...], sp_ref[...]
    cqn = (_rms(cq_ref[...]) * cqg_ref[...]).astype(BF16)
    ckvn = (_rms(ckv_ref[...]) * ckvg_ref[...]).astype(BF16)
    q = _dot(cqn, wq_ref[...])
    kk = _dot(ckvn, wk_ref[...])
    v_out[...] = _dot(ckvn, wv_ref[...]).astype(BF16)
    last = last_ref[...]
    lane = lax.broadcasted_iota(jnp.int32, last.shape, 1)
    kr = jnp.where((lane >= KR_LANE) & (lane < KR_LANE + MLA_ROPE), last, 0.0)
    for h in range(MLA_HEADS):
        sl = slice(h * LANES, (h + 1) * LANES)
        qh = _rms(q[:, sl], hd) * qg_ref[...]
        q_out[:, sl] = (_rope(qh, cos, sn, sp, 8) * scale).astype(BF16)
        kh = _rms(kk[:, sl] + kr, hd) * kg_ref[...]
        k_out[:, sl] = _rope(kh, cos, sn, sp, 8).astype(BF16)


def _mla_prep(p, tabs, cq_gain, ckv_gain, wq, wk, wv, q_gain, k_gain, t):
    m = p.shape[0]
    tm = 768
    tpb = t // tm
    hw = MLA_HEADS * LANES
    cos, sn, sp = tabs
    tab = pl.BlockSpec((tm, LANES), lambda i: (i % tpb, 0))
    full = lambda a: pl.BlockSpec(a.shape, lambda i: (0,) * a.ndim)
    pad = lambda g: jnp.pad(g, (0, LANES - g.shape[0])).reshape(1, LANES)
    consts = (cq_gain.reshape(1, -1), ckv_gain.reshape(1, -1), wq, wk, wv, pad(q_gain), pad(k_gain))
    return pl.pallas_call(
        functools.partial(_mla_prep_kernel, scale=(MLA_NOPE + MLA_ROPE) ** -0.5 * LOG2E),
        grid=(m // tm,),
        in_specs=[pl.BlockSpec((tm, MLA_Q_RANK), lambda i: (i, P_CQ // MLA_Q_RANK)),
                  pl.BlockSpec((tm, LANES), lambda i: (i, P_CKV // LANES)),
                  pl.BlockSpec((tm, LANES), lambda i: (i, P_LAST // LANES)),
                  tab, tab, tab] + [full(a) for a in consts],
        out_specs=[pl.BlockSpec((tm, hw), lambda i: (i, 0)),
                   pl.BlockSpec((tm, hw), lambda i: (i, 0)),
                   pl.BlockSpec((tm, MLA_HEADS * MLA_V), lambda i: (i, 0))],
        out_shape=[jax.ShapeDtypeStruct((m, hw), BF16), jax.ShapeDtypeStruct((m, hw), BF16),
                   jax.ShapeDtypeStruct((m, MLA_HEADS * MLA_V), BF16)],
        compiler_params=_params(("arbitrary",)),
        name="mla_prep",
    )(p, p, p, cos, sn, sp, *consts)


def _mla_attn_kernel(q_ref, k_ref, v_ref, o_ref, *, tq, lc):
    qi = pl.program_id(2)
    t = k_ref.shape[0]

    def attend(nk):
        vv = v_ref[0:nk, :]
        outs = []
        for h in range(2):
            sl = slice(h * LANES, (h + 1) * LANES)
            outs.append(_softmax2_pv(_dot_nt(q_ref[:, sl], k_ref[0:nk, sl]), vv))
        lo = lax.broadcasted_iota(jnp.int32, outs[0].shape, 1) < MLA_V
        o_ref[...] = jnp.where(lo, outs[0], outs[1]).astype(BF16)

    @pl.when(qi * tq < lc)
    def _():
        attend(lc)

    @pl.when(qi * tq >= lc)
    def _():
        attend(t)


def _mla_attn(q, k, v, b, t, lc):
    m = q.shape[0]
    tq = TQ_MLA
    nq = t // tq
    return pl.pallas_call(
        functools.partial(_mla_attn_kernel, tq=tq, lc=lc),
        grid=(b, MLA_HEADS // 2, nq),
        in_specs=[pl.BlockSpec((tq, 2 * LANES), lambda bb, hp, qi: (bb * nq + qi, hp)),
                  pl.BlockSpec((t, 2 * LANES), lambda bb, hp, qi: (bb, hp)),
                  pl.BlockSpec((t, LANES), lambda bb, hp, qi: (bb, hp))],
        out_specs=pl.BlockSpec((tq, LANES), lambda bb, hp, qi: (bb * nq + qi, hp)),
        out_shape=jax.ShapeDtypeStruct((m, MLA_HEADS * MLA_V), BF16),
        compiler_params=_params(("arbitrary", "arbitrary", "arbitrary")),
        name="mla_attn",
    )(q, k, v)


def _na_kernel(q_ref, k_ref, v_ref, bias_ref, qg_ref, kg_ref, o_ref, kn_scr, vb_scr, *, lc, rows, scale):
    s = pl.program_id(1)
    tq = q_ref.shape[0]
    n_ctx_blk = lc // tq
    band = NA_KR * GRID_W
    npair = NA_HEADS // 2

    @pl.when(s == 0)
    def _():
        for pp in range(npair):
            sl = slice(pp * LANES, (pp + 1) * LANES)
            kn_scr[:, sl] = _rms_halves(k_ref[:, sl], kg_ref[:, sl]).astype(BF16)
        vb_scr[...] = v_ref[...].astype(BF16)

    def heads(pv_fn):
        for pp in range(npair):
            sl = slice(pp * LANES, (pp + 1) * LANES)
            q = _rms_halves(q_ref[:, sl], qg_ref[:, sl]) * scale
            lo = lax.broadcasted_iota(jnp.int32, q.shape, 1) < NA_DIM
            qs = jnp.concatenate([jnp.where(lo, q, 0.0), jnp.where(lo, 0.0, q)], axis=0).astype(BF16)
            o2 = pv_fn(qs, sl, pp)
            o_ref[:, sl] = jnp.where(lo, o2[0:tq], o2[tq:2 * tq]).astype(BF16)

    @pl.when(s < n_ctx_blk)
    def _():
        def f(qs, sl, pp):
            return _softmax2_pv(_dot_nt(qs, kn_scr[0:lc, sl]), vb_scr[0:lc, sl])
        heads(f)

    @pl.when(s >= n_ctx_blk)
    def _():
        rb = s - n_ctx_blk
        krow = jnp.clip(rb * NA_QR - NA_WIN_R // 2, 0, rows - NA_KR)
        k0 = pl.multiple_of(lc + krow * GRID_W, GRID_W)

        def f(qs, sl, pp):
            s_loc = _dot_nt(qs, kn_scr[pl.ds(k0, band), sl]) + bias_ref[pp]
            s_ctx = _dot_nt(qs, kn_scr[0:lc, sl])
            mx = jnp.maximum(jnp.max(s_loc, axis=-1, keepdims=True), jnp.max(s_ctx, axis=-1, keepdims=True))
            e_loc = jnp.exp2(s_loc - mx)
            e_ctx = jnp.exp2(s_ctx - mx)
            den = jnp.sum(e_loc, axis=-1, keepdims=True) + jnp.sum(e_ctx, axis=-1, keepdims=True)
            num = _dot(e_loc.astype(BF16), vb_scr[pl.ds(k0, band), sl]) + _dot(e_ctx.astype(BF16), vb_scr[0:lc, sl])
            return num * (1.0 / den)
        heads(f)


def _na_variant(rb, n_rb):
    return jnp.where(rb == 0, 0, jnp.where(rb == n_rb - 1, 2, 1))


def _na_bias_table(rpb, rows):
    n_rb = rows // NA_QR
    rbs = np.array([0, 1, n_rb - 1])
    q_rows = rbs[:, None] * NA_QR + np.arange(NA_QR)
    key_rows = np.clip(rbs * NA_QR - NA_WIN_R // 2, 0, rows - NA_KR)[:, None] + np.arange(NA_KR)
    cols = np.arange(GRID_W)
    r0 = np.clip(q_rows - NA_WIN_R // 2, 0, rows - NA_WIN_R)[:, :, None]
    c0 = np.clip(cols - NA_WIN_C // 2, 0, GRID_W - NA_WIN_C)[:, None]
    kr = key_rows[:, None, :]
    ok_r = (kr >= r0) & (kr < r0 + NA_WIN_R)
    ok_c = (cols[None, :] >= c0) & (cols[None, :] < c0 + NA_WIN_C)
    rel_r = np.clip(kr - q_rows[:, :, None] + NA_WIN_R - 1, 0, 2 * NA_WIN_R - 2)
    rel_c = np.clip(cols[None, :] - cols[:, None] + NA_WIN_C - 1, 0, 2 * NA_WIN_C - 2)
    oh_c = jnp.asarray(np.eye(2 * NA_WIN_C - 1, dtype=np.float32)[rel_c])
    rpb2 = rpb.reshape(NA_HEADS // 2, 2, 2 * NA_WIN_R - 1, 2 * NA_WIN_C - 1)
    tiles = jnp.einsum('phde,qce->phdqc', rpb2, oh_c, precision=lax.Precision.HIGHEST)
    tiles = jnp.where(jnp.asarray(ok_c), tiles * LOG2E, -jnp.inf)
    npair, nv, tq, band = NA_HEADS // 2, len(rbs), NA_QR * GRID_W, NA_KR * GRID_W
    return pl.pallas_call(
        _na_bias_kernel,
        grid_spec=pltpu.PrefetchScalarGridSpec(
            num_scalar_prefetch=2,
            grid=(npair, nv),
            in_specs=[pl.BlockSpec((None,) + tiles.shape[1:], lambda p, v, rel, ok: (p, 0, 0, 0, 0))],
            out_specs=pl.BlockSpec((None, None, 2 * tq, band), lambda p, v, rel, ok: (p, v, 0, 0))),
        out_shape=jax.ShapeDtypeStruct((npair, nv, 2 * tq, band), F32),
        compiler_params=_params(("arbitrary", "arbitrary")),
        name="na_bias",
    )(jnp.asarray(rel_r.reshape(-1), jnp.int32), jnp.asarray(ok_r.reshape(-1), jnp.int32), tiles)


def _na_bias_kernel(rel_ref, ok_ref, t_ref, o_ref):
    v = pl.program_id(1)
    for hh in range(2):
        for i in range(NA_QR):
            rows = slice((hh * NA_QR + i) * GRID_W, (hh * NA_QR + i + 1) * GRID_W)
            for k in range(0, NA_KR, 2):
                pair = []
                for kk in (k, k + 1):
                    idx = (v * NA_QR + i) * NA_KR + kk
                    pair.append(jnp.where(ok_ref[idx] > 0, t_ref[hh, rel_ref[idx]], -jnp.inf))
                o_ref[rows, k * GRID_W:(k + 2) * GRID_W] = jnp.concatenate(pair, axis=1)


def _mixer_na(p, q_gain, k_gain, rpb, b, t, lc):
    m = p.shape[0]
    rows = (t - lc) // GRID_W
    tq = NA_QR * GRID_W
    nblk = t // tq
    n_ctx_blk = lc // tq
    bias = _na_bias_table(rpb, rows)
    tile8 = lambda g: jnp.tile(g, NA_HEADS).reshape(1, NA_HEADS * NA_DIM)
    w = NA_HEADS * NA_DIM
    return pl.pallas_call(
        functools.partial(_na_kernel, lc=lc, rows=rows, scale=NA_DIM ** -0.5 * LOG2E),
        grid=(b, nblk),
        in_specs=[pl.BlockSpec((tq, w), lambda bb, s: (bb * nblk + s, P_DQ // w)),
                  pl.BlockSpec((t, w), lambda bb, s: (bb, P_DK // w)),
                  pl.BlockSpec((t, w), lambda bb, s: (bb, P_DV // w)),
                  pl.BlockSpec((NA_HEADS // 2, None, 2 * tq, NA_KR * GRID_W),
                               lambda bb, s: (0, _na_variant(jnp.maximum(s - n_ctx_blk, 0), rows // NA_QR), 0, 0)),
                  pl.BlockSpec((1, w), lambda bb, s: (0, 0)),
                  pl.BlockSpec((1, w), lambda bb, s: (0, 0))],
        out_specs=pl.BlockSpec((tq, w), lambda bb, s: (bb * nblk + s, 0)),
        out_shape=jax.ShapeDtypeStruct((m, w), BF16),
        scratch_shapes=[pltpu.VMEM((t, w), BF16), pltpu.VMEM((t, w), BF16)],
        compiler_params=_params(("arbitrary", "arbitrary")),
        name="na_attn",
    )(p, p, p, bias, tile8(q_gain), tile8(k_gain))


def _ml_prep_kernel(x_ref, w_ref, b_ref, o_ref, *, lc):
    t = x_ref.shape[0]
    x = x_ref[...]
    row = lax.broadcasted_iota(jnp.int32, (t, 1), 0)
    prev = jnp.where((row == 0) | (row == lc), 0.0, pltpu.roll(x, 1, 0))
    nxt = jnp.where((row == lc - 1) | (row == t - 1), 0.0, pltpu.roll(x, t - 1, 0))
    y = _silu(prev * w_ref[0:1, :] + x * w_ref[1:2, :] + nxt * w_ref[2:3, :] + b_ref[...])
    o_ref[...] = y * jnp.where(pl.program_id(1) == 1, ML_DIM ** -0.5, 1.0)


def _ml_prep(p, conv_w, conv_b, b, t, lc):
    m = p.shape[0]
    w = ML_HEADS * ML_DIM
    return pl.pallas_call(
        functools.partial(_ml_prep_kernel, lc=lc),
        grid=(b, 2),
        in_specs=[pl.BlockSpec((t, w), lambda bb, c: (bb, P_BQ // w + c)),
                  pl.BlockSpec((ML_CONV, w), lambda bb, c: (0, c)),
                  pl.BlockSpec((1, w), lambda bb, c: (0, c))],
        out_specs=pl.BlockSpec((t, w), lambda bb, c: (bb, c)),
        out_shape=jax.ShapeDtypeStruct((m, 2 * w), F32),
        compiler_params=_params(("arbitrary", "arbitrary")),
        name="mlstm_prep",
    )(p, conv_w, conv_b.reshape(1, -1))


def _ml_scan_kernel(qf_ref, kf_ref, vf_ref, gf_ref, qb_ref, kb_ref, vb_ref, gb_ref, of_ref, ob_ref,
                    c_scr, n_scr, m_scr):
    @pl.when(pl.program_id(1) == 0)
    def _():
        c_scr[...] = jnp.zeros_like(c_scr)
        n_scr[...] = jnp.zeros_like(n_scr)
        m_scr[...] = jnp.zeros_like(m_scr)

    L = qf_ref.shape[0]
    tt = lax.broadcasted_iota(jnp.int32, (L, L), 0)
    ss = lax.broadcasted_iota(jnp.int32, (L, L), 1)
    eye = tt == ss
    dirs = ((qf_ref, kf_ref, vf_ref, gf_ref, of_ref, ss <= tt, tt <= ss),
            (qb_ref, kb_ref, vb_ref, gb_ref, ob_ref, ss >= tt, tt >= ss))
    for d, (q_ref, k_ref, v_ref, g_ref, o_ref, cm, cm_t) in enumerate(dirs):
        for h in range(ML_HEADS):
            sl = slice(h * ML_DIM, (h + 1) * ML_DIM)
            _ml_chain(q_ref[:, sl], k_ref[:, sl], v_ref[:, sl], g_ref[h], o_ref.at[:, sl],
                      c_scr.at[d, h], n_scr.at[d, h], m_scr.at[d, h], eye, cm, cm_t)


def _ml_chain(q, k, v, g, o_ref, c_scr, n_scr, m_scr, eye, cm, cm_t):
    i_row = g[0:1, :]
    f_row = g[1:2, :]
    lf_row = jnp.minimum(f_row, 0.0) - jnp.log(1.0 + jnp.exp(-jnp.abs(f_row)))
    to_col = lambda r: jnp.sum(jnp.where(eye, r, 0.0), axis=-1, keepdims=True)
    lf_col = to_col(lf_row)
    i_col = to_col(i_row)
    b_col = jnp.sum(jnp.where(cm, lf_row, 0.0), axis=-1, keepdims=True)
    b_row = jnp.sum(jnp.where(cm_t, lf_col, 0.0), axis=0, keepdims=True)
    b_end = jnp.sum(lf_row, axis=-1, keepdims=True)
    m_old = m_scr[...]
    c_old = c_scr[...]
    n_old = n_scr[...]

    log_w = jnp.where(cm, b_col - b_row + i_row, -jnp.inf)
    inter = b_col + m_old
    m_t = jnp.maximum(inter, jnp.max(log_w, axis=-1, keepdims=True))
    qb = q.astype(BF16)
    kb = k.astype(BF16)
    vb = v.astype(BF16)
    s = _dot_nt(qb, kb) * jnp.exp(log_w - m_t)
    a = jnp.exp(inter - m_t)
    num = a * _dot(qb, c_old.astype(BF16)) + _dot(s.astype(BF16), vb)
    den = a * jnp.sum(q * n_old, axis=-1, keepdims=True) + jnp.sum(s, axis=-1, keepdims=True)
    o_ref[...] = num / jnp.maximum(jnp.abs(den), jnp.exp(-m_t))

    w_end = b_end - b_col + i_col
    m_new = jnp.maximum(b_end + m_old, jnp.max(w_end, axis=0, keepdims=True))
    decay = jnp.exp(b_end + m_old - m_new)
    we = jnp.exp(w_end - m_new)
    c_scr[...] = decay * c_old + _dot_tn(kb, (we * v).astype(BF16))
    n_scr[...] = decay * n_old + jnp.sum(we * k, axis=0, keepdims=True)
    m_scr[...] = m_new


def _ml_scan(qk, p, gates, b, t, lc):
    m = p.shape[0]
    L = ML_L
    nct, ncc = t // L, lc // L
    w = ML_HEADS * ML_DIM

    fwd = lambda j: j
    bwd = lambda j: jnp.where(j < ncc, ncc - 1 - j, nct - 1 - (j - ncc))

    def specs(chunk, d):
        blk = lambda col: pl.BlockSpec((L, w), lambda bb, j: (bb * nct + chunk(j), col))
        return [blk(0), blk(1), blk(P_BV // w),
                pl.BlockSpec((None, None, None, ML_HEADS, 2, L), lambda bb, j: (bb, d, chunk(j), 0, 0, 0))]

    out = lambda chunk: pl.BlockSpec((L, w), lambda bb, j: (bb * nct + chunk(j), 0))
    nh = ML_HEADS
    return pl.pallas_call(
        _ml_scan_kernel,
        grid=(b, nct),
        in_specs=specs(fwd, 0) + specs(bwd, 1),
        out_specs=[out(fwd), out(bwd)],
        out_shape=[jax.ShapeDtypeStruct((m, w), F32)] * 2,
        scratch_shapes=[pltpu.VMEM((2, nh, ML_DIM, ML_DIM), F32), pltpu.VMEM((2, nh, 1, ML_DIM), F32),
                        pltpu.VMEM((2, nh, 1, 1), F32)],
        compiler_params=_params(("arbitrary", "arbitrary")),
        name="mlstm_scan",
    )(qk, qk, p, gates, qk, qk, p, gates)


def _ml_finish_kernel(hf_ref, hb_ref, o_ref, g_ref, out_ref):
    hsum = hf_ref[...] + hb_ref[...]
    og = _sigmoid(o_ref[...])
    for h in range(ML_HEADS):
        sl = slice(h * ML_DIM, (h + 1) * ML_DIM)
        out_ref[:, sl] = (_rms(hsum[:, sl]) * g_ref[...] * og[:, sl]).astype(BF16)


def _ml_finish(hf, hb, p, out_gain):
    m = p.shape[0]
    w = ML_HEADS * ML_DIM
    tm = 768
    return pl.pallas_call(
        _ml_finish_kernel,
        grid=(m // tm,),
        in_specs=[pl.BlockSpec((tm, w), lambda i: (i, 0)),
                  pl.BlockSpec((tm, w), lambda i: (i, 0)),
                  pl.BlockSpec((tm, w), lambda i: (i, P_BO // w)),
                  pl.BlockSpec((1, ML_DIM), lambda i: (0, 0))],
        out_specs=pl.BlockSpec((tm, w), lambda i: (i, 0)),
        out_shape=jax.ShapeDtypeStruct((m, w), BF16),
        compiler_params=_params(("arbitrary",)),
        name="mlstm_finish",
    )(hf, hb, p, out_gain.reshape(1, ML_DIM))


def _mixer_mlstm(p, conv_w, conv_b, gate_b, out_gain, b, t, lc):
    qk = _ml_prep(p, conv_w, conv_b, b, t, lc)
    nct = t // ML_L
    g = p[:, P_LAST:P_LAST + 4 * ML_HEADS].reshape(b, nct, ML_L, 2, 2, ML_HEADS) + gate_b
    gates = g.transpose(0, 3, 1, 5, 4, 2)
    hf, hb = _ml_scan(qk, p, gates, b, t, lc)
    return _ml_finish(hf, hb, p, out_gain)


def _outproj_kernel(a_ref, b_ref, c_ref, d_ref, w_ref, x_ref, mod_ref, o_ref, *, tm, tpb, lc, tn, d):
    i = pl.program_id(0)
    j = pl.program_id(1)
    kw = a_ref.shape[1]
    y = _dot(a_ref[...], w_ref[0:kw, :])
    y += _dot(b_ref[...], w_ref[kw:2 * kw, :])
    y += _dot(c_ref[...], w_ref[2 * kw:3 * kw, :])
    y += _dot(d_ref[...], w_ref[3 * kw:4 * kw, :])
    row = (i % tpb) * tm + lax.broadcasted_iota(jnp.int32, (tm, 1), 0)
    b = i // tpb
    col = pl.multiple_of(2 * d + j * tn, LANES)
    gate = jnp.where(row < lc, mod_ref[4:5, pl.ds(col, tn)], mod_ref[pl.ds(b, 1), pl.ds(col, tn)])
    o_ref[...] = x_ref[...] + gate * y


def _outproj(mixes, w, layer, x, mod, t, lc):
    m, d = x.shape
    tm, tn = 768, min(1024, d)
    kw = mixes[0].shape[1]
    mix_spec = pl.BlockSpec((tm, kw), lambda i, j: (i, 0))
    return pl.pallas_call(
        functools.partial(_outproj_kernel, tm=tm, tpb=t // tm, lc=lc, tn=tn, d=d),
        grid=(m // tm, d // tn),
        in_specs=[mix_spec] * 4 + [pl.BlockSpec((None, 4 * kw, tn), lambda i, j: (layer, 0, j)),
                                   pl.BlockSpec((tm, tn), lambda i, j: (i, j)),
                                   pl.BlockSpec(mod.shape, lambda i, j: (0, 0))],
        out_specs=pl.BlockSpec((tm, tn), lambda i, j: (i, j)),
        out_shape=jax.ShapeDtypeStruct((m, d), F32),
        compiler_params=_params(("arbitrary", "arbitrary")),
        name="outproj",
    )(*mixes, w, x, mod)


def _ffn_kernel(x_ref, mod_ref, w1_ref, w3_ref, w2_ref, o_ref, h_scr, acc_scr, *, tm, tpb, lc):
    i = pl.program_id(0)
    j = pl.program_id(1)

    @pl.when(j == 0)
    def _():
        h_scr[...] = _adaln_tile(x_ref[...], mod_ref, i, tm, tpb, lc, 3, 4).astype(BF16)
        acc_scr[...] = jnp.zeros_like(acc_scr)

    h = h_scr[...]
    act = _silu(_dot(h, w1_ref[...])) * _dot(h, w3_ref[...])
    acc_scr[...] += _dot(act.astype(BF16), w2_ref[...])

    @pl.when(j == pl.num_programs(1) - 1)
    def _():
        d = x_ref.shape[1]
        gate = _gate_tile(mod_ref, i, tm, tpb, lc, 5, 0, d, d)
        o_ref[...] = x_ref[...] + gate * acc_scr[...]


def _ffn_dense(x, mod, w1, w3, w2, layer, t, lc):
    m, d = x.shape
    f = w1.shape[2]
    tm, tf = 768, 512
    return pl.pallas_call(
        functools.partial(_ffn_kernel, tm=tm, tpb=t // tm, lc=lc),
        grid=(m // tm, f // tf),
        in_specs=[pl.BlockSpec((tm, d), lambda i, j: (i, 0), pipeline_mode=pl.Buffered(1)),
                  pl.BlockSpec(mod.shape, lambda i, j: (0, 0)),
                  pl.BlockSpec((None, d, tf), lambda i, j: (layer, 0, j)),
                  pl.BlockSpec((None, d, tf), lambda i, j: (layer, 0, j)),
                  pl.BlockSpec((None, tf, d), lambda i, j: (layer, j, 0))],
        out_specs=pl.BlockSpec((tm, d), lambda i, j: (i, 0), pipeline_mode=pl.Buffered(1)),
        out_shape=jax.ShapeDtypeStruct((m, d), F32),
        scratch_shapes=[pltpu.VMEM((tm, d), BF16), pltpu.VMEM((tm, d), F32)],
        compiler_params=_params(("arbitrary", "arbitrary")),
        name="ffn_dense",
    )(x, mod, w1, w3, w2)


def _router_kernel(x_ref, mod_ref, wr_ref, h_ref, r_ref, *, tm, tpb, lc):
    i = pl.program_id(0)
    h = _adaln_tile(x_ref[...], mod_ref, i, tm, tpb, lc, 3, 4)
    h_ref[...] = h
    lane = lax.broadcasted_iota(jnp.int32, (tm, LANES), 1).astype(F32)
    logits = jnp.where(lane < N_EXPERTS, _dot_f32(h, wr_ref[...]), -jnp.inf)
    v1 = jnp.max(logits, axis=-1, keepdims=True)
    i1 = jnp.min(jnp.where(logits == v1, lane, float(LANES)), axis=-1, keepdims=True)
    rest = jnp.where(lane == i1, -jnp.inf, logits)
    v2 = jnp.max(rest, axis=-1, keepdims=True)
    i2 = jnp.min(jnp.where(rest == v2, lane, float(LANES)), axis=-1, keepdims=True)
    e2 = jnp.exp(v2 - v1)
    g1 = 1.0 / (1.0 + e2)
    g2 = e2 / (1.0 + e2)
    r_ref[...] = jnp.where(lane == 0, i1, jnp.where(lane == 1, i2, jnp.where(lane == 2, g1, jnp.where(lane == 3, g2, 0.0))))


def _router(x, mod, w_router, t, lc):
    m, d = x.shape
    tm = 768
    wr = jnp.pad(w_router, ((0, 0), (0, LANES - w_router.shape[1])))
    return pl.pallas_call(
        functools.partial(_router_kernel, tm=tm, tpb=t // tm, lc=lc),
        grid=(m // tm,),
        in_specs=[pl.BlockSpec((tm, d), lambda i: (i, 0)),
                  pl.BlockSpec(mod.shape, lambda i: (0, 0)),
                  pl.BlockSpec((d, LANES), lambda i: (0, 0))],
        out_specs=[pl.BlockSpec((tm, d), lambda i: (i, 0)),
                   pl.BlockSpec((tm, LANES), lambda i: (i, 0))],
        out_shape=[jax.ShapeDtypeStruct((m, d), F32), jax.ShapeDtypeStruct((m, LANES), F32)],
        compiler_params=_params(("arbitrary",)),
        name="moe_router",
    )(x, mod, wr)


def _gather_kernel(src_ref, h_ref, o_ref, buf, sem, *, tg):
    g = pl.program_id(0)
    last = pl.num_programs(0) - 1
    slot = g % 2

    def row_copy(idx, r, s):
        return pltpu.make_async_copy(h_ref.at[pl.ds(idx, 1)], buf.at[s, pl.ds(r, 1)], sem.at[s])

    def group_valid(gg):
        return src_ref[gg * tg] >= 0

    def issue(gg, s):
        def body(r, carry):
            row_copy(jnp.maximum(src_ref[gg * tg + r], 0), r, s).start()
            return carry
        lax.fori_loop(0, tg, body, 0, unroll=8)

    @pl.when((g == 0) & group_valid(0))
    def _():
        issue(0, 0)

    nxt = jnp.minimum(g + 1, last)

    @pl.when((g < last) & group_valid(nxt))
    def _():
        issue(nxt, 1 - slot)

    @pl.when(group_valid(g))
    def _():
        def body(r, carry):
            row_copy(0, r, slot).wait()
            return carry
        lax.fori_loop(0, tg, body, 0, unroll=8)
        o_ref[...] = buf[slot].astype(BF16)

    @pl.when(jnp.logical_not(group_valid(g)))
    def _():
        o_ref[...] = jnp.zeros_like(o_ref)


def _gather_rows(src, h, n_rows):
    d = h.shape[1]
    tg = MOE_TS
    return pl.pallas_call(
        functools.partial(_gather_kernel, tg=tg),
        grid_spec=pltpu.PrefetchScalarGridSpec(
            num_scalar_prefetch=1,
            grid=(n_rows // tg,),
            in_specs=[pl.BlockSpec(memory_space=pl.ANY)],
            out_specs=pl.BlockSpec((tg, d), lambda i, src: (i, 0)),
            scratch_shapes=[pltpu.VMEM((2, tg, d), h.dtype), pltpu.SemaphoreType.DMA((2,))]),
        out_shape=jax.ShapeDtypeStruct((n_rows, d), BF16),
        compiler_params=_params(("arbitrary",)),
        name="moe_gather",
    )(src, h)


def _rows_switch(n_rows, ts, tm, body):
    n_sub = (n_rows + ts - 1) // ts
    for c in range(tm // ts + 1):
        pl.when(n_sub == c)(functools.partial(body, c * ts))


def _moe_up_kernel(te_ref, tr_ref, nr_ref, hs_ref, w1_ref, w3_ref, a_ref, *, ts):
    tm, tf = a_ref.shape

    def body(n):
        if n > 0:
            h = hs_ref[0:n, :]
            act = _silu(_dot(h, w1_ref[...].astype(BF16))) * _dot(h, w3_ref[...].astype(BF16))
            a_ref[0:n, :] = act.astype(BF16)
        if n < tm:
            a_ref[n:tm, :] = jnp.zeros((tm - n, tf), BF16)

    _rows_switch(nr_ref[pl.program_id(0)], ts, tm, body)


def _moe_down_kernel(te_ref, tr_ref, nr_ref, a_ref, w2_ref, y_ref, *, ts):
    tm, tn = y_ref.shape

    def body(n):
        if n > 0:
            y_ref[0:n, :] = _dot(a_ref[0:n, :], w2_ref[...].astype(BF16))
        if n < tm:
            y_ref[n:tm, :] = jnp.zeros((tm - n, tn), F32)

    _rows_switch(nr_ref[pl.program_id(0)], ts, tm, body)


def _moe_ffn(tile_expert, tile_row, tile_rows, hs, w1, w3, w2, layer):
    r, d = hs.shape
    f = w1.shape[3]
    tm, tf, tn = MOE_TM, 512, 256
    nf, nn = f // tf, d // tn
    prefetch = (tile_expert, tile_row, tile_rows)
    frozen = lambda last: (lambda i, j, nr: jnp.where(nr[i] > 0, j, last))
    jf, jn = frozen(nf - 1), frozen(nn - 1)
    act = pl.pallas_call(
        functools.partial(_moe_up_kernel, ts=MOE_TS),
        grid_spec=pltpu.PrefetchScalarGridSpec(
            num_scalar_prefetch=3,
            grid=(r // tm, nf),
            in_specs=[pl.BlockSpec((tm, d), lambda i, j, te, tr, nr: (tr[i], 0)),
                      pl.BlockSpec((None, None, d, tf), lambda i, j, te, tr, nr: (layer, te[i], 0, jf(i, j, nr))),
                      pl.BlockSpec((None, None, d, tf), lambda i, j, te, tr, nr: (layer, te[i], 0, jf(i, j, nr)))],
            out_specs=pl.BlockSpec((tm, tf), lambda i, j, te, tr, nr: (i, j))),
        out_shape=jax.ShapeDtypeStruct((r, f), BF16),
        compiler_params=_params(("arbitrary", "arbitrary")),
        name="moe_up",
    )(*prefetch, hs, w1, w3)
    return pl.pallas_call(
        functools.partial(_moe_down_kernel, ts=MOE_TS),
        grid_spec=pltpu.PrefetchScalarGridSpec(
            num_scalar_prefetch=3,
            grid=(r // tm, nn),
            in_specs=[pl.BlockSpec((tm, f), lambda i, j, te, tr, nr: (tr[i], 0)),
                      pl.BlockSpec((None, None, f, tn), lambda i, j, te, tr, nr: (layer, te[i], 0, jn(i, j, nr)))],
            out_specs=pl.BlockSpec((tm, tn), lambda i, j, te, tr, nr: (i, j))),
        out_shape=jax.ShapeDtypeStruct((r, d), F32),
        compiler_params=_params(("arbitrary", "arbitrary")),
        name="moe_down",
    )(*prefetch, act, w2)


def _combine_kernel(p1_ref, p2_ref, y_ref, x_ref, r_ref, mod_ref, o_ref, y1_scr, y2_scr, sem, *, tc, tpb, lc):
    i = pl.program_id(0)
    last = pl.num_programs(0) - 1
    slot = i % 2

    def copies(idx1, idx2, r, s):
        return (pltpu.make_async_copy(y_ref.at[pl.ds(idx1, 1)], y1_scr.at[s, pl.ds(r, 1)], sem.at[s]),
                pltpu.make_async_copy(y_ref.at[pl.ds(idx2, 1)], y2_scr.at[s, pl.ds(r, 1)], sem.at[s]))

    def issue(ii, s):
        def body(r, carry):
            c1, c2 = copies(p1_ref[ii * tc + r], p2_ref[ii * tc + r], r, s)
            c1.start()
            c2.start()
            return carry
        lax.fori_loop(0, tc, body, 0, unroll=8)

    @pl.when(i == 0)
    def _():
        issue(0, 0)

    @pl.when(i < last)
    def _():
        issue(jnp.minimum(i + 1, last), 1 - slot)

    def drain(r, carry):
        c1, c2 = copies(0, 0, r, slot)
        c1.wait()
        c2.wait()
        return carry

    lax.fori_loop(0, tc, drain, 0, unroll=8)
    d = x_ref.shape[1]
    gate = _gate_tile(mod_ref, i, tc, tpb, lc, 5, 0, d, d)
    route = r_ref[...]
    f = route[:, 2:3] * y1_scr[slot] + route[:, 3:4] * y2_scr[slot]
    o_ref[...] = x_ref[...] + gate * f


def _combine(pos1, pos2, y, x, route, mod, t, lc):
    m, d = x.shape
    tc = 256
    return pl.pallas_call(
        functools.partial(_combine_kernel, tc=tc, tpb=t // tc, lc=lc),
        grid_spec=pltpu.PrefetchScalarGridSpec(
            num_scalar_prefetch=2,
            grid=(m // tc,),
            in_specs=[pl.BlockSpec(memory_space=pl.ANY),
                      pl.BlockSpec((tc, d), lambda i, p1, p2: (i, 0)),
                      pl.BlockSpec((tc, LANES), lambda i, p1, p2: (i, 0)),
                      pl.BlockSpec(mod.shape, lambda i, p1, p2: (0, 0))],
            out_specs=pl.BlockSpec((tc, d), lambda i, p1, p2: (i, 0)),
            scratch_shapes=[pltpu.VMEM((2, tc, d), F32), pltpu.VMEM((2, tc, d), F32),
                            pltpu.SemaphoreType.DMA((2,))]),
        out_shape=jax.ShapeDtypeStruct((m, d), F32),
        compiler_params=_params(("arbitrary",)),
        name="moe_combine",
    )(pos1, pos2, y, x, route, mod)


def _ffn_moe(x, mod, w_router, w1, w3, w2, layer, t, lc, ctx_out):
    m, d = x.shape
    tm = MOE_TM
    h, route = _router(x, mod, w_router, t, lc)
    experts = route[:, 0:TOP_K].astype(jnp.int32).reshape(-1)
    token = jnp.arange(TOP_K * m, dtype=jnp.int32) // TOP_K
    used = jnp.logical_or(ctx_out, token % t >= lc)
    onehot = ((experts[:, None] == jnp.arange(N_EXPERTS)[None, :]) & used[:, None]).astype(jnp.int32)
    rank = jnp.sum((jnp.cumsum(onehot, axis=0) - onehot) * onehot, axis=1)
    count = jnp.sum(onehot, axis=0)
    tiles = (count + tm - 1) // tm
    tile_end = jnp.cumsum(tiles)
    tile_start = tile_end - tiles
    n_tiles = (TOP_K * m + N_EXPERTS * (tm - 1)) // tm
    n_rows = n_tiles * tm
    pos = (tile_start * tm)[experts] + rank
    src = jnp.full((n_rows,), -1, jnp.int32).at[jnp.where(used, pos, n_rows)].set(token, mode='drop')
    pos = jnp.where(used, pos, token % tm)
    tile_ids = jnp.arange(n_tiles)
    tile_row = jnp.minimum(tile_ids, tile_end[-1] - 1)
    tile_expert = jnp.sum((tile_row[:, None] >= tile_end[None, :]).astype(jnp.int32), axis=1)
    tile_rows = jnp.clip(count[tile_expert] - (tile_ids - tile_start[tile_expert]) * tm, 0, tm)
    tile_rows = jnp.where(tile_ids < tile_end[-1], tile_rows, 0)
    hs = _gather_rows(src, h, n_rows)
    y = _moe_ffn(tile_expert.astype(jnp.int32), tile_row.astype(jnp.int32), tile_rows.astype(jnp.int32),
                 hs, w1, w3, w2, layer)
    pos = pos.reshape(m, TOP_K)
    return _combine(pos[:, 0], pos[:, 1], y, x, route, mod, t, lc)


def _rope_tables(n_lat, lc, rot_dim, lane0, period):
    tok = jnp.arange(n_lat)
    axis_dim = rot_dim // 2
    inv_freq = ROPE_BASE ** (-jnp.arange(0, axis_dim, 2, dtype=F32) / axis_dim)
    ang_r = (tok // GRID_W).astype(F32)[:, None] * inv_freq
    ang_c = (tok % GRID_W).astype(F32)[:, None] * inv_freq
    ang = jnp.concatenate([ang_r, ang_r, ang_c, ang_c], axis=-1)
    cos, sin = jnp.cos(ang), jnp.sin(ang)
    seg = rot_dim // 4
    lane = np.arange(LANES)
    rel = (lane - lane0) % period
    active = (lane >= lane0) & (rel < rot_dim)
    even = ((rel // seg) % 2 == 0)
    idx = np.where(active, rel, 0)
    cos_t = jnp.where(active[None, :], cos[:, idx], 1.0)
    sin_t = jnp.where(active[None, :], sin[:, idx], 0.0)
    sin_next = jnp.where(even[None, :], -sin_t, 0.0)
    sin_prev = jnp.where(even[None, :], 0.0, sin_t)
    ctx = lambda a, v: jnp.concatenate([jnp.full((lc, LANES), v, F32), a], axis=0)
    return ctx(cos_t, 1.0), ctx(sin_next, 0.0), ctx(sin_prev, 0.0)


def _relayout_kernel(w_ref, o_ref):
    rows = w_ref.shape[0]
    seg = lambda lo, n: w_ref[:, lo:lo + n].astype(BF16)
    z = lambda n: jnp.zeros((rows, n), BF16)
    a0, b0, g0, c0, d0 = 0, 1536, 3584, 3600, 4144
    parts = [seg(c0, 384), seg(c0 + 384, 128),
             seg(a0, 1536), seg(b0, 2048), seg(d0, 1536),
             seg(g0, 16), z(KR_LANE - 16), seg(c0 + 512, 32), z(LANES - KR_LANE - 32)]
    o_ref[...] = jnp.concatenate(parts, axis=1)


def _cast_kernel(w_ref, o_ref):
    o_ref[...] = w_ref[...].astype(BF16)


def _cast_bf16(w):
    nl, r, c = w.shape
    tr = 256
    return pl.pallas_call(
        _cast_kernel,
        grid=(nl, r // tr),
        in_specs=[pl.BlockSpec((None, tr, c), lambda l, i: (l, i, 0))],
        out_specs=pl.BlockSpec((None, tr, c), lambda l, i: (l, i, 0)),
        out_shape=jax.ShapeDtypeStruct(w.shape, BF16),
        compiler_params=_params(("arbitrary", "arbitrary")),
        name="weight_cast",
    )(w)


def _relayout_w_in(w_in):
    depth, d, n = w_in.shape
    tr = 256
    return pl.pallas_call(
        _relayout_kernel,
        grid=(depth, d // tr),
        in_specs=[pl.BlockSpec((None, tr, n), lambda l, i: (l, i, 0))],
        out_specs=pl.BlockSpec((None, tr, P_WIDTH), lambda l, i: (l, i, 0)),
        out_shape=jax.ShapeDtypeStruct((depth, d, P_WIDTH), BF16),
        compiler_params=_params(("arbitrary", "arbitrary")),
        name="w_in_relayout",
    )(w_in)


def _mla_weights(w_uq, w_ukv):
    hd = MLA_NOPE + MLA_ROPE
    wq = jnp.pad(w_uq.reshape(MLA_Q_RANK, MLA_HEADS, hd), ((0, 0), (0, 0), (0, LANES - hd)))
    wkv = w_ukv.reshape(MLA_KV_RANK, MLA_HEADS, MLA_NOPE + MLA_V)
    wk = jnp.pad(wkv[:, :, :MLA_NOPE], ((0, 0), (0, 0), (0, LANES - MLA_NOPE)))
    wv = wkv[:, :, MLA_NOPE:]
    return (wq.reshape(MLA_Q_RANK, -1).astype(BF16), wk.reshape(MLA_KV_RANK, -1).astype(BF16),
            wv.reshape(MLA_KV_RANK, -1).astype(BF16))


def kernel(x, c, ctx, c_ctx, w_ada, b_ada, w_in, w_out, da_q_gain, da_k_gain, da_lambda, da_out_gain, ml_conv_w, ml_conv_b, ml_gate_b, ml_out_gain, mla_cq_gain, mla_ckv_gain, mla_w_uq, mla_w_ukv, mla_q_gain, mla_k_gain, na_q_gain, na_k_gain, na_rpb, ffn_w1, ffn_w3, ffn_w2, moe_router, moe_w1, moe_w3, moe_w2):
    b, n_lat, d = x.shape
    lc = ctx.shape[1]
    t = lc + n_lat
    depth = w_in.shape[0]
    assert b <= 4 and lc == NA_QR * GRID_W and lc == TQ_DIFF == TQ_MLA and n_lat % lc == 0 and t % 768 == 0

    xs = jnp.concatenate([ctx, x], axis=1).reshape(b * t, d)
    cond = jnp.zeros((8, d), F32).at[:b].set(c).at[4].set(c_ctx)
    mod = _mod_table(cond, w_ada, b_ada)
    rope_da = _rope_tables(n_lat, lc, DA_QK, 0, DA_QK)
    rope_mla = _rope_tables(n_lat, lc, MLA_ROPE, KR_LANE, LANES)
    w_in_r = _relayout_w_in(w_in)
    w_out_b, ffn_w1_b, ffn_w3_b, ffn_w2_b = (_cast_bf16(w) for w in (w_out, ffn_w1, ffn_w3, ffn_w2))

    for l in range(depth):
        lam_init = 0.8 - 0.6 * math.exp(-0.3 * l)
        p = _inproj(xs, mod[l], w_in_r, l, t, lc)
        mix_a = _mixer_diff(p, rope_da, da_q_gain[l], da_k_gain[l], da_lambda[l], da_out_gain[l], lam_init, b, t, lc)
        mix_b = _mixer_mlstm(p, ml_conv_w[l], ml_conv_b[l], ml_gate_b[l], ml_out_gain[l], b, t, lc)
        wq, wk, wv = _mla_weights(mla_w_uq[l], mla_w_ukv[l])
        qc, kc, vc = _mla_prep(p, rope_mla, mla_cq_gain[l], mla_ckv_gain[l], wq, wk, wv,
                               mla_q_gain[l], mla_k_gain[l], t)
        mix_c = _mla_attn(qc, kc, vc, b, t, lc)
        mix_d = _mixer_na(p, na_q_gain[l], na_k_gain[l], na_rpb[l], b, t, lc)
        xs = _outproj((mix_a, mix_b, mix_c, mix_d), w_out_b, l, xs, mod[l], t, lc)
        if l % 2 == 0:
            xs = _ffn_dense(xs, mod[l], ffn_w1_b, ffn_w3_b, ffn_w2_b, l // 2, t, lc)
        else:
            xs = _ffn_moe(xs, mod[l], moe_router[l // 2], moe_w1, moe_w3, moe_w2, l // 2, t, lc, l < depth - 1)
    return xs.reshape(b, t, d)[:, lc:]
```

```python
import functools
import math

import numpy as np
import jax
import jax.numpy as jnp
from jax import lax
from jax.experimental import pallas as pl
from jax.experimental.pallas import tpu as pltpu

F32 = jnp.float32
BF16 = jnp.bfloat16

GRID_W = 64
DA_HEADS, DA_QK, DA_V = 4, 64, 128
ML_HEADS, ML_DIM, ML_CONV = 4, 128, 3
MLA_HEADS, MLA_NOPE, MLA_ROPE, MLA_V = 8, 64, 32, 64
MLA_Q_RANK, MLA_KV_RANK = 384, 128
NA_HEADS, NA_DIM, NA_WIN_R, NA_WIN_C = 8, 64, 8, 16
N_EXPERTS, TOP_K = 8, 2
ROPE_BASE = 10000.0
RMS_EPS = 1e-6
LOG2E = math.log2(math.e)

LANES = 128
VMEM_LIMIT = 56 * 1024 * 1024

P_CQ, P_CKV = 0, 384
P_AQ, P_AK, P_AV = 512, 1024, 1536
P_BQ, P_BK, P_BV, P_BO = 2048, 2560, 3072, 3584
P_DQ, P_DK, P_DV = 4096, 4608, 5120
P_LAST = 5632
P_WIDTH = 5760
KR_LANE = 64

TQ_DIFF, TQ_MLA = 256, 256
ML_L = 256
NA_QR, NA_KR = 4, 12
MOE_TM, MOE_TS = 1280, 256


def _params(sem):
    return pltpu.CompilerParams(dimension_semantics=sem, vmem_limit_bytes=VMEM_LIMIT)


def _silu(x):
    return x * (1.0 / (1.0 + jnp.exp(-x)))


def _sigmoid(x):
    return 1.0 / (1.0 + jnp.exp(-x))


def _rms(x, n=None):
    n = x.shape[-1] if n is None else n
    return x * lax.rsqrt(jnp.sum(x * x, axis=-1, keepdims=True) * (1.0 / n) + RMS_EPS)


def _rms_halves(x, gain):
    lo = lax.broadcasted_iota(jnp.int32, x.shape, 1) < 64
    x2 = x * x
    s_lo = jnp.sum(jnp.where(lo, x2, 0.0), axis=-1, keepdims=True)
    s_hi = jnp.sum(jnp.where(lo, 0.0, x2), axis=-1, keepdims=True)
    ms = jnp.where(lo, s_lo, s_hi) * (1.0 / 64)
    return x * lax.rsqrt(ms + RMS_EPS) * gain


def _rope(x, cos, sin_next, sin_prev, seg):
    return x * cos + pltpu.roll(x, LANES - seg, 1) * sin_next + pltpu.roll(x, seg, 1) * sin_prev


def _softmax2_pv(s2, v):
    e = jnp.exp2(s2 - jnp.max(s2, axis=-1, keepdims=True))
    return _dot(e.astype(BF16), v) * (1.0 / jnp.sum(e, axis=-1, keepdims=True))


def _dot(a, b):
    return jnp.dot(a, b, preferred_element_type=F32)


def _dot_nt(a, b):
    return lax.dot_general(a, b, (((1,), (1,)), ((), ())), preferred_element_type=F32)


def _dot_tn(a, b):
    return lax.dot_general(a, b, (((0,), (0,)), ((), ())), preferred_element_type=F32)


def _split3(a):
    a1 = a.astype(BF16)
    r = a - a1.astype(F32)
    a2 = r.astype(BF16)
    a3 = (r - a2.astype(F32)).astype(BF16)
    return a1, a2, a3


def _dot_f32(a, b):
    a1, a2, a3 = _split3(a)
    b1, b2, b3 = _split3(b)
    return (_dot(a1, b1) + (_dot(a1, b2) + _dot(a2, b1))
            + (_dot(a1, b3) + _dot(a2, b2) + _dot(a3, b1)))


def _adaln_tile(x, mod_ref, i, tm, tpb, lc, c_shift, c_scale):
    d = x.shape[1]
    b = i // tpb
    row = (i % tpb) * tm + lax.broadcasted_iota(jnp.int32, (tm, 1), 0)
    is_ctx = row < lc
    shift = jnp.where(is_ctx, mod_ref[4:5, c_shift * d:(c_shift + 1) * d],
                      mod_ref[pl.ds(b, 1), c_shift * d:(c_shift + 1) * d])
    scale = jnp.where(is_ctx, mod_ref[4:5, c_scale * d:(c_scale + 1) * d],
                      mod_ref[pl.ds(b, 1), c_scale * d:(c_scale + 1) * d])
    return _rms(x) * (1.0 + scale) + shift


def _gate_tile(mod_ref, i, tm, tpb, lc, c_gate, col0, width, d):
    b = i // tpb
    row = (i % tpb) * tm + lax.broadcasted_iota(jnp.int32, (tm, 1), 0)
    lo = c_gate * d + col0
    return jnp.where(row < lc, mod_ref[4:5, lo:lo + width], mod_ref[pl.ds(b, 1), lo:lo + width])


def _mod_kernel(c_ref, w_ref, b_ref, o_ref):
    s = _silu(c_ref[...]).astype(BF16)
    o_ref[...] = _dot(s, w_ref[...].astype(BF16)) + b_ref[...]


def _mod_table(cond, w_ada, b_ada):
    depth, d, n = w_ada.shape
    tn = 1024
    return pl.pallas_call(
        _mod_kernel,
        grid=(depth, n // tn),
        in_specs=[pl.BlockSpec((8, d), lambda l, j: (0, 0)),
                  pl.BlockSpec((None, d, tn), lambda l, j: (l, 0, j)),
                  pl.BlockSpec((None, 1, tn), lambda l, j: (l, 0, j))],
        out_specs=pl.BlockSpec((None, 8, tn), lambda l, j: (l, 0, j)),
        out_shape=jax.ShapeDtypeStruct((depth, 8, n), F32),
        compiler_params=_params(("arbitrary", "arbitrary")),
        name="mod_table",
    )(cond, w_ada, b_ada.reshape(depth, 1, n))


def _inproj_kernel(x_ref, mod_ref, w_ref, o_ref, h_scr, *, tm, tpb, lc):
    i = pl.program_id(0)

    @pl.when(pl.program_id(1) == 0)
    def _():
        h_scr[...] = _adaln_tile(x_ref[...], mod_ref, i, tm, tpb, lc, 0, 1).astype(BF16)

    o_ref[...] = _dot(h_scr[...], w_ref[...])


def _inproj(x, mod, w, layer, t, lc):
    m, d = x.shape
    n = w.shape[2]
    tm, tn = 768, 1152
    return pl.pallas_call(
        functools.partial(_inproj_kernel, tm=tm, tpb=t // tm, lc=lc),
        grid=(m // tm, n // tn),
        in_specs=[pl.BlockSpec((tm, d), lambda i, j: (i, 0)),
                  pl.BlockSpec(mod.shape, lambda i, j: (0, 0)),
                  pl.BlockSpec((None, d, tn), lambda i, j: (layer, 0, j))],
        out_specs=pl.BlockSpec((tm, tn), lambda i, j: (i, j)),
        out_shape=jax.ShapeDtypeStruct((m, n), F32),
        scratch_shapes=[pltpu.VMEM((tm, d), BF16)],
        compiler_params=_params(("arbitrary", "arbitrary")),
        name="inproj",
    )(x, mod, w)


def _diff_attn_kernel(q_ref, k_ref, v_ref, cos_ref, sn_ref, sp_ref, qg_ref, kg_ref, og_ref, lam_ref,
                      o_ref, kn_scr, vb_scr, *, tq, lc, lam_init, scale):
    qi = pl.program_id(2)
    t = k_ref.shape[0]

    @pl.when(qi == 0)
    def _():
        k = _rms_halves(k_ref[...], kg_ref[...])
        kn_scr[...] = _rope(k, cos_ref[...], sn_ref[...], sp_ref[...], 16).astype(BF16)
        vb_scr[...] = v_ref[...].astype(BF16)

    r0 = pl.multiple_of(qi * tq, tq)
    q = _rms_halves(q_ref[...], qg_ref[...])
    q = _rope(q, cos_ref[pl.ds(r0, tq), :], sn_ref[pl.ds(r0, tq), :], sp_ref[pl.ds(r0, tq), :], 16) * scale
    lo = lax.broadcasted_iota(jnp.int32, q.shape, 1) < 64
    q1 = jnp.where(lo, q, 0.0).astype(BF16)
    q2 = jnp.where(lo, 0.0, q).astype(BF16)
    lp = lam_ref[...]
    lam = (jnp.exp(jnp.sum(lp[0:1] * lp[1:2], axis=-1, keepdims=True))
           - jnp.exp(jnp.sum(lp[2:3] * lp[3:4], axis=-1, keepdims=True)) + lam_init)

    def attend(nk):
        kk = kn_scr[0:nk, :]
        vv = vb_scr[0:nk, :]
        o = _softmax2_pv(_dot_nt(q1, kk), vv) - lam * _softmax2_pv(_dot_nt(q2, kk), vv)
        o_ref[...] = (_rms(o) * og_ref[...] * (1.0 - lam_init)).astype(BF16)

    @pl.when(qi * tq < lc)
    def _():
        attend(lc)

    @pl.when(qi * tq >= lc)
    def _():
        attend(t)


def _mixer_diff(p, tabs, q_gain, k_gain, lam_params, out_gain, lam_init, b, t, lc):
    m = p.shape[0]
    tq = TQ_DIFF
    nq = t // tq
    cos, sn, sp = tabs
    tile2 = lambda g: jnp.tile(g, 2).reshape(1, LANES)
    full = lambda a: pl.BlockSpec(a.shape, lambda bb, h, qi: (0,) * a.ndim)
    args = (cos, sn, sp, tile2(q_gain), tile2(k_gain), out_gain.reshape(1, LANES), lam_params)
    return pl.pallas_call(
        functools.partial(_diff_attn_kernel, tq=tq, lc=lc, lam_init=lam_init, scale=DA_QK ** -0.5 * LOG2E),
        grid=(b, DA_HEADS, nq),
        in_specs=[pl.BlockSpec((tq, LANES), lambda bb, h, qi: (bb * nq + qi, P_AQ // LANES + h)),
                  pl.BlockSpec((t, LANES), lambda bb, h, qi: (bb, P_AK // LANES + h)),
                  pl.BlockSpec((t, LANES), lambda bb, h, qi: (bb, P_AV // LANES + h))]
                 + [full(a) for a in args],
        out_specs=pl.BlockSpec((tq, LANES), lambda bb, h, qi: (bb * nq + qi, h)),
        out_shape=jax.ShapeDtypeStruct((m, DA_HEADS * DA_V), BF16),
        scratch_shapes=[pltpu.VMEM((t, LANES), BF16), pltpu.VMEM((t, LANES), BF16)],
        compiler_params=_params(("arbitrary", "arbitrary", "arbitrary")),
        name="diff_attn",
    )(p, p, p, *args)


def _mla_prep_kernel(cq_ref, ckv_ref, last_ref, cos_ref, sn_ref, sp_ref, cqg_ref, ckvg_ref, wq_ref, wk_ref,
                     wv_ref, qg_ref, kg_ref, q_out, k_out, v_out, *, scale):
    hd = MLA_NOPE + MLA_ROPE
    cos, sn, sp = cos_ref[...], sn_ref[...], sp_ref[...]
    cqn = (_rms(cq_ref[...]) * cqg_ref[...]).astype(BF16)
    ckvn = (_rms(ckv_ref[...]) * ckvg_ref[...]).astype(BF16)
    q = _dot(cqn, wq_ref[...])
    kk = _dot(ckvn, wk_ref[...])
    v_out[...] = _dot(ckvn, wv_ref[...]).astype(BF16)
    last = last_ref[...]
    lane = lax.broadcasted_iota(jnp.int32, last.shape, 1)
    kr = jnp.where((lane >= KR_LANE) & (lane < KR_LANE + MLA_ROPE), last, 0.0)
    for h in range(MLA_HEADS):
        sl = slice(h * LANES, (h + 1) * LANES)
        qh = _rms(q[:, sl], hd) * qg_ref[...]
        q_out[:, sl] = (_rope(qh, cos, sn, sp, 8) * scale).astype(BF16)
        kh = _rms(kk[:, sl] + kr, hd) * kg_ref[...]
        k_out[:, sl] = _rope(kh, cos, sn, sp, 8).astype(BF16)


def _mla_prep(p, tabs, cq_gain, ckv_gain, wq, wk, wv, q_gain, k_gain, t):
    m = p.shape[0]
    tm = 768
    tpb = t // tm
    hw = MLA_HEADS * LANES
    cos, sn, sp = tabs
    tab = pl.BlockSpec((tm, LANES), lambda i: (i % tpb, 0))
    full = lambda a: pl.BlockSpec(a.shape, lambda i: (0,) * a.ndim)
    pad = lambda g: jnp.pad(g, (0, LANES - g.shape[0])).reshape(1, LANES)
    consts = (cq_gain.reshape(1, -1), ckv_gain.reshape(1, -1), wq, wk, wv, pad(q_gain), pad(k_gain))
    return pl.pallas_call(
        functools.partial(_mla_prep_kernel, scale=(MLA_NOPE + MLA_ROPE) ** -0.5 * LOG2E),
        grid=(m // tm,),
        in_specs=[pl.BlockSpec((tm, MLA_Q_RANK), lambda i: (i, P_CQ // MLA_Q_RANK)),
                  pl.BlockSpec((tm, LANES), lambda i: (i, P_CKV // LANES)),
                  pl.BlockSpec((tm, LANES), lambda i: (i, P_LAST // LANES)),
                  tab, tab, tab] + [full(a) for a in consts],
        out_specs=[pl.BlockSpec((tm, hw), lambda i: (i, 0)),
                   pl.BlockSpec((tm, hw), lambda i: (i, 0)),
                   pl.BlockSpec((tm, MLA_HEADS * MLA_V), lambda i: (i, 0))],
        out_shape=[jax.ShapeDtypeStruct((m, hw), BF16), jax.ShapeDtypeStruct((m, hw), BF16),
                   jax.ShapeDtypeStruct((m, MLA_HEADS * MLA_V), BF16)],
        compiler_params=_params(("arbitrary",)),
        name="mla_prep",
    )(p, p, p, cos, sn, sp, *consts)


def _mla_attn_kernel(q_ref, k_ref, v_ref, o_ref, *, tq, lc):
    qi = pl.program_id(2)
    t = k_ref.shape[0]

    def attend(nk):
        vv = v_ref[0:nk, :]
        outs = []
        for h in range(2):
            sl = slice(h * LANES, (h + 1) * LANES)
            outs.append(_softmax2_pv(_dot_nt(q_ref[:, sl], k_ref[0:nk, sl]), vv))
        lo = lax.broadcasted_iota(jnp.int32, outs[0].shape, 1) < MLA_V
        o_ref[...] = jnp.where(lo, outs[0], outs[1]).astype(BF16)

    @pl.when(qi * tq < lc)
    def _():
        attend(lc)

    @pl.when(qi * tq >= lc)
    def _():
        attend(t)


def _mla_attn(q, k, v, b, t, lc):
    m = q.shape[0]
    tq = TQ_MLA
    nq = t // tq
    return pl.pallas_call(
        functools.partial(_mla_attn_kernel, tq=tq, lc=lc),
        grid=(b, MLA_HEADS // 2, nq),
        in_specs=[pl.BlockSpec((tq, 2 * LANES), lambda bb, hp, qi: (bb * nq + qi, hp)),
                  pl.BlockSpec((t, 2 * LANES), lambda bb, hp, qi: (bb, hp)),
                  pl.BlockSpec((t, LANES), lambda bb, hp, qi: (bb, hp))],
        out_specs=pl.BlockSpec((tq, LANES), lambda bb, hp, qi: (bb * nq + qi, hp)),
        out_shape=jax.ShapeDtypeStruct((m, MLA_HEADS * MLA_V), BF16),
        compiler_params=_params(("arbitrary", "arbitrary", "arbitrary")),
        name="mla_attn",
    )(q, k, v)


def _na_kernel(q_ref, k_ref, v_ref, bias_ref, qg_ref, kg_ref, o_ref, kn_scr, vb_scr, *, lc, rows, scale):
    s = pl.program_id(1)
    tq = q_ref.shape[0]
    n_ctx_blk = lc // tq
    band = NA_KR * GRID_W
    npair = NA_HEADS // 2

    @pl.when(s == 0)
    def _():
        for pp in range(npair):
            sl = slice(pp * LANES, (pp + 1) * LANES)
            kn_scr[:, sl] = _rms_halves(k_ref[:, sl], kg_ref[:, sl]).astype(BF16)
        vb_scr[...] = v_ref[...].astype(BF16)

    def heads(pv_fn):
        for pp in range(npair):
            sl = slice(pp * LANES, (pp + 1) * LANES)
            q = _rms_halves(q_ref[:, sl], qg_ref[:, sl]) * scale
            lo = lax.broadcasted_iota(jnp.int32, q.shape, 1) < NA_DIM
            qs = jnp.concatenate([jnp.where(lo, q, 0.0), jnp.where(lo, 0.0, q)], axis=0).astype(BF16)
            o2 = pv_fn(qs, sl, pp)
            o_ref[:, sl] = jnp.where(lo, o2[0:tq], o2[tq:2 * tq]).astype(BF16)

    @pl.when(s < n_ctx_blk)
    def _():
        def f(qs, sl, pp):
            return _softmax2_pv(_dot_nt(qs, kn_scr[0:lc, sl]), vb_scr[0:lc, sl])
        heads(f)

    @pl.when(s >= n_ctx_blk)
    def _():
        rb = s - n_ctx_blk
        krow = jnp.clip(rb * NA_QR - NA_WIN_R // 2, 0, rows - NA_KR)
        k0 = pl.multiple_of(lc + krow * GRID_W, GRID_W)

        def f(qs, sl, pp):
            s_loc = _dot_nt(qs, kn_scr[pl.ds(k0, band), sl]) + bias_ref[pp]
            s_ctx = _dot_nt(qs, kn_scr[0:lc, sl])
            mx = jnp.maximum(jnp.max(s_loc, axis=-1, keepdims=True), jnp.max(s_ctx, axis=-1, keepdims=True))
            e_loc = jnp.exp2(s_loc - mx)
            e_ctx = jnp.exp2(s_ctx - mx)
            den = jnp.sum(e_loc, axis=-1, keepdims=True) + jnp.sum(e_ctx, axis=-1, keepdims=True)
            num = _dot(e_loc.astype(BF16), vb_scr[pl.ds(k0, band), sl]) + _dot(e_ctx.astype(BF16), vb_scr[0:lc, sl])
            return num * (1.0 / den)
        heads(f)


def _na_variant(rb, n_rb):
    return jnp.where(rb == 0, 0, jnp.where(rb == n_rb - 1, 2, 1))


def _na_bias_table(rpb, rows):
    n_rb = rows // NA_QR
    rbs = np.array([0, 1, n_rb - 1])
    q_rows = rbs[:, None] * NA_QR + np.arange(NA_QR)
    key_rows = np.clip(rbs * NA_QR - NA_WIN_R // 2, 0, rows - NA_KR)[:, None] + np.arange(NA_KR)
    cols = np.arange(GRID_W)
    r0 = np.clip(q_rows - NA_WIN_R // 2, 0, rows - NA_WIN_R)[:, :, None]
    c0 = np.clip(cols - NA_WIN_C // 2, 0, GRID_W - NA_WIN_C)[:, None]
    kr = key_rows[:, None, :]
    ok_r = (kr >= r0) & (kr < r0 + NA_WIN_R)
    ok_c = (cols[None, :] >= c0) & (cols[None, :] < c0 + NA_WIN_C)
    rel_r = np.clip(kr - q_rows[:, :, None] + NA_WIN_R - 1, 0, 2 * NA_WIN_R - 2)
    rel_c = np.clip(cols[None, :] - cols[:, None] + NA_WIN_C - 1, 0, 2 * NA_WIN_C - 2)
    oh_c = jnp.asarray(np.eye(2 * NA_WIN_C - 1, dtype=np.float32)[rel_c])
    rpb2 = rpb.reshape(NA_HEADS // 2, 2, 2 * NA_WIN_R - 1, 2 * NA_WIN_C - 1)
    tiles = jnp.einsum('phde,qce->phdqc', rpb2, oh_c, precision=lax.Precision.HIGHEST)
    tiles = jnp.where(jnp.asarray(ok_c), tiles * LOG2E, -jnp.inf)
    npair, nv, tq, band = NA_HEADS // 2, len(rbs), NA_QR * GRID_W, NA_KR * GRID_W
    return pl.pallas_call(
        _na_bias_kernel,
        grid_spec=pltpu.PrefetchScalarGridSpec(
            num_scalar_prefetch=2,
            grid=(npair, nv),
            in_specs=[pl.BlockSpec((None,) + tiles.shape[1:], lambda p, v, rel, ok: (p, 0, 0, 0, 0))],
            out_specs=pl.BlockSpec((None, None, 2 * tq, band), lambda p, v, rel, ok: (p, v, 0, 0))),
        out_shape=jax.ShapeDtypeStruct((npair, nv, 2 * tq, band), F32),
        compiler_params=_params(("arbitrary", "arbitrary")),
        name="na_bias",
    )(jnp.asarray(rel_r.reshape(-1), jnp.int32), jnp.asarray(ok_r.reshape(-1), jnp.int32), tiles)


def _na_bias_kernel(rel_ref, ok_ref, t_ref, o_ref):
    v = pl.program_id(1)
    for hh in range(2):
        for i in range(NA_QR):
            rows = slice((hh * NA_QR + i) * GRID_W, (hh * NA_QR + i + 1) * GRID_W)
            for k in range(0, NA_KR, 2):
                pair = []
                for kk in (k, k + 1):
                    idx = (v * NA_QR + i) * NA_KR + kk
                    pair.append(jnp.where(ok_ref[idx] > 0, t_ref[hh, rel_ref[idx]], -jnp.inf))
                o_ref[rows, k * GRID_W:(k + 2) * GRID_W] = jnp.concatenate(pair, axis=1)


def _mixer_na(p, q_gain, k_gain, rpb, b, t, lc):
    m = p.shape[0]
    rows = (t - lc) // GRID_W
    tq = NA_QR * GRID_W
    nblk = t // tq
    n_ctx_blk = lc // tq
    bias = _na_bias_table(rpb, rows)
    tile8 = lambda g: jnp.tile(g, NA_HEADS).reshape(1, NA_HEADS * NA_DIM)
    w = NA_HEADS * NA_DIM
    return pl.pallas_call(
        functools.partial(_na_kernel, lc=lc, rows=rows, scale=NA_DIM ** -0.5 * LOG2E),
        grid=(b, nblk),
        in_specs=[pl.BlockSpec((tq, w), lambda bb, s: (bb * nblk + s, P_DQ // w)),
                  pl.BlockSpec((t, w), lambda bb, s: (bb, P_DK // w)),
                  pl.BlockSpec((t, w), lambda bb, s: (bb, P_DV // w)),
                  pl.BlockSpec((NA_HEADS // 2, None, 2 * tq, NA_KR * GRID_W),
                               lambda bb, s: (0, _na_variant(jnp.maximum(s - n_ctx_blk, 0), rows // NA_QR), 0, 0)),
                  pl.BlockSpec((1, w), lambda bb, s: (0, 0)),
                  pl.BlockSpec((1, w), lambda bb, s: (0, 0))],
        out_specs=pl.BlockSpec((tq, w), lambda bb, s: (bb * nblk + s, 0)),
        out_shape=jax.ShapeDtypeStruct((m, w), BF16),
        scratch_shapes=[pltpu.VMEM((t, w), BF16), pltpu.VMEM((t, w), BF16)],
        compiler_params=_params(("arbitrary", "arbitrary")),
        name="na_attn",
    )(p, p, p, bias, tile8(q_gain), tile8(k_gain))


def _ml_prep_kernel(x_ref, w_ref, b_ref, o_ref, *, lc):
    t = x_ref.shape[0]
    x = x_ref[...]
    row = lax.broadcasted_iota(jnp.int32, (t, 1), 0)
    prev = jnp.where((row == 0) | (row == lc), 0.0, pltpu.roll(x, 1, 0))
    nxt = jnp.where((row == lc - 1) | (row == t - 1), 0.0, pltpu.roll(x, t - 1, 0))
    y = _silu(prev * w_ref[0:1, :] + x * w_ref[1:2, :] + nxt * w_ref[2:3, :] + b_ref[...])
    o_ref[...] = y * jnp.where(pl.program_id(1) == 1, ML_DIM ** -0.5, 1.0)


def _ml_prep(p, conv_w, conv_b, b, t, lc):
    m = p.shape[0]
    w = ML_HEADS * ML_DIM
    return pl.pallas_call(
        functools.partial(_ml_prep_kernel, lc=lc),
        grid=(b, 2),
        in_specs=[pl.BlockSpec((t, w), lambda bb, c: (bb, P_BQ // w + c)),
                  pl.BlockSpec((ML_CONV, w), lambda bb, c: (0, c)),
                  pl.BlockSpec((1, w), lambda bb, c: (0, c))],
        out_specs=pl.BlockSpec((t, w), lambda bb, c: (bb, c)),
        out_shape=jax.ShapeDtypeStruct((m, 2 * w), F32),
        compiler_params=_params(("arbitrary", "arbitrary")),
        name="mlstm_prep",
    )(p, conv_w, conv_b.reshape(1, -1))


def _ml_scan_kernel(qf_ref, kf_ref, vf_ref, gf_ref, qb_ref, kb_ref, vb_ref, gb_ref, of_ref, ob_ref,
                    c_scr, n_scr, m_scr):
    @pl.when(pl.program_id(1) == 0)
    def _():
        c_scr[...] = jnp.zeros_like(c_scr)
        n_scr[...] = jnp.zeros_like(n_scr)
        m_scr[...] = jnp.zeros_like(m_scr)

    L = qf_ref.shape[0]
    tt = lax.broadcasted_iota(jnp.int32, (L, L), 0)
    ss = lax.broadcasted_iota(jnp.int32, (L, L), 1)
    eye = tt == ss
    dirs = ((qf_ref, kf_ref, vf_ref, gf_ref, of_ref, ss <= tt, tt <= ss),
            (qb_ref, kb_ref, vb_ref, gb_ref, ob_ref, ss >= tt, tt >= ss))
    for d, (q_ref, k_ref, v_ref, g_ref, o_ref, cm, cm_t) in enumerate(dirs):
        for h in range(ML_HEADS):
            sl = slice(h * ML_DIM, (h + 1) * ML_DIM)
            _ml_chain(q_ref[:, sl], k_ref[:, sl], v_ref[:, sl], g_ref[h], o_ref.at[:, sl],
                      c_scr.at[d, h], n_scr.at[d, h], m_scr.at[d, h], eye, cm, cm_t)


def _ml_chain(q, k, v, g, o_ref, c_scr, n_scr, m_scr, eye, cm, cm_t):
    i_row = g[0:1, :]
    f_row = g[1:2, :]
    lf_row = jnp.minimum(f_row, 0.0) - jnp.log(1.0 + jnp.exp(-jnp.abs(f_row)))
    to_col = lambda r: jnp.sum(jnp.where(eye, r, 0.0), axis=-1, keepdims=True)
    lf_col = to_col(lf_row)
    i_col = to_col(i_row)
    b_col = jnp.sum(jnp.where(cm, lf_row, 0.0), axis=-1, keepdims=True)
    b_row = jnp.sum(jnp.where(cm_t, lf_col, 0.0), axis=0, keepdims=True)
    b_end = jnp.sum(lf_row, axis=-1, keepdims=True)
    m_old = m_scr[...]
    c_old = c_scr[...]
    n_old = n_scr[...]

    log_w = jnp.where(cm, b_col - b_row + i_row, -jnp.inf)
    inter = b_col + m_old
    m_t = jnp.maximum(inter, jnp.max(log_w, axis=-1, keepdims=True))
    qb = q.astype(BF16)
    kb = k.astype(BF16)
    vb = v.astype(BF16)
    s = _dot_nt(qb, kb) * jnp.exp(log_w - m_t)
    a = jnp.exp(inter - m_t)
    num = a * _dot(qb, c_old.astype(BF16)) + _dot(s.astype(BF16), vb)
    den = a * jnp.sum(q * n_old, axis=-1, keepdims=True) + jnp.sum(s, axis=-1, keepdims=True)
    o_ref[...] = num / jnp.maximum(jnp.abs(den), jnp.exp(-m_t))

    w_end = b_end - b_col + i_col
    m_new = jnp.maximum(b_end + m_old, jnp.max(w_end, axis=0, keepdims=True))
    decay = jnp.exp(b_end + m_old - m_new)
    we = jnp.exp(w_end - m_new)
    c_scr[...] = decay * c_old + _dot_tn(kb, (we * v).astype(BF16))
    n_scr[...] = decay * n_old + jnp.sum(we * k, axis=0, keepdims=True)
    m_scr[...] = m_new


def _ml_scan(qk, p, gates, b, t, lc):
    m = p.shape[0]
    L = ML_L
    nct, ncc = t // L, lc // L
    w = ML_HEADS * ML_DIM

    fwd = lambda j: j
    bwd = lambda j: jnp.where(j < ncc, ncc - 1 - j, nct - 1 - (j - ncc))

    def specs(chunk, d):
        blk = lambda col: pl.BlockSpec((L, w), lambda bb, j: (bb * nct + chunk(j), col))
        return [blk(0), blk(1), blk(P_BV // w),
                pl.BlockSpec((None, None, None, ML_HEADS, 2, L), lambda bb, j: (bb, d, chunk(j), 0, 0, 0))]

    out = lambda chunk: pl.BlockSpec((L, w), lambda bb, j: (bb * nct + chunk(j), 0))
    nh = ML_HEADS
    return pl.pallas_call(
        _ml_scan_kernel,
        grid=(b, nct),
        in_specs=specs(fwd, 0) + specs(bwd, 1),
        out_specs=[out(fwd), out(bwd)],
        out_shape=[jax.ShapeDtypeStruct((m, w), F32)] * 2,
        scratch_shapes=[pltpu.VMEM((2, nh, ML_DIM, ML_DIM), F32), pltpu.VMEM((2, nh, 1, ML_DIM), F32),
                        pltpu.VMEM((2, nh, 1, 1), F32)],
        compiler_params=_params(("arbitrary", "arbitrary")),
        name="mlstm_scan",
    )(qk, qk, p, gates, qk, qk, p, gates)


def _ml_finish_kernel(hf_ref, hb_ref, o_ref, g_ref, out_ref):
    hsum = hf_ref[...] + hb_ref[...]
    og = _sigmoid(o_ref[...])
    for h in range(ML_HEADS):
        sl = slice(h * ML_DIM, (h + 1) * ML_DIM)
        out_ref[:, sl] = (_rms(hsum[:, sl]) * g_ref[...] * og[:, sl]).astype(BF16)


def _ml_finish(hf, hb, p, out_gain):
    m = p.shape[0]
    w = ML_HEADS * ML_DIM
    tm = 768
    return pl.pallas_call(
        _ml_finish_kernel,
        grid=(m // tm,),
        in_specs=[pl.BlockSpec((tm, w), lambda i: (i, 0)),
                  pl.BlockSpec((tm, w), lambda i: (i, 0)),
                  pl.BlockSpec((tm, w), lambda i: (i, P_BO // w)),
                  pl.BlockSpec((1, ML_DIM), lambda i: (0, 0))],
        out_specs=pl.BlockSpec((tm, w), lambda i: (i, 0)),
        out_shape=jax.ShapeDtypeStruct((m, w), BF16),
        compiler_params=_params(("arbitrary",)),
        name="mlstm_finish",
    )(hf, hb, p, out_gain.reshape(1, ML_DIM))


def _mixer_mlstm(p, conv_w, conv_b, gate_b, out_gain, b, t, lc):
    qk = _ml_prep(p, conv_w, conv_b, b, t, lc)
    nct = t // ML_L
    g = p[:, P_LAST:P_LAST + 4 * ML_HEADS].reshape(b, nct, ML_L, 2, 2, ML_HEADS) + gate_b
    gates = g.transpose(0, 3, 1, 5, 4, 2)
    hf, hb = _ml_scan(qk, p, gates, b, t, lc)
    return _ml_finish(hf, hb, p, out_gain)


def _outproj_kernel(a_ref, b_ref, c_ref, d_ref, w_ref, x_ref, mod_ref, o_ref, *, tm, tpb, lc, tn, d):
    i = pl.program_id(0)
    j = pl.program_id(1)
    kw = a_ref.shape[1]
    y = _dot(a_ref[...], w_ref[0:kw, :])
    y += _dot(b_ref[...], w_ref[kw:2 * kw, :])
    y += _dot(c_ref[...], w_ref[2 * kw:3 * kw, :])
    y += _dot(d_ref[...], w_ref[3 * kw:4 * kw, :])
    row = (i % tpb) * tm + lax.broadcasted_iota(jnp.int32, (tm, 1), 0)
    b = i // tpb
    col = pl.multiple_of(2 * d + j * tn, LANES)
    gate = jnp.where(row < lc, mod_ref[4:5, pl.ds(col, tn)], mod_ref[pl.ds(b, 1), pl.ds(col, tn)])
    o_ref[...] = x_ref[...] + gate * y


def _outproj(mixes, w, layer, x, mod, t, lc):
    m, d = x.shape
    tm, tn = 768, min(1024, d)
    kw = mixes[0].shape[1]
    mix_spec = pl.BlockSpec((tm, kw), lambda i, j: (i, 0))
    return pl.pallas_call(
        functools.partial(_outproj_kernel, tm=tm, tpb=t // tm, lc=lc, tn=tn, d=d),
        grid=(m // tm, d // tn),
        in_specs=[mix_spec] * 4 + [pl.BlockSpec((None, 4 * kw, tn), lambda i, j: (layer, 0, j)),
                                   pl.BlockSpec((tm, tn), lambda i, j: (i, j)),
                                   pl.BlockSpec(mod.shape, lambda i, j: (0, 0))],
        out_specs=pl.BlockSpec((tm, tn), lambda i, j: (i, j)),
        out_shape=jax.ShapeDtypeStruct((m, d), F32),
        compiler_params=_params(("arbitrary", "arbitrary")),
        name="outproj",
    )(*mixes, w, x, mod)


def _ffn_kernel(x_ref, mod_ref, w1_ref, w3_ref, w2_ref, o_ref, h_scr, acc_scr, *, tm, tpb, lc):
    i = pl.program_id(0)
    j = pl.program_id(1)

    @pl.when(j == 0)
    def _():
        h_scr[...] = _adaln_tile(x_ref[...], mod_ref, i, tm, tpb, lc, 3, 4).astype(BF16)
        acc_scr[...] = jnp.zeros_like(acc_scr)

    h = h_scr[...]
    act = _silu(_dot(h, w1_ref[...])) * _dot(h, w3_ref[...])
    acc_scr[...] += _dot(act.astype(BF16), w2_ref[...])

    @pl.when(j == pl.num_programs(1) - 1)
    def _():
        d = x_ref.shape[1]
        gate = _gate_tile(mod_ref, i, tm, tpb, lc, 5, 0, d, d)
        o_ref[...] = x_ref[...] + gate * acc_scr[...]


def _ffn_dense(x, mod, w1, w3, w2, layer, t, lc):
    m, d = x.shape
    f = w1.shape[2]
    tm, tf = 768, 512
    return pl.pallas_call(
        functools.partial(_ffn_kernel, tm=tm, tpb=t // tm, lc=lc),
        grid=(m // tm, f // tf),
        in_specs=[pl.BlockSpec((tm, d), lambda i, j: (i, 0), pipeline_mode=pl.Buffered(1)),
                  pl.BlockSpec(mod.shape, lambda i, j: (0, 0)),
                  pl.BlockSpec((None, d, tf), lambda i, j: (layer, 0, j)),
                  pl.BlockSpec((None, d, tf), lambda i, j: (layer, 0, j)),
                  pl.BlockSpec((None, tf, d), lambda i, j: (layer, j, 0))],
        out_specs=pl.BlockSpec((tm, d), lambda i, j: (i, 0), pipeline_mode=pl.Buffered(1)),
        out_shape=jax.ShapeDtypeStruct((m, d), F32),
        scratch_shapes=[pltpu.VMEM((tm, d), BF16), pltpu.VMEM((tm, d), F32)],
        compiler_params=_params(("arbitrary", "arbitrary")),
        name="ffn_dense",
    )(x, mod, w1, w3, w2)


def _router_kernel(x_ref, mod_ref, wr_ref, h_ref, r_ref, *, tm, tpb, lc):
    i = pl.program_id(0)
    h = _adaln_tile(x_ref[...], mod_ref, i, tm, tpb, lc, 3, 4)
    h_ref[...] = h
    lane = lax.broadcasted_iota(jnp.int32, (tm, LANES), 1).astype(F32)
    logits = jnp.where(lane < N_EXPERTS, _dot_f32(h, wr_ref[...]), -jnp.inf)
    v1 = jnp.max(logits, axis=-1, keepdims=True)
    i1 = jnp.min(jnp.where(logits == v1, lane, float(LANES)), axis=-1, keepdims=True)
    rest = jnp.where(lane == i1, -jnp.inf, logits)
    v2 = jnp.max(rest, axis=-1, keepdims=True)
    i2 = jnp.min(jnp.where(rest == v2, lane, float(LANES)), axis=-1, keepdims=True)
    e2 = jnp.exp(v2 - v1)
    g1 = 1.0 / (1.0 + e2)
    g2 = e2 / (1.0 + e2)
    r_ref[...] = jnp.where(lane == 0, i1, jnp.where(lane == 1, i2, jnp.where(lane == 2, g1, jnp.where(lane == 3, g2, 0.0))))


def _router(x, mod, w_router, t, lc):
    m, d = x.shape
    tm = 768
    wr = jnp.pad(w_router, ((0, 0), (0, LANES - w_router.shape[1])))
    return pl.pallas_call(
        functools.partial(_router_kernel, tm=tm, tpb=t // tm, lc=lc),
        grid=(m // tm,),
        in_specs=[pl.BlockSpec((tm, d), lambda i: (i, 0)),
                  pl.BlockSpec(mod.shape, lambda i: (0, 0)),
                  pl.BlockSpec((d, LANES), lambda i: (0, 0))],
        out_specs=[pl.BlockSpec((tm, d), lambda i: (i, 0)),
                   pl.BlockSpec((tm, LANES), lambda i: (i, 0))],
        out_shape=[jax.ShapeDtypeStruct((m, d), F32), jax.ShapeDtypeStruct((m, LANES), F32)],
        compiler_params=_params(("arbitrary",)),
        name="moe_router",
    )(x, mod, wr)


def _gather_kernel(src_ref, h_ref, o_ref, buf, sem, *, tg):
    g = pl.program_id(0)
    last = pl.num_programs(0) - 1
    slot = g % 2

    def row_copy(idx, r, s):
        return pltpu.make_async_copy(h_ref.at[pl.ds(idx, 1)], buf.at[s, pl.ds(r, 1)], sem.at[s])

    def group_valid(gg):
        return src_ref[gg * tg] >= 0

    def issue(gg, s):
        def body(r2, carry):
            for prio in range(2):
                r = 2 * r2 + prio
                row_copy(jnp.maximum(src_ref[gg * tg + r], 0), r, s).start(priority=prio)
            return carry
        lax.fori_loop(0, tg // 2, body, 0, unroll=4)

    @pl.when((g == 0) & group_valid(0))
    def _():
        issue(0, 0)

    nxt = jnp.minimum(g + 1, last)

    @pl.when((g < last) & group_valid(nxt))
    def _():
        issue(nxt, 1 - slot)

    @pl.when(group_valid(g))
    def _():
        def body(r, carry):
            row_copy(0, r, slot).wait()
            return carry
        lax.fori_loop(0, tg, body, 0, unroll=8)
        o_ref[...] = buf[slot].astype(BF16)

    @pl.when(jnp.logical_not(group_valid(g)))
    def _():
        o_ref[...] = jnp.zeros_like(o_ref)


def _gather_rows(src, h, n_rows):
    d = h.shape[1]
    tg = MOE_TS
    return pl.pallas_call(
        functools.partial(_gather_kernel, tg=tg),
        grid_spec=pltpu.PrefetchScalarGridSpec(
            num_scalar_prefetch=1,
            grid=(n_rows // tg,),
            in_specs=[pl.BlockSpec(memory_space=pl.ANY)],
            out_specs=pl.BlockSpec((tg, d), lambda i, src: (i, 0)),
            scratch_shapes=[pltpu.VMEM((2, tg, d), h.dtype), pltpu.SemaphoreType.DMA((2,))]),
        out_shape=jax.ShapeDtypeStruct((n_rows, d), BF16),
        compiler_params=_params(("arbitrary",)),
        name="moe_gather",
    )(src, h)


def _rows_switch(n_rows, ts, tm, body):
    n_sub = (n_rows + ts - 1) // ts
    for c in range(tm // ts + 1):
        pl.when(n_sub == c)(functools.partial(body, c * ts))


def _moe_up_kernel(te_ref, tr_ref, nr_ref, hs_ref, w1_ref, w3_ref, a_ref, *, ts):
    tm, tf = a_ref.shape

    def body(n):
        if n > 0:
            h = hs_ref[0:n, :]
            act = _silu(_dot(h, w1_ref[...].astype(BF16))) * _dot(h, w3_ref[...].astype(BF16))
            a_ref[0:n, :] = act.astype(BF16)
        if n < tm:
            a_ref[n:tm, :] = jnp.zeros((tm - n, tf), BF16)

    _rows_switch(nr_ref[pl.program_id(0)], ts, tm, body)


def _moe_down_kernel(te_ref, tr_ref, nr_ref, a_ref, w2_ref, y_ref, *, ts):
    tm, tn = y_ref.shape

    def body(n):
        if n > 0:
            y_ref[0:n, :] = _dot(a_ref[0:n, :], w2_ref[...].astype(BF16))
        if n < tm:
            y_ref[n:tm, :] = jnp.zeros((tm - n, tn), F32)

    _rows_switch(nr_ref[pl.program_id(0)], ts, tm, body)


def _moe_ffn(tile_expert, tile_row, tile_rows, hs, w1, w3, w2, layer):
    r, d = hs.shape
    f = w1.shape[3]
    tm, tf, tn = MOE_TM, 512, 256
    nf, nn = f // tf, d // tn
    prefetch = (tile_expert, tile_row, tile_rows)
    frozen = lambda last: (lambda i, j, nr: jnp.where(nr[i] > 0, j, last))
    jf, jn = frozen(nf - 1), frozen(nn - 1)
    act = pl.pallas_call(
        functools.partial(_moe_up_kernel, ts=MOE_TS),
        grid_spec=pltpu.PrefetchScalarGridSpec(
            num_scalar_prefetch=3,
            grid=(r // tm, nf),
            in_specs=[pl.BlockSpec((tm, d), lambda i, j, te, tr, nr: (tr[i], 0)),
                      pl.BlockSpec((None, None, d, tf), lambda i, j, te, tr, nr: (layer, te[i], 0, jf(i, j, nr))),
                      pl.BlockSpec((None, None, d, tf), lambda i, j, te, tr, nr: (layer, te[i], 0, jf(i, j, nr)))],
            out_specs=pl.BlockSpec((tm, tf), lambda i, j, te, tr, nr: (i, j))),
        out_shape=jax.ShapeDtypeStruct((r, f), BF16),
        compiler_params=_params(("arbitrary", "arbitrary")),
        name="moe_up",
    )(*prefetch, hs, w1, w3)
    return pl.pallas_call(
        functools.partial(_moe_down_kernel, ts=MOE_TS),
        grid_spec=pltpu.PrefetchScalarGridSpec(
            num_scalar_prefetch=3,
            grid=(r // tm, nn),
            in_specs=[pl.BlockSpec((tm, f), lambda i, j, te, tr, nr: (tr[i], 0)),
                      pl.BlockSpec((None, None, f, tn), lambda i, j, te, tr, nr: (layer, te[i], 0, jn(i, j, nr)))],
            out_specs=pl.BlockSpec((tm, tn), lambda i, j, te, tr, nr: (i, j))),
        out_shape=jax.ShapeDtypeStruct((r, d), F32),
        compiler_params=_params(("arbitrary", "arbitrary")),
        name="moe_down",
    )(*prefetch, act, w2)


def _combine_kernel(p1_ref, p2_ref, y_ref, x_ref, r_ref, mod_ref, o_ref, y1_scr, y2_scr, sem, *, tc, tpb, lc):
    i = pl.program_id(0)
    last = pl.num_programs(0) - 1
    slot = i % 2

    def copies(idx1, idx2, r, s):
        return (pltpu.make_async_copy(y_ref.at[pl.ds(idx1, 1)], y1_scr.at[s, pl.ds(r, 1)], sem.at[s]),
                pltpu.make_async_copy(y_ref.at[pl.ds(idx2, 1)], y2_scr.at[s, pl.ds(r, 1)], sem.at[s]))

    def issue(ii, s):
        def body(r, carry):
            c1, c2 = copies(p1_ref[ii * tc + r], p2_ref[ii * tc + r], r, s)
            c1.start(priority=0)
            c2.start(priority=1)
            return carry
        lax.fori_loop(0, tc, body, 0, unroll=8)

    @pl.when(i == 0)
    def _():
        issue(0, 0)

    @pl.when(i < last)
    def _():
        issue(jnp.minimum(i + 1, last), 1 - slot)

    def drain(r, carry):
        c1, c2 = copies(0, 0, r, slot)
        c1.wait()
        c2.wait()
        return carry

    lax.fori_loop(0, tc, drain, 0, unroll=8)
    d = x_ref.shape[1]
    gate = _gate_tile(mod_ref, i, tc, tpb, lc, 5, 0, d, d)
    route = r_ref[...]
    f = route[:, 2:3] * y1_scr[slot] + route[:, 3:4] * y2_scr[slot]
    o_ref[...] = x_ref[...] + gate * f


def _combine(pos1, pos2, y, x, route, mod, t, lc):
    m, d = x.shape
    tc = 256
    return pl.pallas_call(
        functools.partial(_combine_kernel, tc=tc, tpb=t // tc, lc=lc),
        grid_spec=pltpu.PrefetchScalarGridSpec(
            num_scalar_prefetch=2,
            grid=(m // tc,),
            in_specs=[pl.BlockSpec(memory_space=pl.ANY),
                      pl.BlockSpec((tc, d), lambda i, p1, p2: (i, 0)),
                      pl.BlockSpec((tc, LANES), lambda i, p1, p2: (i, 0)),
                      pl.BlockSpec(mod.shape, lambda i, p1, p2: (0, 0))],
            out_specs=pl.BlockSpec((tc, d), lambda i, p1, p2: (i, 0)),
            scratch_shapes=[pltpu.VMEM((2, tc, d), F32), pltpu.VMEM((2, tc, d), F32),
                            pltpu.SemaphoreType.DMA((2,))]),
        out_shape=jax.ShapeDtypeStruct((m, d), F32),
        compiler_params=_params(("arbitrary",)),
        name="moe_combine",
    )(pos1, pos2, y, x, route, mod)


def _ffn_moe(x, mod, w_router, w1, w3, w2, layer, t, lc, ctx_out):
    m, d = x.shape
    tm = MOE_TM
    h, route = _router(x, mod, w_router, t, lc)
    experts = route[:, 0:TOP_K].astype(jnp.int32).reshape(-1)
    token = jnp.arange(TOP_K * m, dtype=jnp.int32) // TOP_K
    used = jnp.logical_or(ctx_out, token % t >= lc)
    onehot = ((experts[:, None] == jnp.arange(N_EXPERTS)[None, :]) & used[:, None]).astype(jnp.int32)
    rank = jnp.sum((jnp.cumsum(onehot, axis=0) - onehot) * onehot, axis=1)
    count = jnp.sum(onehot, axis=0)
    tiles = (count + tm - 1) // tm
    tile_end = jnp.cumsum(tiles)
    tile_start = tile_end - tiles
    n_tiles = (TOP_K * m + N_EXPERTS * (tm - 1)) // tm
    n_rows = n_tiles * tm
    pos = (tile_start * tm)[experts] + rank
    src = jnp.full((n_rows,), -1, jnp.int32).at[jnp.where(used, pos, n_rows)].set(token, mode='drop')
    pos = jnp.where(used, pos, token % tm)
    tile_ids = jnp.arange(n_tiles)
    tile_row = jnp.minimum(tile_ids, tile_end[-1] - 1)
    tile_expert = jnp.sum((tile_row[:, None] >= tile_end[None, :]).astype(jnp.int32), axis=1)
    tile_rows = jnp.clip(count[tile_expert] - (tile_ids - tile_start[tile_expert]) * tm, 0, tm)
    tile_rows = jnp.where(tile_ids < tile_end[-1], tile_rows, 0)
    hs = _gather_rows(src, h, n_rows)
    y = _moe_ffn(tile_expert.astype(jnp.int32), tile_row.astype(jnp.int32), tile_rows.astype(jnp.int32),
                 hs, w1, w3, w2, layer)
    pos = pos.reshape(m, TOP_K)
    return _combine(pos[:, 0], pos[:, 1], y, x, route, mod, t, lc)


def _rope_tables(n_lat, lc, rot_dim, lane0, period):
    tok = jnp.arange(n_lat)
    axis_dim = rot_dim // 2
    inv_freq = ROPE_BASE ** (-jnp.arange(0, axis_dim, 2, dtype=F32) / axis_dim)
    ang_r = (tok // GRID_W).astype(F32)[:, None] * inv_freq
    ang_c = (tok % GRID_W).astype(F32)[:, None] * inv_freq
    ang = jnp.concatenate([ang_r, ang_r, ang_c, ang_c], axis=-1)
    cos, sin = jnp.cos(ang), jnp.sin(ang)
    seg = rot_dim // 4
    lane = np.arange(LANES)
    rel = (lane - lane0) % period
    active = (lane >= lane0) & (rel < rot_dim)
    even = ((rel // seg) % 2 == 0)
    idx = np.where(active, rel, 0)
    cos_t = jnp.where(active[None, :], cos[:, idx], 1.0)
    sin_t = jnp.where(active[None, :], sin[:, idx], 0.0)
    sin_next = jnp.where(even[None, :], -sin_t, 0.0)
    sin_prev = jnp.where(even[None, :], 0.0, sin_t)
    ctx = lambda a, v: jnp.concatenate([jnp.full((lc, LANES), v, F32), a], axis=0)
    return ctx(cos_t, 1.0), ctx(sin_next, 0.0), ctx(sin_prev, 0.0)


def _relayout_kernel(w_ref, o_ref):
    rows = w_ref.shape[0]
    seg = lambda lo, n: w_ref[:, lo:lo + n].astype(BF16)
    z = lambda n: jnp.zeros((rows, n), BF16)
    a0, b0, g0, c0, d0 = 0, 1536, 3584, 3600, 4144
    parts = [seg(c0, 384), seg(c0 + 384, 128),
             seg(a0, 1536), seg(b0, 2048), seg(d0, 1536),
             seg(g0, 16), z(KR_LANE - 16), seg(c0 + 512, 32), z(LANES - KR_LANE - 32)]
    o_ref[...] = jnp.concatenate(parts, axis=1)


def _cast_kernel(w_ref, o_ref):
    o_ref[...] = w_ref[...].astype(BF16)


def _cast_bf16(w):
    nl, r, c = w.shape
    tr = 256
    return pl.pallas_call(
        _cast_kernel,
        grid=(nl, r // tr),
        in_specs=[pl.BlockSpec((None, tr, c), lambda l, i: (l, i, 0))],
        out_specs=pl.BlockSpec((None, tr, c), lambda l, i: (l, i, 0)),
        out_shape=jax.ShapeDtypeStruct(w.shape, BF16),
        compiler_params=_params(("arbitrary", "arbitrary")),
        name="weight_cast",
    )(w)


def _relayout_w_in(w_in):
    depth, d, n = w_in.shape
    tr = 256
    return pl.pallas_call(
        _relayout_kernel,
        grid=(depth, d // tr),
        in_specs=[pl.BlockSpec((None, tr, n), lambda l, i: (l, i, 0))],
        out_specs=pl.BlockSpec((None, tr, P_WIDTH), lambda l, i: (l, i, 0)),
        out_shape=jax.ShapeDtypeStruct((depth, d, P_WIDTH), BF16),
        compiler_params=_params(("arbitrary", "arbitrary")),
        name="w_in_relayout",
    )(w_in)


def _mla_weights(w_uq, w_ukv):
    hd = MLA_NOPE + MLA_ROPE
    wq = jnp.pad(w_uq.reshape(MLA_Q_RANK, MLA_HEADS, hd), ((0, 0), (0, 0), (0, LANES - hd)))
    wkv = w_ukv.reshape(MLA_KV_RANK, MLA_HEADS, MLA_NOPE + MLA_V)
    wk = jnp.pad(wkv[:, :, :MLA_NOPE], ((0, 0), (0, 0), (0, LANES - MLA_NOPE)))
    wv = wkv[:, :, MLA_NOPE:]
    return (wq.reshape(MLA_Q_RANK, -1).astype(BF16), wk.reshape(MLA_KV_RANK, -1).astype(BF16),
            wv.reshape(MLA_KV_RANK, -1).astype(BF16))


def kernel(x, c, ctx, c_ctx, w_ada, b_ada, w_in, w_out, da_q_gain, da_k_gain, da_lambda, da_out_gain, ml_conv_w, ml_conv_b, ml_gate_b, ml_out_gain, mla_cq_gain, mla_ckv_gain, mla_w_uq, mla_w_ukv, mla_q_gain, mla_k_gain, na_q_gain, na_k_gain, na_rpb, ffn_w1, ffn_w3, ffn_w2, moe_router, moe_w1, moe_w3, moe_w2):
    b, n_lat, d = x.shape
    lc = ctx.shape[1]
    t = lc + n_lat
    depth = w_in.shape[0]
    assert b <= 4 and lc == NA_QR * GRID_W and lc == TQ_DIFF == TQ_MLA and n_lat % lc == 0 and t % 768 == 0

    xs = jnp.concatenate([ctx, x], axis=1).reshape(b * t, d)
    cond = jnp.zeros((8, d), F32).at[:b].set(c).at[4].set(c_ctx)
    mod = _mod_table(cond, w_ada, b_ada)
    rope_da = _rope_tables(n_lat, lc, DA_QK, 0, DA_QK)
    rope_mla = _rope_tables(n_lat, lc, MLA_ROPE, KR_LANE, LANES)
    w_in_r = _relayout_w_in(w_in)
    w_out_b, ffn_w1_b, ffn_w3_b, ffn_w2_b = (_cast_bf16(w) for w in (w_out, ffn_w1, ffn_w3, ffn_w2))

    for l in range(depth):
        lam_init = 0.8 - 0.6 * math.exp(-0.3 * l)
        p = _inproj(xs, mod[l], w_in_r, l, t, lc)
        mix_a = _mixer_diff(p, rope_da, da_q_gain[l], da_k_gain[l], da_lambda[l], da_out_gain[l], lam_init, b, t, lc)
        mix_b = _mixer_mlstm(p, ml_conv_w[l], ml_conv_b[l], ml_gate_b[l], ml_out_gain[l], b, t, lc)
        wq, wk, wv = _mla_weights(mla_w_uq[l], mla_w_ukv[l])
        qc, kc, vc = _mla_prep(p, rope_mla, mla_cq_gain[l], mla_ckv_gain[l], wq, wk, wv,
                               mla_q_gain[l], mla_k_gain[l], t)
        mix_c = _mla_attn(qc, kc, vc, b, t, lc)
        mix_d = _mixer_na(p, na_q_gain[l], na_k_gain[l], na_rpb[l], b, t, lc)
        xs = _outproj((mix_a, mix_b, mix_c, mix_d), w_out_b, l, xs, mod[l], t, lc)
        if l % 2 == 0:
            xs = _ffn_dense(xs, mod[l], ffn_w1_b, ffn_w3_b, ffn_w2_b, l // 2, t, lc)
        else:
            xs = _ffn_moe(xs, mod[l], moe_router[l // 2], moe_w1, moe_w3, moe_w2, l // 2, t, lc, l < depth - 1)
    return xs.reshape(b, t, d)[:, lc:]
```
